```python
import jax
import jax.numpy as jnp
from jax import lax
import numpy as np

D_MODEL = 2048
BATCH = 4
SEQ = 2048
DEPTH = 4
DEC_BATCH = 8
DEC_SEQ = 4
PAST_LEN = 16384
PAGE_SIZE = 128

N_MIXERS = 3
N_POOL_LAYERS = (DEPTH + 2) // 3
N_CHUNK_LAYERS = (DEPTH + 1) // 3
N_ATTN_LAYERS = DEPTH // 3

D_FF = 5504
RMS_EPS = 1e-6

POOL_WINDOWS = (2, 4, 8, 16)
POOL_GROUP = D_MODEL // len(POOL_WINDOWS)
POOL_BUF = max(POOL_WINDOWS) - 1

CHUNK = 128
CHUNK_WIDTH = D_MODEL
CHUNK_GROUPS = 8
CHUNK_GROUP_W = CHUNK_WIDTH // CHUNK_GROUPS

ATTN_GROUPS = ((128, 1), (512, 4), (2048, 16))
N_ATTN_GROUPS = len(ATTN_GROUPS)
HEADS_PER_GROUP = 16
HEAD_DIM = 128
ROPE_THETA = 10000.0
ATTN_SCALE = HEAD_DIM ** -0.5
NEG = float(np.finfo(np.float32).min)

kernel_name = 'hybrid_pool_chunkgmlp_dilated_attn_decode_step'


def rms_norm(x, g):
    x32 = x.astype(jnp.float32)
    y = x32 * lax.rsqrt(jnp.mean(x32 * x32, axis=-1, keepdims=True) + RMS_EPS)
    return (y * g.astype(jnp.float32)).astype(x.dtype)


def swiglu(h, w_in, w_out):
    gate, up = jnp.split(h @ w_in, 2, axis=-1)
    return (jax.nn.silu(gate) * up) @ w_out


def pool_mixer(h, buf, start, w_pool, scale):
    B, T, _ = h.shape
    hb = jnp.concatenate([buf.astype(h.dtype), h], axis=1)
    hb32 = hb.astype(jnp.float32)
    cs = jnp.concatenate([jnp.zeros_like(hb32[:, :1]), jnp.cumsum(hb32, axis=1)], axis=1)
    pos = start + jnp.arange(T)
    h32 = h.astype(jnp.float32)
    hi = POOL_BUF + 1
    groups = []
    for g, w in enumerate(POOL_WINDOWS):
        sl = slice(g * POOL_GROUP, (g + 1) * POOL_GROUP)
        win_sum = cs[:, hi:hi + T, sl] - cs[:, hi - w:hi - w + T, sl]
        count = jnp.minimum(w, pos + 1).astype(jnp.float32)
        groups.append(win_sum / count[None, :, None] - h32[:, :, sl])
    pooled = jnp.stack(groups, axis=2)
    mixed = jnp.einsum('btgc,gcd->btgd', pooled, w_pool.astype(jnp.float32))
    out = mixed.reshape(B, T, D_MODEL) * scale.astype(jnp.float32)
    return out.astype(h.dtype), hb[:, -POOL_BUF:]


def chunk_proj(h, w_in, v_gain):
    u, v = jnp.split(jax.nn.gelu(h @ w_in, approximate=False), 2, axis=-1)
    return u, rms_norm(v, v_gain)


def causal_ws(w_s):
    mask = jnp.tril(jnp.ones((CHUNK, CHUNK), dtype=bool))
    return jnp.where(mask[None], w_s, jnp.zeros_like(w_s))


def chunk_mixer_prompt(h, w_in, v_gain, w_s, b_s, w_out):
    B, S, _ = h.shape
    u, v = chunk_proj(h, w_in, v_gain)
    vc = v.reshape(B, S // CHUNK, CHUNK, CHUNK_GROUPS, CHUNK_GROUP_W)
    mixed = jnp.einsum('bncgd,gqc->bnqgd', vc, causal_ws(w_s).astype(v.dtype))
    mixed = mixed + b_s.T.astype(v.dtype)[None, None, :, :, None]
    return (u * mixed.reshape(B, S, CHUNK_WIDTH)) @ w_out


def chunk_mixer_sample(h, w_in, v_gain, w_s, b_s, w_out):
    B, T, _ = h.shape
    u, v = chunk_proj(h, w_in, v_gain)
    ws = causal_ws(w_s)[:, :T, :T].astype(v.dtype)
    mixed = jnp.einsum('btgd,gqt->bqgd', v.reshape(B, T, CHUNK_GROUPS, CHUNK_GROUP_W), ws)
    mixed = mixed + b_s[:, :T].T.astype(v.dtype)[None, :, :, None]
    return (u * mixed.reshape(B, T, CHUNK_WIDTH)) @ w_out, v


def rope(x, pos):
    half = HEAD_DIM // 2
    freqs = ROPE_THETA ** (-2.0 * jnp.arange(half, dtype=jnp.float32) / HEAD_DIM)
    ang = pos.astype(jnp.float32)[:, None] * freqs[None, :]
    cos = jnp.cos(ang)[None, :, None, None, :]
    sin = jnp.sin(ang)[None, :, None, None, :]
    x32 = x.astype(jnp.float32)
    x1, x2 = x32[..., :half], x32[..., half:]
    return jnp.concatenate([x1 * cos - x2 * sin, x1 * sin + x2 * cos], axis=-1).astype(x.dtype)


def attn_qkv(h, pos, w_qkv, q_gain, k_gain):
    B, T, _ = h.shape
    qkv = (h @ w_qkv).reshape(B, T, 3, N_ATTN_GROUPS, HEADS_PER_GROUP, HEAD_DIM)
    q = rope(rms_norm(qkv[:, :, 0], q_gain), pos)
    k = rope(rms_norm(qkv[:, :, 1], k_gain), pos)
    return q, k, qkv[:, :, 2]


def dilated_attn_prompt(q, k, v, window, dil):
    B, S, H, E = q.shape
    R = window // dil
    n = S // dil
    nb = -(-n // R)
    n_pad = nb * R

    def to_blocks(x):
        x = x.reshape(B, n, dil, H, E).transpose(0, 2, 1, 3, 4)
        x = jnp.pad(x, ((0, 0), (0, 0), (0, n_pad - n), (0, 0), (0, 0)))
        return x.reshape(B, dil, nb, R, H, E)

    def with_prev(x):
        prev = jnp.pad(x, ((0, 0), (0, 0), (1, 0), (0, 0), (0, 0), (0, 0)))[:, :, :nb]
        return jnp.concatenate([prev, x], axis=3)

    qb = to_blocks(q)
    kk = with_prev(to_blocks(k))
    vv = with_prev(to_blocks(v))
    s = jnp.einsum('brnqhe,brnkhe->brnhqk', qb, kk, preferred_element_type=jnp.float32) * ATTN_SCALE
    qi = jnp.arange(R)[:, None]
    kj = jnp.arange(2 * R)[None, :]
    dist = R + qi - kj
    band = (dist >= 0) & (dist <= R)
    key_m = (jnp.arange(nb) * R - R)[:, None, None] + kj[None]
    mask = band[None] & (key_m >= 0)
    s = jnp.where(mask[None, None, :, None], s, NEG)
    lse = jax.nn.logsumexp(s, axis=-1)
    p = jnp.exp(s - lse[..., None])
    o = jnp.einsum('brnhqk,brnkhe->brnqhe', p, vv.astype(jnp.float32))
    o = o.reshape(B, dil, n_pad, H, E)[:, :, :n].transpose(0, 2, 1, 3, 4).reshape(B, S, H, E)
    lse = lse.transpose(0, 1, 2, 4, 3).reshape(B, dil, n_pad, H)[:, :, :n]
    lse = lse.transpose(0, 2, 1, 3).reshape(B, S, H)
    return o, lse


def dilated_attn_sample(q, k_new, v_new, cache_kv, window, dil):
    B, T, H, E = q.shape
    L = cache_kv.shape[1]
    R = window // dil
    k_all = jnp.concatenate([cache_kv[:, :, 0].astype(k_new.dtype), k_new], axis=1)
    v_all = jnp.concatenate([cache_kv[:, :, 1].astype(v_new.dtype), v_new], axis=1)
    idx = L + jnp.arange(T)[:, None] - dil * jnp.arange(R + 1)[None, :]
    valid = idx >= 0
    idx_c = jnp.clip(idx, 0, L + T - 1)
    kg = k_all[:, idx_c]
    vg = v_all[:, idx_c]
    s = jnp.einsum('bthe,btkhe->bthk', q, kg, preferred_element_type=jnp.float32) * ATTN_SCALE
    s = jnp.where(valid[None, :, None, :], s, NEG)
    lse = jax.nn.logsumexp(s, axis=-1)
    p = jnp.exp(s - lse[..., None])
    o = jnp.einsum('bthk,btkhe->bthe', p, vg.astype(jnp.float32))
    return o, lse


def combine_groups(outs, lses, w_out, dtype):
    alpha = jax.nn.softmax(jnp.stack(lses, axis=0), axis=0)
    o = jnp.einsum('gbth,gbthe->bthe', alpha, jnp.stack(outs, axis=0).astype(jnp.float32))
    B, T = o.shape[:2]
    return o.reshape(B, T, HEADS_PER_GROUP * HEAD_DIM).astype(dtype) @ w_out


def attn_mixer_prompt(h, w_qkv, q_gain, k_gain, w_out):
    B, S, _ = h.shape
    q, k, v = attn_qkv(h, jnp.arange(S), w_qkv, q_gain, k_gain)
    outs, lses, rows = [], [], []
    for g, (window, dil) in enumerate(ATTN_GROUPS):
        o, l = dilated_attn_prompt(q[:, :, g], k[:, :, g], v[:, :, g], window, dil)
        outs.append(o)
        lses.append(l)
        keep = min(window, S)
        rows.append(jnp.stack([k[:, S - keep:, g], v[:, S - keep:, g]], axis=2))
    return combine_groups(outs, lses, w_out, h.dtype), rows


def attn_mixer_sample(h, caches, w_qkv, q_gain, k_gain, w_out):
    B, T, _ = h.shape
    q, k, v = attn_qkv(h, PAST_LEN + jnp.arange(T), w_qkv, q_gain, k_gain)
    outs, lses, rows = [], [], []
    for g, (window, dil) in enumerate(ATTN_GROUPS):
        o, l = dilated_attn_sample(q[:, :, g], k[:, :, g], v[:, :, g], caches[g], window, dil)
        outs.append(o)
        lses.append(l)
        rows.append(jnp.stack([k[:, :, g], v[:, :, g]], axis=2))
    return combine_groups(outs, lses, w_out, h.dtype), rows


def setup_inputs(seed: int = 0) -> dict:
    key = jax.random.key(seed)
    ks = jax.random.split(key, 32)
    f32 = jnp.float32

    def nrm(k, shape, scale):
        return jax.random.normal(k, shape, f32) * scale

    def gain(k, shape, noise):
        return 1.0 + noise * jax.random.normal(k, shape, f32)

    attn_w = N_ATTN_GROUPS * HEADS_PER_GROUP * HEAD_DIM
    inp = {}
    inp['x_prompt'] = nrm(ks[0], (BATCH, SEQ, D_MODEL), 1.0)
    inp['x_sample'] = nrm(ks[1], (DEC_BATCH, DEC_SEQ, D_MODEL), 1.0)
    inp['state_pool'] = nrm(ks[2], (N_POOL_LAYERS, DEC_BATCH, POOL_BUF, D_MODEL), 1.0)
    for g, (window, _) in enumerate(ATTN_GROUPS):
        keep = min(window, PAST_LEN)
        inp['cache_kv_g%d' % g] = nrm(ks[3 + g], (N_ATTN_LAYERS, DEC_BATCH, keep, 2, HEADS_PER_GROUP, HEAD_DIM), 1.0)
    inp['norm_ffn1'] = gain(ks[6], (DEPTH, D_MODEL), 0.05)
    inp['ffn1_w_in'] = nrm(ks[7], (DEPTH, D_MODEL, 2 * D_FF), D_MODEL ** -0.5)
    inp['ffn1_w_out'] = nrm(ks[8], (DEPTH, D_FF, D_MODEL), D_FF ** -0.5)
    inp['norm_mix'] = gain(ks[9], (DEPTH, D_MODEL), 0.05)
    inp['norm_ffn2'] = gain(ks[10], (DEPTH, D_MODEL), 0.05)
    inp['ffn2_w_in'] = nrm(ks[11], (DEPTH, D_MODEL, 2 * D_FF), D_MODEL ** -0.5)
    inp['ffn2_w_out'] = nrm(ks[12], (DEPTH, D_FF, D_MODEL), D_FF ** -0.5)
    inp['pool_w'] = nrm(ks[13], (N_POOL_LAYERS, len(POOL_WINDOWS), POOL_GROUP, POOL_GROUP), POOL_GROUP ** -0.5)
    inp['pool_scale'] = gain(ks[14], (N_POOL_LAYERS, D_MODEL), 0.1)
    inp['chunk_w_in'] = nrm(ks[15], (N_CHUNK_LAYERS, D_MODEL, 2 * CHUNK_WIDTH), D_MODEL ** -0.5)
    inp['chunk_v_norm'] = gain(ks[16], (N_CHUNK_LAYERS, CHUNK_WIDTH), 0.05)
    inp['chunk_w_s'] = nrm(ks[17], (N_CHUNK_LAYERS, CHUNK_GROUPS, CHUNK, CHUNK), CHUNK ** -0.5)
    inp['chunk_b_s'] = gain(ks[18], (N_CHUNK_LAYERS, CHUNK_GROUPS, CHUNK), 0.1)
    inp['chunk_w_out'] = nrm(ks[19], (N_CHUNK_LAYERS, CHUNK_WIDTH, D_MODEL), CHUNK_WIDTH ** -0.5)
    inp['attn_w_qkv'] = nrm(ks[20], (N_ATTN_LAYERS, D_MODEL, 3 * attn_w), D_MODEL ** -0.5)
    inp['attn_q_norm'] = gain(ks[21], (N_ATTN_LAYERS, HEAD_DIM), 0.05)
    inp['attn_k_norm'] = gain(ks[22], (N_ATTN_LAYERS, HEAD_DIM), 0.05)
    inp['attn_w_out'] = nrm(ks[23], (N_ATTN_LAYERS, HEADS_PER_GROUP * HEAD_DIM, D_MODEL), (HEADS_PER_GROUP * HEAD_DIM) ** -0.5)
    return inp


def reference(x_prompt, x_sample, state_pool, cache_kv_g0, cache_kv_g1, cache_kv_g2,
              norm_ffn1, ffn1_w_in, ffn1_w_out, norm_mix, norm_ffn2, ffn2_w_in, ffn2_w_out,
              pool_w, pool_scale,
              chunk_w_in, chunk_v_norm, chunk_w_s, chunk_b_s, chunk_w_out,
              attn_w_qkv, attn_q_norm, attn_k_norm, attn_w_out):
    caches = (cache_kv_g0, cache_kv_g1, cache_kv_g2)
    xp, xs = x_prompt, x_sample
    pool_p, pool_s, chunk_s = [], [], []
    kv_p = [[] for _ in ATTN_GROUPS]
    kv_s = [[] for _ in ATTN_GROUPS]
    for i in range(DEPTH):
        kind, j = i % N_MIXERS, i // N_MIXERS
        xp = xp + 0.5 * swiglu(rms_norm(xp, norm_ffn1[i]), ffn1_w_in[i], ffn1_w_out[i])
        xs = xs + 0.5 * swiglu(rms_norm(xs, norm_ffn1[i]), ffn1_w_in[i], ffn1_w_out[i])
        hp = rms_norm(xp, norm_mix[i])
        hs = rms_norm(xs, norm_mix[i])
        if kind == 0:
            zero_buf = jnp.zeros((hp.shape[0], POOL_BUF, D_MODEL), hp.dtype)
            mp, st_p = pool_mixer(hp, zero_buf, 0, pool_w[j], pool_scale[j])
            ms, st_s = pool_mixer(hs, state_pool[j], PAST_LEN, pool_w[j], pool_scale[j])
            pool_p.append(st_p)
            pool_s.append(st_s)
        elif kind == 1:
            mp = chunk_mixer_prompt(hp, chunk_w_in[j], chunk_v_norm[j], chunk_w_s[j], chunk_b_s[j], chunk_w_out[j])
            ms, v_new = chunk_mixer_sample(hs, chunk_w_in[j], chunk_v_norm[j], chunk_w_s[j], chunk_b_s[j], chunk_w_out[j])
            chunk_s.append(v_new)
        else:
            mp, rows_p = attn_mixer_prompt(hp, attn_w_qkv[j], attn_q_norm[j], attn_k_norm[j], attn_w_out[j])
            ms, rows_s = attn_mixer_sample(hs, tuple(c[j] for c in caches), attn_w_qkv[j], attn_q_norm[j], attn_k_norm[j], attn_w_out[j])
            for g in range(N_ATTN_GROUPS):
                kv_p[g].append(rows_p[g])
                kv_s[g].append(rows_s[g])
        xp = xp + mp
        xs = xs + ms
        xp = xp + 0.5 * swiglu(rms_norm(xp, norm_ffn2[i]), ffn2_w_in[i], ffn2_w_out[i])
        xs = xs + 0.5 * swiglu(rms_norm(xs, norm_ffn2[i]), ffn2_w_in[i], ffn2_w_out[i])
    return (xp, xs, jnp.stack(pool_p), jnp.stack(pool_s), jnp.stack(chunk_s),
            jnp.stack(kv_p[0]), jnp.stack(kv_s[0]), jnp.stack(kv_p[1]), jnp.stack(kv_s[1]),
            jnp.stack(kv_p[2]), jnp.stack(kv_s[2]))
```

```python
import functools

import numpy as np
import jax
import jax.numpy as jnp
from jax import lax
from jax.experimental import pallas as pl
from jax.experimental.pallas import tpu as pltpu

F32 = jnp.float32
BF16 = jnp.bfloat16

D_MODEL = 2048
D_FF = 5504
RMS_EPS = 1e-6
POOL_WINDOWS = (2, 4, 8, 16)
POOL_GROUP = D_MODEL // len(POOL_WINDOWS)
POOL_BUF = max(POOL_WINDOWS) - 1
CHUNK = 128
CHUNK_GROUPS = 8
CHUNK_GROUP_W = D_MODEL // CHUNK_GROUPS
ATTN_GROUPS = ((128, 1), (512, 4), (2048, 16))
HEADS = 16
HEAD_DIM = 128
ATTN_W = HEADS * HEAD_DIM
ROPE_THETA = 10000.0
ATTN_SCALE = HEAD_DIM ** -0.5
NEG = float(np.finfo(np.float32).min)
PAST_LEN = 16384

LANE = 128
FF_TILE = 512
D_FF_PAD = -(-D_FF // FF_TILE) * FF_TILE
ROW_TILE = 512
CHUNK_ROW_TILE = 256
COMBINE_ROW_TILE = 256
ATTN_BLOCK = 128
MIB = 1024 * 1024


def _params(semantics, vmem_mib):
    return pltpu.CompilerParams(dimension_semantics=semantics, vmem_limit_bytes=vmem_mib * MIB)


def _rms(x, g):
    ms = jnp.mean(x * x, axis=-1, keepdims=True)
    return x * lax.rsqrt(ms + RMS_EPS) * g


def _dot(a, b):
    return jnp.dot(a, b, preferred_element_type=F32)


def _dot_nt(a, b):
    return lax.dot_general(a, b, (((1,), (1,)), ((), ())), preferred_element_type=F32)


def _ffn_body(xp_ref, xs_ref, g_ref, wg_ref, wu_ref, wo_ref, yp_ref, ys_ref, hp_ref, hs_ref):
    m = pl.program_id(0)
    f = pl.program_id(1)

    def init(x_ref, h_ref, y_ref):
        x = x_ref[...]
        h_ref[...] = _rms(x, g_ref[...]).astype(BF16)
        y_ref[...] = x

    def accumulate(h_ref, y_ref):
        h = h_ref[...]
        gate = _dot(h, wg_ref[...])
        up = _dot(h, wu_ref[...])
        hid = (gate * jax.nn.sigmoid(gate) * up * 0.5).astype(BF16)
        y_ref[...] += _dot(hid, wo_ref[...])

    @pl.when(f == 0)
    def _():
        init(xp_ref, hp_ref, yp_ref)

    @pl.when((f == 0) & (m == 0))
    def _():
        init(xs_ref, hs_ref, ys_ref)

    accumulate(hp_ref, yp_ref)

    @pl.when(m == 0)
    def _():
        accumulate(hs_ref, ys_ref)


def _ffn(xp, xs, g, w_in, w_out):
    mp, ms = xp.shape[0], xs.shape[0]
    nf = D_FF_PAD // FF_TILE
    pad = D_FF_PAD - D_FF
    wg = jnp.pad(w_in[:, :D_FF], ((0, 0), (0, pad))).astype(BF16)
    wu = jnp.pad(w_in[:, D_FF:], ((0, 0), (0, pad))).astype(BF16)
    wo = jnp.pad(w_out, ((0, pad), (0, 0))).astype(BF16)
    return pl.pallas_call(
        _ffn_body,
        grid=(mp // ROW_TILE, nf),
        in_specs=[
            pl.BlockSpec((ROW_TILE, D_MODEL), lambda m, f: (m, 0)),
            pl.BlockSpec((ms, D_MODEL), lambda m, f: (0, 0)),
            pl.BlockSpec((1, D_MODEL), lambda m, f: (0, 0)),
            pl.BlockSpec((D_MODEL, FF_TILE), lambda m, f: (0, f)),
            pl.BlockSpec((D_MODEL, FF_TILE), lambda m, f: (0, f)),
            pl.BlockSpec((FF_TILE, D_MODEL), lambda m, f: (f, 0)),
        ],
        out_specs=[
            pl.BlockSpec((ROW_TILE, D_MODEL), lambda m, f: (m, 0)),
            pl.BlockSpec((ms, D_MODEL), lambda m, f: (0, 0)),
        ],
        out_shape=[jax.ShapeDtypeStruct(xp.shape, F32), jax.ShapeDtypeStruct(xs.shape, F32)],
        scratch_shapes=[pltpu.VMEM((ROW_TILE, D_MODEL), BF16), pltpu.VMEM((ms, D_MODEL), BF16)],
        compiler_params=_params(("arbitrary", "arbitrary"), 48),
        name="ffn",
    )(xp, xs, g.reshape(1, D_MODEL), wg, wu, wo)


def _pool_body(x_ref, buf_ref, g_ref, pw_ref, sc_ref, y_ref, st_ref, hb_ref, *, tt, start, nt):
    t = pl.program_id(1)
    halo = POOL_BUF + 1

    @pl.when(t == 0)
    def _():
        hb_ref[0:halo, :] = buf_ref[0]

    x = x_ref[0]
    h = _rms(x, g_ref[...])
    hb_ref[halo:halo + tt, :] = h
    pos = start + t * tt + lax.broadcasted_iota(jnp.int32, (tt, 1), 0)
    for gi, w in enumerate(POOL_WINDOWS):
        sl = slice(gi * POOL_GROUP, (gi + 1) * POOL_GROUP)
        win = hb_ref[halo:halo + tt, sl]
        for k in range(1, w):
            win = win + hb_ref[halo - k:halo - k + tt, sl]
        count = jnp.minimum(w, pos + 1).astype(F32)
        pooled = win / count - h[:, sl]
        mixed = _dot(pooled.astype(BF16), pw_ref[gi])
        y_ref[0, :, sl] = x[:, sl] + mixed * sc_ref[:, sl]
    st_ref[0] = hb_ref[tt + 1:tt + halo, :]
    if nt > 1:
        hb_ref[0:halo, :] = hb_ref[tt:tt + halo, :]


def _pool(x, buf, g, pool_w, scale, start):
    b, t_len, _ = x.shape
    tt = min(t_len, ROW_TILE)
    nt = t_len // tt
    halo = POOL_BUF + 1
    buf16 = jnp.pad(buf, ((0, 0), (1, 0), (0, 0)))
    body = functools.partial(_pool_body, tt=tt, start=start, nt=nt)
    return pl.pallas_call(
        body,
        grid=(b, nt),
        in_specs=[
            pl.BlockSpec((1, tt, D_MODEL), lambda i, t: (i, t, 0)),
            pl.BlockSpec((1, halo, D_MODEL), lambda i, t: (i, 0, 0)),
            pl.BlockSpec((1, D_MODEL), lambda i, t: (0, 0)),
            pl.BlockSpec((len(POOL_WINDOWS), POOL_GROUP, POOL_GROUP), lambda i, t: (0, 0, 0)),
            pl.BlockSpec((1, D_MODEL), lambda i, t: (0, 0)),
        ],
        out_specs=[
            pl.BlockSpec((1, tt, D_MODEL), lambda i, t: (i, t, 0)),
            pl.BlockSpec((1, POOL_BUF, D_MODEL), lambda i, t: (i, 0, 0)),
        ],
        out_shape=[jax.ShapeDtypeStruct(x.shape, F32),
                   jax.ShapeDtypeStruct((b, POOL_BUF, D_MODEL), F32)],
        scratch_shapes=[pltpu.VMEM((halo + tt, D_MODEL), F32)],
        compiler_params=_params(("arbitrary", "arbitrary"), 40),
        name="pool",
    )(x, buf16, g.reshape(1, D_MODEL), pool_w.astype(BF16), scale.reshape(1, D_MODEL))


def _chunk_body(x_ref, g_ref, win_ref, vg_ref, ws_ref, bst_ref, wout_ref, *rest, tm, emit_v):
    if emit_v:
        y_ref, vn_ref, mix_ref = rest
    else:
        y_ref, mix_ref = rest
    x = x_ref[...]
    h = _rms(x, g_ref[...]).astype(BF16)
    uv = _dot(h, win_ref[...])
    uv = 0.5 * uv * (1.0 + lax.erf(uv * np.float32(np.sqrt(0.5))))
    u = uv[:, :D_MODEL]
    vn = _rms(uv[:, D_MODEL:], vg_ref[...])
    if emit_v:
        vn_ref[...] = vn
    q_idx = lax.broadcasted_iota(jnp.int32, (CHUNK, CHUNK), 0)
    c_idx = lax.broadcasted_iota(jnp.int32, (CHUNK, CHUNK), 1)
    causal = c_idx <= q_idx
    for gi in range(CHUNK_GROUPS):
        cols = slice(gi * CHUNK_GROUP_W, (gi + 1) * CHUNK_GROUP_W)
        ws = jnp.where(causal, ws_ref[gi], 0.0).astype(BF16)
        bias = bst_ref[:, gi:gi + 1]
        for c in range(tm // CHUNK):
            rows = slice(c * CHUNK, (c + 1) * CHUNK)
            mixed = _dot(ws, vn[rows, cols].astype(BF16)) + bias
            mix_ref[rows, cols] = (u[rows, cols] * mixed).astype(BF16)
    y_ref[...] = x + _dot(mix_ref[...], wout_ref[...])


def _chunk(x, g, w_in, v_gain, w_s, b_s, w_out, emit_v):
    m = x.shape[0]
    tm = CHUNK_ROW_TILE
    body = functools.partial(_chunk_body, tm=tm, emit_v=emit_v)
    row_spec = pl.BlockSpec((tm, D_MODEL), lambda i: (i, 0))
    once = pl.Buffered(1)
    out_specs = [row_spec]
    out_shape = [jax.ShapeDtypeStruct((m, D_MODEL), F32)]
    if emit_v:
        out_specs.append(row_spec)
        out_shape.append(jax.ShapeDtypeStruct((m, D_MODEL), F32))
    return pl.pallas_call(
        body,
        grid=(m // tm,),
        in_specs=[
            row_spec,
            pl.BlockSpec((1, D_MODEL), lambda i: (0, 0)),
            pl.BlockSpec((D_MODEL, 2 * D_MODEL), lambda i: (0, 0), pipeline_mode=once),
            pl.BlockSpec((1, D_MODEL), lambda i: (0, 0)),
            pl.BlockSpec((CHUNK_GROUPS, CHUNK, CHUNK), lambda i: (0, 0, 0)),
            pl.BlockSpec((CHUNK, CHUNK_GROUPS), lambda i: (0, 0)),
            pl.BlockSpec((D_MODEL, D_MODEL), lambda i: (0, 0), pipeline_mode=once),
        ],
        out_specs=out_specs,
        out_shape=out_shape,
        scratch_shapes=[pltpu.VMEM((tm, D_MODEL), BF16)],
        compiler_params=_params(("arbitrary",), 56),
        name="chunk",
    )(x, g.reshape(1, D_MODEL), w_in.astype(BF16), v_gain.reshape(1, D_MODEL), w_s, b_s.T,
      w_out.astype(BF16))


def _rope_tables(pos):
    half = HEAD_DIM // 2
    freqs = ROPE_THETA ** (-2.0 * jnp.arange(half, dtype=F32) / HEAD_DIM)
    ang = pos.astype(F32)[:, None] * freqs[None, :]
    cos, sin = jnp.cos(ang), jnp.sin(ang)
    return jnp.concatenate([cos, cos], axis=-1), jnp.concatenate([-sin, sin], axis=-1)


def _qkv_body(x_ref, g_ref, w_ref, qn_ref, kn_ref, cos_ref, sin_ref, q_ref, kv_ref, h_ref):
    s = pl.program_id(1)

    @pl.when(s == 0)
    def _():
        h_ref[...] = _rms(x_ref[...], g_ref[...]).astype(BF16)

    y = _dot(h_ref[...], w_ref[...])

    def norm_rope(gain_ref, out_ref, out_scale):
        cos = cos_ref[...]
        sin = sin_ref[...]
        for hd in range(HEADS):
            cols = slice(hd * HEAD_DIM, (hd + 1) * HEAD_DIM)
            yn = _rms(y[:, cols], gain_ref[...])
            rot = pltpu.roll(yn, HEAD_DIM // 2, 1)
            out = yn * cos + rot * sin
            if out_scale is not None:
                out = out * out_scale
            out_ref[:, cols] = out.astype(out_ref.dtype)

    @pl.when(s == 0)
    def _():
        norm_rope(qn_ref, q_ref, ATTN_SCALE)

    @pl.when(s == 1)
    def _():
        norm_rope(kn_ref, kv_ref, None)

    @pl.when(s == 2)
    def _():
        kv_ref[...] = y


def _qkv(x, g, w_qkv, q_gain, k_gain, cos, sin, group):
    m = x.shape[0]
    tm = min(m, ROW_TILE)
    n_tab = cos.shape[0] // tm
    n_groups = len(ATTN_GROUPS)
    return pl.pallas_call(
        _qkv_body,
        grid=(m // tm, 3),
        in_specs=[
            pl.BlockSpec((tm, D_MODEL), lambda i, s: (i, 0)),
            pl.BlockSpec((1, D_MODEL), lambda i, s: (0, 0)),
            pl.BlockSpec((D_MODEL, ATTN_W), lambda i, s: (0, s * n_groups + group)),
            pl.BlockSpec((1, HEAD_DIM), lambda i, s: (0, 0)),
            pl.BlockSpec((1, HEAD_DIM), lambda i, s: (0, 0)),
            pl.BlockSpec((tm, HEAD_DIM), lambda i, s: (i % n_tab, 0)),
            pl.BlockSpec((tm, HEAD_DIM), lambda i, s: (i % n_tab, 0)),
        ],
        out_specs=[
            pl.BlockSpec((tm, ATTN_W), lambda i, s: (i, 0)),
            pl.BlockSpec((tm, ATTN_W), lambda i, s: (i, jnp.maximum(s - 1, 0))),
        ],
        out_shape=[jax.ShapeDtypeStruct((m, ATTN_W), BF16),
                   jax.ShapeDtypeStruct((m, 2 * ATTN_W), F32)],
        scratch_shapes=[pltpu.VMEM((tm, D_MODEL), BF16)],
        compiler_params=_params(("arbitrary", "arbitrary"), 48),
        name="qkv",
    )(x, g.reshape(1, D_MODEL), w_qkv, q_gain.reshape(1, HEAD_DIM), k_gain.reshape(1, HEAD_DIM),
      cos, sin)


def _band_attn_body(q_ref, kp_ref, kc_ref, vp_ref, vc_ref, o_ref, lse_ref):
    blk = pl.program_id(2)
    r = ATTN_BLOCK
    qi = lax.broadcasted_iota(jnp.int32, (r, 2 * r), 0)
    kj = lax.broadcasted_iota(jnp.int32, (r, 2 * r), 1)
    dist = r + qi - kj
    mask = (dist >= 0) & (dist <= r) & ((kj >= r) | (blk > 0))
    for hd in range(HEADS):
        cols = slice(hd * HEAD_DIM, (hd + 1) * HEAD_DIM)
        k = jnp.concatenate([kp_ref[0, :, cols], kc_ref[0, :, cols]], axis=0).astype(BF16)
        v = jnp.concatenate([vp_ref[0, :, cols], vc_ref[0, :, cols]], axis=0).astype(BF16)
        s = jnp.where(mask, _dot_nt(q_ref[0, :, cols], k), NEG)
        mx = jnp.max(s, axis=-1, keepdims=True)
        p = jnp.exp(s - mx)
        den = jnp.sum(p, axis=-1, keepdims=True)
        o_ref[0, :, cols] = _dot(p.astype(BF16), v) / den
        lse_ref[0, :, cols] = jnp.broadcast_to(mx + jnp.log(den), (r, HEAD_DIM))


def _band_attn(q, kv, batch, dil):
    m = q.shape[0]
    n = m // batch // dil
    nb = n // ATTN_BLOCK
    qv = q.reshape(batch, n, dil * ATTN_W)
    kvv = kv.reshape(batch, n, dil * 2 * ATTN_W)
    blk = (1, ATTN_BLOCK, ATTN_W)
    prev = lambda j: jnp.maximum(j - 1, 0)
    o, lse = pl.pallas_call(
        _band_attn_body,
        grid=(batch, dil, nb),
        in_specs=[
            pl.BlockSpec(blk, lambda b, r, j: (b, j, r)),
            pl.BlockSpec(blk, lambda b, r, j: (b, prev(j), 2 * r)),
            pl.BlockSpec(blk, lambda b, r, j: (b, j, 2 * r)),
            pl.BlockSpec(blk, lambda b, r, j: (b, prev(j), 2 * r + 1)),
            pl.BlockSpec(blk, lambda b, r, j: (b, j, 2 * r + 1)),
        ],
        out_specs=[pl.BlockSpec(blk, lambda b, r, j: (b, j, r))] * 2,
        out_shape=[jax.ShapeDtypeStruct(qv.shape, F32)] * 2,
        compiler_params=_params(("arbitrary",) * 3, 32),
        name="band_attn",
    )(qv, kvv, kvv, kvv, kvv)
    return o.reshape(m, ATTN_W), lse.reshape(m, ATTN_W)


def _cache_attn_body(q_ref, new_ref, cache_ref, o_ref, lse_ref, *, dil, t_len):
    r = ATTN_BLOCK
    lane = lax.broadcasted_iota(jnp.int32, (8, r), 1)
    row = lax.broadcasted_iota(jnp.int32, (t_len, 1), 0)
    for t in range(t_len):
        base = (t if dil > 1 else 0) * 2 * ATTN_W
        for hd in range(HEADS):
            cols = slice(hd * HEAD_DIM, (hd + 1) * HEAD_DIM)
            kcols = slice(base + hd * HEAD_DIM, base + (hd + 1) * HEAD_DIM)
            vcols = slice(base + ATTN_W + hd * HEAD_DIM, base + ATTN_W + (hd + 1) * HEAD_DIM)
            q = q_ref[0, t:t + 1, cols]
            q8 = jnp.broadcast_to(q, (8, HEAD_DIM))
            s_c = _dot_nt(q8, cache_ref[0, :, kcols].astype(BF16))
            s_n = jnp.sum(new_ref[0, :, cols] * q.astype(F32), axis=-1, keepdims=True)
            if dil == 1:
                s_c = jnp.where(lane >= t, s_c, NEG)
                s_n = jnp.where(row <= t, s_n, NEG)
            else:
                s_n = jnp.where(row == t, s_n, NEG)
            mx = jnp.maximum(jnp.max(s_c, axis=-1, keepdims=True)[0:1],
                             jnp.max(s_n, axis=0, keepdims=True))
            p_c = jnp.exp(s_c - mx)
            p_n = jnp.exp(s_n - mx)
            den = jnp.sum(p_c, axis=-1, keepdims=True)[0:1] + jnp.sum(p_n, axis=0, keepdims=True)
            acc = _dot(p_c.astype(BF16), cache_ref[0, :, vcols].astype(BF16))[0:1]
            acc = acc + jnp.sum(p_n * new_ref[0, :, ATTN_W + hd * HEAD_DIM:ATTN_W + (hd + 1) * HEAD_DIM],
                                axis=0, keepdims=True)
            o_ref[0, t:t + 1, cols] = acc / den
            lse_ref[0, t:t + 1, cols] = jnp.broadcast_to(mx + jnp.log(den), (1, HEAD_DIM))


def _cache_attn(q, kv_new, cache, window, dil, batch):
    t_len = q.shape[0] // batch
    length = cache.shape[1]
    assert length == window and length // dil == ATTN_BLOCK and (dil == 1 or t_len <= dil)
    n_res = t_len if dil > 1 else 1
    cv = cache.reshape(batch, ATTN_BLOCK, dil * 2 * ATTN_W)
    body = functools.partial(_cache_attn_body, dil=dil, t_len=t_len)
    o, lse = pl.pallas_call(
        body,
        grid=(batch,),
        in_specs=[
            pl.BlockSpec((1, t_len, ATTN_W), lambda b: (b, 0, 0)),
            pl.BlockSpec((1, t_len, 2 * ATTN_W), lambda b: (b, 0, 0)),
            pl.BlockSpec((1, ATTN_BLOCK, n_res * 2 * ATTN_W), lambda b: (b, 0, 0)),
        ],
        out_specs=[pl.BlockSpec((1, t_len, ATTN_W), lambda b: (b, 0, 0))] * 2,
        out_shape=[jax.ShapeDtypeStruct((batch, t_len, ATTN_W), F32)] * 2,
        compiler_params=_params(("arbitrary",), 40),
        name="cache_attn",
    )(q.reshape(batch, t_len, ATTN_W), kv_new.reshape(batch, t_len, 2 * ATTN_W), cv)
    return o.reshape(batch * t_len, ATTN_W), lse.reshape(batch * t_len, ATTN_W)


def _combine_body(x_ref, o0_ref, o1_ref, o2_ref, l0_ref, l1_ref, l2_ref, w_ref, y_ref):
    l0, l1, l2 = l0_ref[...], l1_ref[...], l2_ref[...]
    mx = jnp.maximum(jnp.maximum(l0, l1), l2)
    e0, e1, e2 = jnp.exp(l0 - mx), jnp.exp(l1 - mx), jnp.exp(l2 - mx)
    o = (e0 * o0_ref[...] + e1 * o1_ref[...] + e2 * o2_ref[...]) / (e0 + e1 + e2)
    y_ref[...] = x_ref[...] + _dot(o.astype(BF16), w_ref[...])


def _combine(x, outs, lses, w_out):
    m = x.shape[0]
    tm = min(m, COMBINE_ROW_TILE)
    row_spec = pl.BlockSpec((tm, D_MODEL), lambda i: (i, 0))
    return pl.pallas_call(
        _combine_body,
        grid=(m // tm,),
        in_specs=[row_spec] * 7 + [pl.BlockSpec((ATTN_W, D_MODEL), lambda i: (0, 0))],
        out_specs=row_spec,
        out_shape=jax.ShapeDtypeStruct((m, D_MODEL), F32),
        compiler_params=_params(("arbitrary",), 56),
        name="attn_combine",
    )(x, *outs, *lses, w_out)


def kernel(x_prompt, x_sample, state_pool, cache_kv_g0, cache_kv_g1, cache_kv_g2, norm_ffn1, ffn1_w_in, ffn1_w_out, norm_mix, norm_ffn2, ffn2_w_in, ffn2_w_out, pool_w, pool_scale, chunk_w_in, chunk_v_norm, chunk_w_s, chunk_b_s, chunk_w_out, attn_w_qkv, attn_q_norm, attn_k_norm, attn_w_out):
    caches = (cache_kv_g0, cache_kv_g1, cache_kv_g2)
    batch, seq, _ = x_prompt.shape
    dec_batch, dec_seq, _ = x_sample.shape
    depth = norm_ffn1.shape[0]
    xp = x_prompt.reshape(batch * seq, D_MODEL)
    xs = x_sample.reshape(dec_batch * dec_seq, D_MODEL)
    pool_p, pool_s, chunk_s = [], [], []
    kv_p = [[] for _ in ATTN_GROUPS]
    kv_s = [[] for _ in ATTN_GROUPS]
    for i in range(depth):
        kind, j = i % 3, i // 3
        xp, xs = _ffn(xp, xs, norm_ffn1[i], ffn1_w_in[i], ffn1_w_out[i])
        if kind == 0:
            zero_buf = jnp.zeros((batch, POOL_BUF, D_MODEL), F32)
            yp, st_p = _pool(xp.reshape(batch, seq, D_MODEL), zero_buf, norm_mix[i], pool_w[j],
                             pool_scale[j], 0)
            ys, st_s = _pool(xs.reshape(dec_batch, dec_seq, D_MODEL), state_pool[j], norm_mix[i],
                             pool_w[j], pool_scale[j], PAST_LEN)
            xp = yp.reshape(batch * seq, D_MODEL)
            xs = ys.reshape(dec_batch * dec_seq, D_MODEL)
            pool_p.append(st_p)
            pool_s.append(st_s)
        elif kind == 1:
            args = (norm_mix[i], chunk_w_in[j], chunk_v_norm[j], chunk_w_s[j], chunk_b_s[j],
                    chunk_w_out[j])
            (xp,) = _chunk(xp, *args, emit_v=False)
            xs_pad = jnp.pad(xs.reshape(dec_batch, dec_seq, D_MODEL),
                             ((0, 0), (0, CHUNK - dec_seq), (0, 0)))
            ys_pad, v_pad = _chunk(xs_pad.reshape(dec_batch * CHUNK, D_MODEL), *args, emit_v=True)
            xs = ys_pad.reshape(dec_batch, CHUNK, D_MODEL)[:, :dec_seq].reshape(-1, D_MODEL)
            chunk_s.append(v_pad.reshape(dec_batch, CHUNK, D_MODEL)[:, :dec_seq])
        else:
            w_qkv = attn_w_qkv[j].astype(BF16)
            w_o = attn_w_out[j].astype(BF16)
            cos_p, sin_p = _rope_tables(jnp.arange(seq))
            cos_s, sin_s = _rope_tables(jnp.tile(PAST_LEN + jnp.arange(dec_seq), dec_batch))
            outs_p, lses_p, outs_s, lses_s = [], [], [], []
            for gi, (window, dil) in enumerate(ATTN_GROUPS):
                q_p, kvn_p = _qkv(xp, norm_mix[i], w_qkv, attn_q_norm[j], attn_k_norm[j], cos_p, sin_p, gi)
                o, l = _band_attn(q_p, kvn_p, batch, dil)
                outs_p.append(o)
                lses_p.append(l)
                keep = min(window, seq)
                kv_p[gi].append(kvn_p.reshape(batch, seq, 2, HEADS, HEAD_DIM)[:, seq - keep:])
                q_s, kvn_s = _qkv(xs, norm_mix[i], w_qkv, attn_q_norm[j], attn_k_norm[j], cos_s, sin_s, gi)
                o, l = _cache_attn(q_s, kvn_s, caches[gi][j], window, dil, dec_batch)
                outs_s.append(o)
                lses_s.append(l)
                kv_s[gi].append(kvn_s.reshape(dec_batch, dec_seq, 2, HEADS, HEAD_DIM))
            xp = _combine(xp, outs_p, lses_p, w_o)
            xs = _combine(xs, outs_s, lses_s, w_o)
        xp, xs = _ffn(xp, xs, norm_ffn2[i], ffn2_w_in[i], ffn2_w_out[i])
    return (xp.reshape(batch, seq, D_MODEL), xs.reshape(dec_batch, dec_seq, D_MODEL),
            jnp.stack(pool_p), jnp.stack(pool_s), jnp.stack(chunk_s),
            jnp.stack(kv_p[0]), jnp.stack(kv_s[0]), jnp.stack(kv_p[1]), jnp.stack(kv_s[1]),
            jnp.stack(kv_p[2]), jnp.stack(kv_s[2]))
```

```python
import functools

import numpy as np
import jax
import jax.numpy as jnp
from jax import lax
from jax.experimental import pallas as pl
from jax.experimental.pallas import tpu as pltpu

F32 = jnp.float32
BF16 = jnp.bfloat16

D_MODEL = 2048
D_FF = 5504
RMS_EPS = 1e-6
POOL_WINDOWS = (2, 4, 8, 16)
POOL_GROUP = D_MODEL // len(POOL_WINDOWS)
POOL_BUF = max(POOL_WINDOWS) - 1
CHUNK = 128
CHUNK_GROUPS = 8
CHUNK_GROUP_W = D_MODEL // CHUNK_GROUPS
ATTN_GROUPS = ((128, 1), (512, 4), (2048, 16))
HEADS = 16
HEAD_DIM = 128
ATTN_W = HEADS * HEAD_DIM
ROPE_THETA = 10000.0
ATTN_SCALE = HEAD_DIM ** -0.5
NEG = float(np.finfo(np.float32).min)
PAST_LEN = 16384

LANE = 128
FF_TILE = 256
N_FF_TILES = -(-D_FF // FF_TILE)
FFN_ROW_TILE = 1024
ROW_TILE = 512
CHUNK_ROW_TILE = 256
COMBINE_ROW_TILE = 256
ATTN_BLOCK = 128
MIB = 1024 * 1024


def _params(semantics, vmem_mib):
    return pltpu.CompilerParams(dimension_semantics=semantics, vmem_limit_bytes=vmem_mib * MIB)


def _rms(x, g):
    ms = jnp.mean(x * x, axis=-1, keepdims=True)
    return x * lax.rsqrt(ms + RMS_EPS) * g


def _dot(a, b):
    return jnp.dot(a, b, preferred_element_type=F32)


def _dot_nt(a, b):
    return lax.dot_general(a, b, (((1,), (1,)), ((), ())), preferred_element_type=F32)


def _ff_start(f, base=0):
    return (base // LANE + jnp.minimum(f * (FF_TILE // LANE), (D_FF - FF_TILE) // LANE)) * LANE


def _ffn_body(xp_ref, xs_ref, g_ref, wg_ref, wu_ref, wo_ref, yp_ref, ys_ref,
              hp_ref, hs_ref, wgu_ref, wos_ref):
    m = pl.program_id(0)
    s = pl.program_id(1)
    last = N_FF_TILES - 1

    def init(x_ref, h_ref, y_ref):
        x = x_ref[...]
        h_ref[...] = _rms(x, g_ref[...]).astype(BF16)
        y_ref[...] = x

    def stage(slot):
        wgu_ref[slot, :, :FF_TILE] = wg_ref[...].astype(BF16)
        wgu_ref[slot, :, FF_TILE:] = wu_ref[...].astype(BF16)
        wos_ref[slot] = wo_ref[...].astype(BF16)

    def accumulate(slot, h_ref, y_ref, covered):
        gu = _dot(h_ref[...], wgu_ref[slot])
        gate, up = gu[:, :FF_TILE], gu[:, FF_TILE:]
        hid = gate * jax.nn.sigmoid(gate) * up * 0.5
        if covered:
            col = lax.broadcasted_iota(jnp.int32, (1, FF_TILE), 1)
            hid = jnp.where(col >= covered, hid, 0.0)
        y_ref[...] += _dot(hid.astype(BF16), wos_ref[slot])

    def step(slot, covered=0, stage_next=True):
        accumulate(slot, hp_ref, yp_ref, covered)
        if stage_next:
            stage(1 - slot)

        @pl.when(m == 0)
        def _():
            accumulate(slot, hs_ref, ys_ref, covered)

    @pl.when(s == 0)
    def _():
        init(xp_ref, hp_ref, yp_ref)
        stage(0)

    @pl.when((s == 0) & (m == 0))
    def _():
        init(xs_ref, hs_ref, ys_ref)

    @pl.when((s % 2 == 1) & (s <= last))
    def _():
        step(0)

    @pl.when((s % 2 == 0) & (s > 0) & (s <= last))
    def _():
        step(1)

    @pl.when(s == last + 1)
    def _():
        step(last % 2, covered=last * FF_TILE - (D_FF - FF_TILE), stage_next=False)


def _ffn(xp, xs, g, w_in, w_out):
    mp, ms = xp.shape[0], xs.shape[0]
    elem = pl.Element
    tile = lambda s: jnp.minimum(s, N_FF_TILES - 1)
    return pl.pallas_call(
        _ffn_body,
        grid=(mp // FFN_ROW_TILE, N_FF_TILES + 1),
        in_specs=[
            pl.BlockSpec((FFN_ROW_TILE, D_MODEL), lambda m, s: (m, 0), pipeline_mode=pl.Buffered(1)),
            pl.BlockSpec((ms, D_MODEL), lambda m, s: (0, 0)),
            pl.BlockSpec((1, D_MODEL), lambda m, s: (0, 0)),
            pl.BlockSpec((elem(D_MODEL), elem(FF_TILE)), lambda m, s: (0, _ff_start(tile(s)))),
            pl.BlockSpec((elem(D_MODEL), elem(FF_TILE)), lambda m, s: (0, _ff_start(tile(s), D_FF))),
            pl.BlockSpec((elem(FF_TILE), elem(D_MODEL)), lambda m, s: (_ff_start(tile(s)), 0)),
        ],
        out_specs=[
            pl.BlockSpec((FFN_ROW_TILE, D_MODEL), lambda m, s: (m, 0)),
            pl.BlockSpec((ms, D_MODEL), lambda m, s: (0, 0)),
        ],
        out_shape=[jax.ShapeDtypeStruct(xp.shape, F32), jax.ShapeDtypeStruct(xs.shape, F32)],
        scratch_shapes=[pltpu.VMEM((FFN_ROW_TILE, D_MODEL), BF16), pltpu.VMEM((ms, D_MODEL), BF16),
                        pltpu.VMEM((2, D_MODEL, 2 * FF_TILE), BF16),
                        pltpu.VMEM((2, FF_TILE, D_MODEL), BF16)],
        compiler_params=_params(("arbitrary", "arbitrary"), 62),
        name="ffn",
    )(xp, xs, g.reshape(1, D_MODEL), w_in, w_in, w_out)


def _pool_body(x_ref, buf_ref, g_ref, pw_ref, sc_ref, y_ref, st_ref, hb_ref, *, tt, start, nt):
    t = pl.program_id(1)
    halo = POOL_BUF + 1

    @pl.when(t == 0)
    def _():
        hb_ref[0:halo, :] = buf_ref[0]

    x = x_ref[0]
    h = _rms(x, g_ref[...])
    hb_ref[halo:halo + tt, :] = h
    pos = start + t * tt + lax.broadcasted_iota(jnp.int32, (tt, 1), 0)
    for gi, w in enumerate(POOL_WINDOWS):
        sl = slice(gi * POOL_GROUP, (gi + 1) * POOL_GROUP)
        win = hb_ref[halo:halo + tt, sl]
        for k in range(1, w):
            win = win + hb_ref[halo - k:halo - k + tt, sl]
        count = jnp.minimum(w, pos + 1).astype(F32)
        pooled = win / count - h[:, sl]
        mixed = _dot(pooled.astype(BF16), pw_ref[gi])
        y_ref[0, :, sl] = x[:, sl] + mixed * sc_ref[:, sl]
    st_ref[0] = hb_ref[tt + 1:tt + halo, :]
    if nt > 1:
        hb_ref[0:halo, :] = hb_ref[tt:tt + halo, :]


def _pool(x, buf, g, pool_w, scale, start):
    b, t_len, _ = x.shape
    tt = min(t_len, ROW_TILE)
    nt = t_len // tt
    halo = POOL_BUF + 1
    buf16 = jnp.pad(buf, ((0, 0), (1, 0), (0, 0)))
    body = functools.partial(_pool_body, tt=tt, start=start, nt=nt)
    return pl.pallas_call(
        body,
        grid=(b, nt),
        in_specs=[
            pl.BlockSpec((1, tt, D_MODEL), lambda i, t: (i, t, 0)),
            pl.BlockSpec((1, halo, D_MODEL), lambda i, t: (i, 0, 0)),
            pl.BlockSpec((1, D_MODEL), lambda i, t: (0, 0)),
            pl.BlockSpec((len(POOL_WINDOWS), POOL_GROUP, POOL_GROUP), lambda i, t: (0, 0, 0)),
            pl.BlockSpec((1, D_MODEL), lambda i, t: (0, 0)),
        ],
        out_specs=[
            pl.BlockSpec((1, tt, D_MODEL), lambda i, t: (i, t, 0)),
            pl.BlockSpec((1, POOL_BUF, D_MODEL), lambda i, t: (i, 0, 0)),
        ],
        out_shape=[jax.ShapeDtypeStruct(x.shape, F32),
                   jax.ShapeDtypeStruct((b, POOL_BUF, D_MODEL), F32)],
        scratch_shapes=[pltpu.VMEM((halo + tt, D_MODEL), F32)],
        compiler_params=_params(("arbitrary", "arbitrary"), 40),
        name="pool",
    )(x, buf16, g.reshape(1, D_MODEL), pool_w.astype(BF16), scale.reshape(1, D_MODEL))


def _chunk_body(x_ref, g_ref, win_ref, vg_ref, ws_ref, bst_ref, wout_ref, *rest, tm, emit_v):
    if emit_v:
        y_ref, vn_ref, mix_ref = rest
    else:
        y_ref, mix_ref = rest
    x = x_ref[...]
    h = _rms(x, g_ref[...]).astype(BF16)
    uv = _dot(h, win_ref[...])
    uv = 0.5 * uv * (1.0 + lax.erf(uv * np.float32(np.sqrt(0.5))))
    u = uv[:, :D_MODEL]
    vn = _rms(uv[:, D_MODEL:], vg_ref[...])
    if emit_v:
        vn_ref[...] = vn
    q_idx = lax.broadcasted_iota(jnp.int32, (CHUNK, CHUNK), 0)
    c_idx = lax.broadcasted_iota(jnp.int32, (CHUNK, CHUNK), 1)
    causal = c_idx <= q_idx
    for gi in range(CHUNK_GROUPS):
        cols = slice(gi * CHUNK_GROUP_W, (gi + 1) * CHUNK_GROUP_W)
        ws = jnp.where(causal, ws_ref[gi], 0.0).astype(BF16)
        bias = bst_ref[:, gi:gi + 1]
        for c in range(tm // CHUNK):
            rows = slice(c * CHUNK, (c + 1) * CHUNK)
            mixed = _dot(ws, vn[rows, cols].astype(BF16)) + bias
            mix_ref[rows, cols] = (u[rows, cols] * mixed).astype(BF16)
    y_ref[...] = x + _dot(mix_ref[...], wout_ref[...])


def _chunk(x, g, w_in, v_gain, w_s, b_s, w_out, emit_v):
    m = x.shape[0]
    tm = CHUNK_ROW_TILE
    body = functools.partial(_chunk_body, tm=tm, emit_v=emit_v)
    row_spec = pl.BlockSpec((tm, D_MODEL), lambda i: (i, 0))
    once = pl.Buffered(1)
    out_specs = [row_spec]
    out_shape = [jax.ShapeDtypeStruct((m, D_MODEL), F32)]
    if emit_v:
        out_specs.append(row_spec)
        out_shape.append(jax.ShapeDtypeStruct((m, D_MODEL), F32))
    return pl.pallas_call(
        body,
        grid=(m // tm,),
        in_specs=[
            row_spec,
            pl.BlockSpec((1, D_MODEL), lambda i: (0, 0)),
            pl.BlockSpec((D_MODEL, 2 * D_MODEL), lambda i: (0, 0), pipeline_mode=once),
            pl.BlockSpec((1, D_MODEL), lambda i: (0, 0)),
            pl.BlockSpec((CHUNK_GROUPS, CHUNK, CHUNK), lambda i: (0, 0, 0)),
            pl.BlockSpec((CHUNK, CHUNK_GROUPS), lambda i: (0, 0)),
            pl.BlockSpec((D_MODEL, D_MODEL), lambda i: (0, 0), pipeline_mode=once),
        ],
        out_specs=out_specs,
        out_shape=out_shape,
        scratch_shapes=[pltpu.VMEM((tm, D_MODEL), BF16)],
        compiler_params=_params(("arbitrary",), 56),
        name="chunk",
    )(x, g.reshape(1, D_MODEL), w_in.astype(BF16), v_gain.reshape(1, D_MODEL), w_s, b_s.T,
      w_out.astype(BF16))


def _rope_tables(pos):
    half = HEAD_DIM // 2
    freqs = ROPE_THETA ** (-2.0 * jnp.arange(half, dtype=F32) / HEAD_DIM)
    ang = pos.astype(F32)[:, None] * freqs[None, :]
    cos, sin = jnp.cos(ang), jnp.sin(ang)
    return jnp.concatenate([cos, cos], axis=-1), jnp.concatenate([-sin, sin], axis=-1)


def _qkv_body(x_ref, g_ref, w_ref, qn_ref, kn_ref, cos_ref, sin_ref, q_ref, kv_ref, h_ref):
    s = pl.program_id(1)

    @pl.when(s == 0)
    def _():
        h_ref[...] = _rms(x_ref[...], g_ref[...]).astype(BF16)

    y = _dot(h_ref[...], w_ref[...])

    def norm_rope(gain_ref, out_ref, out_scale):
        cos = cos_ref[...]
        sin = sin_ref[...]
        for hd in range(HEADS):
            cols = slice(hd * HEAD_DIM, (hd + 1) * HEAD_DIM)
            yn = _rms(y[:, cols], gain_ref[...])
            rot = pltpu.roll(yn, HEAD_DIM // 2, 1)
            out = yn * cos + rot * sin
            if out_scale is not None:
                out = out * out_scale
            out_ref[:, cols] = out.astype(out_ref.dtype)

    @pl.when(s == 0)
    def _():
        norm_rope(qn_ref, q_ref, ATTN_SCALE)

    @pl.when(s == 1)
    def _():
        norm_rope(kn_ref, kv_ref, None)

    @pl.when(s == 2)
    def _():
        kv_ref[...] = y


def _qkv(x, g, w_qkv, q_gain, k_gain, cos, sin, group):
    m = x.shape[0]
    tm = min(m, ROW_TILE)
    n_tab = cos.shape[0] // tm
    n_groups = len(ATTN_GROUPS)
    return pl.pallas_call(
        _qkv_body,
        grid=(m // tm, 3),
        in_specs=[
            pl.BlockSpec((tm, D_MODEL), lambda i, s: (i, 0)),
            pl.BlockSpec((1, D_MODEL), lambda i, s: (0, 0)),
            pl.BlockSpec((D_MODEL, ATTN_W), lambda i, s: (0, s * n_groups + group)),
            pl.BlockSpec((1, HEAD_DIM), lambda i, s: (0, 0)),
            pl.BlockSpec((1, HEAD_DIM), lambda i, s: (0, 0)),
            pl.BlockSpec((tm, HEAD_DIM), lambda i, s: (i % n_tab, 0)),
            pl.BlockSpec((tm, HEAD_DIM), lambda i, s: (i % n_tab, 0)),
        ],
        out_specs=[
            pl.BlockSpec((tm, ATTN_W), lambda i, s: (i, 0)),
            pl.BlockSpec((tm, ATTN_W), lambda i, s: (i, jnp.maximum(s - 1, 0))),
        ],
        out_shape=[jax.ShapeDtypeStruct((m, ATTN_W), BF16),
                   jax.ShapeDtypeStruct((m, 2 * ATTN_W), F32)],
        scratch_shapes=[pltpu.VMEM((tm, D_MODEL), BF16)],
        compiler_params=_params(("arbitrary", "arbitrary"), 48),
        name="qkv",
    )(x, g.reshape(1, D_MODEL), w_qkv, q_gain.reshape(1, HEAD_DIM), k_gain.reshape(1, HEAD_DIM),
      cos, sin)


def _band_attn_body(q_ref, kp_ref, kc_ref, vp_ref, vc_ref, o_ref, lse_ref):
    blk = pl.program_id(2)
    r = ATTN_BLOCK
    qi = lax.broadcasted_iota(jnp.int32, (r, 2 * r), 0)
    kj = lax.broadcasted_iota(jnp.int32, (r, 2 * r), 1)
    dist = r + qi - kj
    mask = (dist >= 0) & (dist <= r) & ((kj >= r) | (blk > 0))
    for hd in range(HEADS):
        cols = slice(hd * HEAD_DIM, (hd + 1) * HEAD_DIM)
        k = jnp.concatenate([kp_ref[0, :, cols], kc_ref[0, :, cols]], axis=0).astype(BF16)
        v = jnp.concatenate([vp_ref[0, :, cols], vc_ref[0, :, cols]], axis=0).astype(BF16)
        s = jnp.where(mask, _dot_nt(q_ref[0, :, cols], k), NEG)
        mx = jnp.max(s, axis=-1, keepdims=True)
        p = jnp.exp(s - mx)
        den = jnp.sum(p, axis=-1, keepdims=True)
        o_ref[0, :, cols] = _dot(p.astype(BF16), v) / den
        lse_ref[0, :, cols] = jnp.broadcast_to(mx + jnp.log(den), (r, HEAD_DIM))


def _band_attn(q, kv, batch, dil):
    m = q.shape[0]
    n = m // batch // dil
    nb = n // ATTN_BLOCK
    qv = q.reshape(batch, n, dil * ATTN_W)
    kvv = kv.reshape(batch, n, dil * 2 * ATTN_W)
    blk = (1, ATTN_BLOCK, ATTN_W)
    prev = lambda j: jnp.maximum(j - 1, 0)
    o, lse = pl.pallas_call(
        _band_attn_body,
        grid=(batch, dil, nb),
        in_specs=[
            pl.BlockSpec(blk, lambda b, r, j: (b, j, r)),
            pl.BlockSpec(blk, lambda b, r, j: (b, prev(j), 2 * r)),
            pl.BlockSpec(blk, lambda b, r, j: (b, j, 2 * r)),
            pl.BlockSpec(blk, lambda b, r, j: (b, prev(j), 2 * r + 1)),
            pl.BlockSpec(blk, lambda b, r, j: (b, j, 2 * r + 1)),
        ],
        out_specs=[pl.BlockSpec(blk, lambda b, r, j: (b, j, r))] * 2,
        out_shape=[jax.ShapeDtypeStruct(qv.shape, F32)] * 2,
        compiler_params=_params(("arbitrary",) * 3, 32),
        name="band_attn",
    )(qv, kvv, kvv, kvv, kvv)
    return o.reshape(m, ATTN_W), lse.reshape(m, ATTN_W)


def _cache_attn_body(q_ref, new_ref, cache_ref, o_ref, lse_ref, *, dil, t_len):
    r = ATTN_BLOCK
    lane = lax.broadcasted_iota(jnp.int32, (8, r), 1)
    row = lax.broadcasted_iota(jnp.int32, (t_len, 1), 0)
    for t in range(t_len):
        base = (t if dil > 1 else 0) * 2 * ATTN_W
        for hd in range(HEADS):
            cols = slice(hd * HEAD_DIM, (hd + 1) * HEAD_DIM)
            kcols = slice(base + hd * HEAD_DIM, base + (hd + 1) * HEAD_DIM)
            vcols = slice(base + ATTN_W + hd * HEAD_DIM, base + ATTN_W + (hd + 1) * HEAD_DIM)
            q = q_ref[0, t:t + 1, cols]
            q8 = jnp.broadcast_to(q, (8, HEAD_DIM))
            s_c = _dot_nt(q8, cache_ref[0, :, kcols].astype(BF16))
            s_n = jnp.sum(new_ref[0, :, cols] * q.astype(F32), axis=-1, keepdims=True)
            if dil == 1:
                s_c = jnp.where(lane >= t, s_c, NEG)
                s_n = jnp.where(row <= t, s_n, NEG)
            else:
                s_n = jnp.where(row == t, s_n, NEG)
            mx = jnp.maximum(jnp.max(s_c, axis=-1, keepdims=True)[0:1],
                             jnp.max(s_n, axis=0, keepdims=True))
            p_c = jnp.exp(s_c - mx)
            p_n = jnp.exp(s_n - mx)
            den = jnp.sum(p_c, axis=-1, keepdims=True)[0:1] + jnp.sum(p_n, axis=0, keepdims=True)
            acc = _dot(p_c.astype(BF16), cache_ref[0, :, vcols].astype(BF16))[0:1]
            acc = acc + jnp.sum(p_n * new_ref[0, :, ATTN_W + hd * HEAD_DIM:ATTN_W + (hd + 1) * HEAD_DIM],
                                axis=0, keepdims=True)
            o_ref[0, t:t + 1, cols] = acc / den
            lse_ref[0, t:t + 1, cols] = jnp.broadcast_to(mx + jnp.log(den), (1, HEAD_DIM))


def _cache_attn(q, kv_new, cache, window, dil, batch):
    t_len = q.shape[0] // batch
    length = cache.shape[1]
    assert length == window and length // dil == ATTN_BLOCK and (dil == 1 or t_len <= dil)
    n_res = t_len if dil > 1 else 1
    cv = cache.reshape(batch, ATTN_BLOCK, dil * 2 * ATTN_W)
    body = functools.partial(_cache_attn_body, dil=dil, t_len=t_len)
    o, lse = pl.pallas_call(
        body,
        grid=(batch,),
        in_specs=[
            pl.BlockSpec((1, t_len, ATTN_W), lambda b: (b, 0, 0)),
            pl.BlockSpec((1, t_len, 2 * ATTN_W), lambda b: (b, 0, 0)),
            pl.BlockSpec((1, ATTN_BLOCK, n_res * 2 * ATTN_W), lambda b: (b, 0, 0)),
        ],
        out_specs=[pl.BlockSpec((1, t_len, ATTN_W), lambda b: (b, 0, 0))] * 2,
        out_shape=[jax.ShapeDtypeStruct((batch, t_len, ATTN_W), F32)] * 2,
        compiler_params=_params(("arbitrary",), 40),
        name="cache_attn",
    )(q.reshape(batch, t_len, ATTN_W), kv_new.reshape(batch, t_len, 2 * ATTN_W), cv)
    return o.reshape(batch * t_len, ATTN_W), lse.reshape(batch * t_len, ATTN_W)


def _combine_body(x_ref, o0_ref, o1_ref, o2_ref, l0_ref, l1_ref, l2_ref, w_ref, y_ref):
    l0, l1, l2 = l0_ref[...], l1_ref[...], l2_ref[...]
    mx = jnp.maximum(jnp.maximum(l0, l1), l2)
    e0, e1, e2 = jnp.exp(l0 - mx), jnp.exp(l1 - mx), jnp.exp(l2 - mx)
    o = (e0 * o0_ref[...] + e1 * o1_ref[...] + e2 * o2_ref[...]) / (e0 + e1 + e2)
    y_ref[...] = x_ref[...] + _dot(o.astype(BF16), w_ref[...])


def _combine(x, outs, lses, w_out):
    m = x.shape[0]
    tm = min(m, COMBINE_ROW_TILE)
    row_spec = pl.BlockSpec((tm, D_MODEL), lambda i: (i, 0))
    return pl.pallas_call(
        _combine_body,
        grid=(m // tm,),
        in_specs=[row_spec] * 7 + [pl.BlockSpec((ATTN_W, D_MODEL), lambda i: (0, 0))],
        out_specs=row_spec,
        out_shape=jax.ShapeDtypeStruct((m, D_MODEL), F32),
        compiler_params=_params(("arbitrary",), 56),
        name="attn_combine",
    )(x, *outs, *lses, w_out)


def kernel(x_prompt, x_sample, state_pool, cache_kv_g0, cache_kv_g1, cache_kv_g2, norm_ffn1, ffn1_w_in, ffn1_w_out, norm_mix, norm_ffn2, ffn2_w_in, ffn2_w_out, pool_w, pool_scale, chunk_w_in, chunk_v_norm, chunk_w_s, chunk_b_s, chunk_w_out, attn_w_qkv, attn_q_norm, attn_k_norm, attn_w_out):
    caches = (cache_kv_g0, cache_kv_g1, cache_kv_g2)
    batch, seq, _ = x_prompt.shape
    dec_batch, dec_seq, _ = x_sample.shape
    depth = norm_ffn1.shape[0]
    xp = x_prompt.reshape(batch * seq, D_MODEL)
    xs = x_sample.reshape(dec_batch * dec_seq, D_MODEL)
    pool_p, pool_s, chunk_s = [], [], []
    kv_p = [[] for _ in ATTN_GROUPS]
    kv_s = [[] for _ in ATTN_GROUPS]
    for i in range(depth):
        kind, j = i % 3, i // 3
        xp, xs = _ffn(xp, xs, norm_ffn1[i], ffn1_w_in[i], ffn1_w_out[i])
        if kind == 0:
            zero_buf = jnp.zeros((batch, POOL_BUF, D_MODEL), F32)
            yp, st_p = _pool(xp.reshape(batch, seq, D_MODEL), zero_buf, norm_mix[i], pool_w[j],
                             pool_scale[j], 0)
            ys, st_s = _pool(xs.reshape(dec_batch, dec_seq, D_MODEL), state_pool[j], norm_mix[i],
                             pool_w[j], pool_scale[j], PAST_LEN)
            xp = yp.reshape(batch * seq, D_MODEL)
            xs = ys.reshape(dec_batch * dec_seq, D_MODEL)
            pool_p.append(st_p)
            pool_s.append(st_s)
        elif kind == 1:
            args = (norm_mix[i], chunk_w_in[j], chunk_v_norm[j], chunk_w_s[j], chunk_b_s[j],
                    chunk_w_out[j])
            (xp,) = _chunk(xp, *args, emit_v=False)
            xs_pad = jnp.pad(xs.reshape(dec_batch, dec_seq, D_MODEL),
                             ((0, 0), (0, CHUNK - dec_seq), (0, 0)))
            ys_pad, v_pad = _chunk(xs_pad.reshape(dec_batch * CHUNK, D_MODEL), *args, emit_v=True)
            xs = ys_pad.reshape(dec_batch, CHUNK, D_MODEL)[:, :dec_seq].reshape(-1, D_MODEL)
            chunk_s.append(v_pad.reshape(dec_batch, CHUNK, D_MODEL)[:, :dec_seq])
        else:
            w_qkv = attn_w_qkv[j].astype(BF16)
            w_o = attn_w_out[j].astype(BF16)
            cos_p, sin_p = _rope_tables(jnp.arange(seq))
            cos_s, sin_s = _rope_tables(jnp.tile(PAST_LEN + jnp.arange(dec_seq), dec_batch))
            outs_p, lses_p, outs_s, lses_s = [], [], [], []
            for gi, (window, dil) in enumerate(ATTN_GROUPS):
                q_p, kvn_p = _qkv(xp, norm_mix[i], w_qkv, attn_q_norm[j], attn_k_norm[j], cos_p, sin_p, gi)
                o, l = _band_attn(q_p, kvn_p, batch, dil)
                outs_p.append(o)
                lses_p.append(l)
                keep = min(window, seq)
                kv_p[gi].append(kvn_p.reshape(batch, seq, 2, HEADS, HEAD_DIM)[:, seq - keep:])
                q_s, kvn_s = _qkv(xs, norm_mix[i], w_qkv, attn_q_norm[j], attn_k_norm[j], cos_s, sin_s, gi)
                o, l = _cache_attn(q_s, kvn_s, caches[gi][j], window, dil, dec_batch)
                outs_s.append(o)
                lses_s.append(l)
                kv_s[gi].append(kvn_s.reshape(dec_batch, dec_seq, 2, HEADS, HEAD_DIM))
            xp = _combine(xp, outs_p, lses_p, w_o)
            xs = _combine(xs, outs_s, lses_s, w_o)
        xp, xs = _ffn(xp, xs, norm_ffn2[i], ffn2_w_in[i], ffn2_w_out[i])
    return (xp.reshape(batch, seq, D_MODEL), xs.reshape(dec_batch, dec_seq, D_MODEL),
            jnp.stack(pool_p), jnp.stack(pool_s), jnp.stack(chunk_s),
            jnp.stack(kv_p[0]), jnp.stack(kv_s[0]), jnp.stack(kv_p[1]), jnp.stack(kv_s[1]),
            jnp.stack(kv_p[2]), jnp.stack(kv_s[2]))
```

```python
import functools

import numpy as np
import jax
import jax.numpy as jnp
from jax import lax
from jax.experimental import pallas as pl
from jax.experimental.pallas import tpu as pltpu

F32 = jnp.float32
BF16 = jnp.bfloat16

D_MODEL = 2048
D_FF = 5504
RMS_EPS = 1e-6
POOL_WINDOWS = (2, 4, 8, 16)
POOL_GROUP = D_MODEL // len(POOL_WINDOWS)
POOL_BUF = max(POOL_WINDOWS) - 1
CHUNK = 128
CHUNK_GROUPS = 8
CHUNK_GROUP_W = D_MODEL // CHUNK_GROUPS
ATTN_GROUPS = ((128, 1), (512, 4), (2048, 16))
N_GROUPS = len(ATTN_GROUPS)
HEADS = 16
HEAD_DIM = 128
ATTN_W = HEADS * HEAD_DIM
ROPE_THETA = 10000.0
ATTN_SCALE = HEAD_DIM ** -0.5
NEG = float(np.finfo(np.float32).min)
PAST_LEN = 16384

LANE = 128
SUBLANE = 8
FF_TILE = 512
D_FF_PAD = -(-D_FF // FF_TILE) * FF_TILE
FFN_ROW_TILE = 1024
ROW_TILE = 512
CHUNK_ROW_TILE = 256
ATTN_BLOCK = 128
QKV_COL_TILE = 1024
HEADS_PER_STEP = QKV_COL_TILE // HEAD_DIM
SLAB_PITCH = ROW_TILE + SUBLANE
MIB = 1024 * 1024


def _params(semantics, vmem_mib):
    return pltpu.CompilerParams(dimension_semantics=semantics, vmem_limit_bytes=vmem_mib * MIB)


def _rms(x, g):
    ms = jnp.mean(x * x, axis=-1, keepdims=True)
    return x * lax.rsqrt(ms + RMS_EPS) * g


def _dot(a, b):
    return jnp.dot(a, b, preferred_element_type=F32)


def _dot_nt(a, b):
    return lax.dot_general(a, b, (((1,), (1,)), ((), ())), preferred_element_type=F32)


def _ffn_body(xp_ref, xs_ref, g_ref, wg_ref, wu_ref, wo_ref, yp_ref, ys_ref, hp_ref, hs_ref):
    m = pl.program_id(0)
    f = pl.program_id(1)

    def init(x_ref, h_ref, y_ref):
        x = x_ref[...]
        h_ref[...] = _rms(x, g_ref[...]).astype(BF16)
        y_ref[...] = x

    def accumulate(h_ref, y_ref):
        h = h_ref[...]
        gate = _dot(h, wg_ref[...])
        up = _dot(h, wu_ref[...])
        hid = (gate * jax.nn.sigmoid(gate) * up * 0.5).astype(BF16)
        y_ref[...] += _dot(hid, wo_ref[...])

    @pl.when(f == 0)
    def _():
        init(xp_ref, hp_ref, yp_ref)

    @pl.when((f == 0) & (m == 0))
    def _():
        init(xs_ref, hs_ref, ys_ref)

    accumulate(hp_ref, yp_ref)

    @pl.when(m == 0)
    def _():
        accumulate(hs_ref, ys_ref)


def _ffn(xp, xs, g, w_in, w_out):
    mp, ms = xp.shape[0], xs.shape[0]
    nf = D_FF_PAD // FF_TILE
    pad = D_FF_PAD - D_FF
    wg = jnp.pad(w_in[:, :D_FF], ((0, 0), (0, pad))).astype(BF16)
    wu = jnp.pad(w_in[:, D_FF:], ((0, 0), (0, pad))).astype(BF16)
    wo = jnp.pad(w_out, ((0, pad), (0, 0))).astype(BF16)
    return pl.pallas_call(
        _ffn_body,
        grid=(mp // FFN_ROW_TILE, nf),
        in_specs=[
            pl.BlockSpec((FFN_ROW_TILE, D_MODEL), lambda m, f: (m, 0)),
            pl.BlockSpec((ms, D_MODEL), lambda m, f: (0, 0)),
            pl.BlockSpec((1, D_MODEL), lambda m, f: (0, 0)),
            pl.BlockSpec((D_MODEL, FF_TILE), lambda m, f: (0, f)),
            pl.BlockSpec((D_MODEL, FF_TILE), lambda m, f: (0, f)),
            pl.BlockSpec((FF_TILE, D_MODEL), lambda m, f: (f, 0)),
        ],
        out_specs=[
            pl.BlockSpec((FFN_ROW_TILE, D_MODEL), lambda m, f: (m, 0)),
            pl.BlockSpec((ms, D_MODEL), lambda m, f: (0, 0)),
        ],
        out_shape=[jax.ShapeDtypeStruct(xp.shape, F32), jax.ShapeDtypeStruct(xs.shape, F32)],
        scratch_shapes=[pltpu.VMEM((FFN_ROW_TILE, D_MODEL), BF16), pltpu.VMEM((ms, D_MODEL), BF16)],
        compiler_params=_params(("arbitrary", "arbitrary"), 60),
        name="ffn",
    )(xp, xs, g.reshape(1, D_MODEL), wg, wu, wo)


def _pool_body(x_ref, buf_ref, g_ref, pw_ref, sc_ref, y_ref, st_ref, hb_ref, *, tt, start, nt):
    t = pl.program_id(1)
    halo = POOL_BUF + 1

    @pl.when(t == 0)
    def _():
        hb_ref[0:halo, :] = buf_ref[0]

    x = x_ref[0]
    h = _rms(x, g_ref[...])
    hb_ref[halo:halo + tt, :] = h
    pos = start + t * tt + lax.broadcasted_iota(jnp.int32, (tt, 1), 0)
    for gi, w in enumerate(POOL_WINDOWS):
        sl = slice(gi * POOL_GROUP, (gi + 1) * POOL_GROUP)
        win = hb_ref[halo:halo + tt, sl]
        for k in range(1, w):
            win = win + hb_ref[halo - k:halo - k + tt, sl]
        count = jnp.minimum(w, pos + 1).astype(F32)
        pooled = win / count - h[:, sl]
        mixed = _dot(pooled.astype(BF16), pw_ref[gi])
        y_ref[0, :, sl] = x[:, sl] + mixed * sc_ref[:, sl]
    st_ref[0] = hb_ref[tt + 1:tt + halo, :]
    if nt > 1:
        hb_ref[0:halo, :] = hb_ref[tt:tt + halo, :]


def _pool(x, buf, g, pool_w, scale, start):
    b, t_len, _ = x.shape
    tt = min(t_len, ROW_TILE)
    nt = t_len // tt
    halo = POOL_BUF + 1
    buf16 = jnp.pad(buf, ((0, 0), (1, 0), (0, 0)))
    body = functools.partial(_pool_body, tt=tt, start=start, nt=nt)
    return pl.pallas_call(
        body,
        grid=(b, nt),
        in_specs=[
            pl.BlockSpec((1, tt, D_MODEL), lambda i, t: (i, t, 0)),
            pl.BlockSpec((1, halo, D_MODEL), lambda i, t: (i, 0, 0)),
            pl.BlockSpec((1, D_MODEL), lambda i, t: (0, 0)),
            pl.BlockSpec((len(POOL_WINDOWS), POOL_GROUP, POOL_GROUP), lambda i, t: (0, 0, 0)),
            pl.BlockSpec((1, D_MODEL), lambda i, t: (0, 0)),
        ],
        out_specs=[
            pl.BlockSpec((1, tt, D_MODEL), lambda i, t: (i, t, 0)),
            pl.BlockSpec((1, POOL_BUF, D_MODEL), lambda i, t: (i, 0, 0)),
        ],
        out_shape=[jax.ShapeDtypeStruct(x.shape, F32),
                   jax.ShapeDtypeStruct((b, POOL_BUF, D_MODEL), F32)],
        scratch_shapes=[pltpu.VMEM((halo + tt, D_MODEL), F32)],
        compiler_params=_params(("arbitrary", "arbitrary"), 40),
        name="pool",
    )(x, buf16, g.reshape(1, D_MODEL), pool_w.astype(BF16), scale.reshape(1, D_MODEL))


def _chunk_body(x_ref, g_ref, win_ref, vg_ref, ws_ref, bst_ref, wout_ref, *rest, tm, emit_v):
    if emit_v:
        y_ref, vn_ref, mix_ref = rest
    else:
        y_ref, mix_ref = rest
    x = x_ref[...]
    h = _rms(x, g_ref[...]).astype(BF16)
    uv = _dot(h, win_ref[...])
    uv = 0.5 * uv * (1.0 + lax.erf(uv * np.float32(np.sqrt(0.5))))
    u = uv[:, :D_MODEL]
    vn = _rms(uv[:, D_MODEL:], vg_ref[...])
    if emit_v:
        vn_ref[...] = vn
    q_idx = lax.broadcasted_iota(jnp.int32, (CHUNK, CHUNK), 0)
    c_idx = lax.broadcasted_iota(jnp.int32, (CHUNK, CHUNK), 1)
    causal = c_idx <= q_idx
    for gi in range(CHUNK_GROUPS):
        cols = slice(gi * CHUNK_GROUP_W, (gi + 1) * CHUNK_GROUP_W)
        ws = jnp.where(causal, ws_ref[gi], 0.0).astype(BF16)
        bias = bst_ref[:, gi:gi + 1]
        for c in range(tm // CHUNK):
            rows = slice(c * CHUNK, (c + 1) * CHUNK)
            mixed = _dot(ws, vn[rows, cols].astype(BF16)) + bias
            mix_ref[rows, cols] = (u[rows, cols] * mixed).astype(BF16)
    y_ref[...] = x + _dot(mix_ref[...], wout_ref[...])


def _chunk(x, g, w_in, v_gain, w_s, b_s, w_out, emit_v):
    m = x.shape[0]
    tm = CHUNK_ROW_TILE
    body = functools.partial(_chunk_body, tm=tm, emit_v=emit_v)
    row_spec = pl.BlockSpec((tm, D_MODEL), lambda i: (i, 0))
    once = pl.Buffered(1)
    out_specs = [row_spec]
    out_shape = [jax.ShapeDtypeStruct((m, D_MODEL), F32)]
    if emit_v:
        out_specs.append(row_spec)
        out_shape.append(jax.ShapeDtypeStruct((m, D_MODEL), F32))
    return pl.pallas_call(
        body,
        grid=(m // tm,),
        in_specs=[
            row_spec,
            pl.BlockSpec((1, D_MODEL), lambda i: (0, 0)),
            pl.BlockSpec((D_MODEL, 2 * D_MODEL), lambda i: (0, 0), pipeline_mode=once),
            pl.BlockSpec((1, D_MODEL), lambda i: (0, 0)),
            pl.BlockSpec((CHUNK_GROUPS, CHUNK, CHUNK), lambda i: (0, 0, 0)),
            pl.BlockSpec((CHUNK, CHUNK_GROUPS), lambda i: (0, 0)),
            pl.BlockSpec((D_MODEL, D_MODEL), lambda i: (0, 0), pipeline_mode=once),
        ],
        out_specs=out_specs,
        out_shape=out_shape,
        scratch_shapes=[pltpu.VMEM((tm, D_MODEL), BF16)],
        compiler_params=_params(("arbitrary",), 56),
        name="chunk",
    )(x, g.reshape(1, D_MODEL), w_in.astype(BF16), v_gain.reshape(1, D_MODEL), w_s, b_s.T,
      w_out.astype(BF16))


def _tile_perm(dil):
    u = np.arange(ROW_TILE)
    n_k = ROW_TILE // dil
    return (u % n_k) * dil + u // n_k


def _rope_tables(pos):
    half = HEAD_DIM // 2
    freqs = ROPE_THETA ** (-2.0 * jnp.arange(half, dtype=F32) / HEAD_DIM)
    ang = pos.astype(F32)[:, None] * freqs[None, :]
    cos, sin = jnp.cos(ang), jnp.sin(ang)
    return jnp.concatenate([cos, cos], axis=-1), jnp.concatenate([-sin, sin], axis=-1)


def _rope(x, cos, sin_signed):
    return x * cos + pltpu.roll(x, HEAD_DIM // 2, 1) * sin_signed


def _norm_perm_body(x_ref, g_ref, *h_refs):
    h = _rms(x_ref[...], g_ref[...]).astype(BF16)
    u = lax.broadcasted_iota(jnp.int32, (ROW_TILE, ROW_TILE), 0)
    t = lax.broadcasted_iota(jnp.int32, (ROW_TILE, ROW_TILE), 1)
    for h_ref, (_, dil) in zip(h_refs, ATTN_GROUPS):
        if dil == 1:
            h_ref[...] = h
        else:
            n_k = ROW_TILE // dil
            src = (u & (n_k - 1)) * dil + lax.shift_right_logical(u, n_k.bit_length() - 1)
            select = jnp.where(t == src, 1.0, 0.0).astype(BF16)
            h_ref[...] = _dot(select, h).astype(BF16)


def _norm_perm(x, g):
    m = x.shape[0]
    row_spec = pl.BlockSpec((ROW_TILE, D_MODEL), lambda i: (i, 0))
    return pl.pallas_call(
        _norm_perm_body,
        grid=(m // ROW_TILE,),
        in_specs=[row_spec, pl.BlockSpec((1, D_MODEL), lambda i: (0, 0))],
        out_specs=[row_spec] * N_GROUPS,
        out_shape=[jax.ShapeDtypeStruct((m, D_MODEL), BF16)] * N_GROUPS,
        compiler_params=_params(("arbitrary",), 40),
        name="norm_perm",
    )(x, g.reshape(1, D_MODEL))


def _qkv_body(h_ref, w_ref, qn_ref, kn_ref, cos_ref, sin_ref, q_ref, k_ref, v_ref, kvf_ref,
              wbf_ref, slab_ref, *, dil, final_rows, tiles_per_seq):
    c = pl.program_id(0)
    m = pl.program_id(1)
    s = c // (ATTN_W // QKV_COL_TILE)

    @pl.when(m == 0)
    def _():
        wbf_ref[...] = w_ref[...].astype(BF16)

    def heads():
        for pair in range(HEADS_PER_STEP // 2):
            cols = slice(pair * 2 * HEAD_DIM, (pair + 1) * 2 * HEAD_DIM)
            y = _dot(h_ref[...], wbf_ref[:, cols])
            yield y[:, :HEAD_DIM]
            yield y[:, HEAD_DIM:]

    def to_slab(hl, val):
        slab_ref[hl * SLAB_PITCH:hl * SLAB_PITCH + ROW_TILE, :] = val

    def gather_final():
        n_k = ROW_TILE // dil

        def body(t, carry):
            t_nat = (ROW_TILE - final_rows) + t
            u = (t_nat & (dil - 1)) * n_k + lax.shift_right_logical(t_nat, dil.bit_length() - 1)
            kvf_ref[t, 0] = slab_ref[pl.ds(u, HEADS_PER_STEP, stride=SLAB_PITCH), :]
            return carry

        lax.fori_loop(0, final_rows, body, 0, unroll=8)

    def emit_final():
        if final_rows == ROW_TILE and tiles_per_seq == 1:
            gather_final()
        else:
            pl.when(m % tiles_per_seq == tiles_per_seq - 1)(gather_final)

    @pl.when(s == 0)
    def _():
        for hl, yh in enumerate(heads()):
            qn = _rope(_rms(yh, qn_ref[...]), cos_ref[...], sin_ref[...])
            q_ref[hl] = (qn * ATTN_SCALE).astype(BF16)

    @pl.when(s == 1)
    def _():
        for hl, yh in enumerate(heads()):
            kn = _rope(_rms(yh, kn_ref[...]), cos_ref[...], sin_ref[...])
            k_ref[hl] = kn.astype(BF16)
            to_slab(hl, kn)
        emit_final()

    @pl.when(s == 2)
    def _():
        for hl, yh in enumerate(heads()):
            v_ref[hl] = yh.astype(BF16)
            to_slab(hl, yh)
        emit_final()


def _qkv(h, w_qkv, q_gain, k_gain, cos, sin, group, seq):
    window, dil = ATTN_GROUPS[group]
    m = h.shape[0]
    n_m = m // ROW_TILE
    tiles_per_seq = seq // ROW_TILE
    keep = min(window, seq)
    final_rows = min(keep, ROW_TILE)
    every_tile = keep == seq
    n_col = 3 * ATTN_W // QKV_COL_TILE // 1
    halves = ATTN_W // QKV_COL_TILE

    def att_idx(s_own):
        def idx(c, i):
            s, hh = c // halves, c % halves
            before, after = s < s_own, s > s_own
            return (jnp.where(before, 0, jnp.where(after, halves - 1, hh)),
                    jnp.where(before, 0, jnp.where(after, n_m - 1, i)), 0)
        return idx

    def final_idx(c, i):
        s, hh = c // halves, c % halves
        rb = i if every_tile else i // tiles_per_seq
        live = s > 0
        return (jnp.where(live, rb, 0), jnp.where(live, s - 1, 0), jnp.where(live, hh, 0), 0)

    body = functools.partial(_qkv_body, dil=dil, final_rows=final_rows,
                             tiles_per_seq=1 if every_tile else tiles_per_seq)
    att_shape = jax.ShapeDtypeStruct((HEADS, m, HEAD_DIM), BF16)
    att_block = (HEADS_PER_STEP, ROW_TILE, HEAD_DIM)
    batch = m // seq
    return pl.pallas_call(
        body,
        grid=(n_col, n_m),
        in_specs=[
            pl.BlockSpec((ROW_TILE, D_MODEL), lambda c, i: (i, 0)),
            pl.BlockSpec((D_MODEL, QKV_COL_TILE),
                         lambda c, i: (0, ((c // halves) * N_GROUPS + group) * halves + c % halves)),
            pl.BlockSpec((1, HEAD_DIM), lambda c, i: (0, 0)),
            pl.BlockSpec((1, HEAD_DIM), lambda c, i: (0, 0)),
            pl.BlockSpec((ROW_TILE, HEAD_DIM), lambda c, i: (i % tiles_per_seq, 0)),
            pl.BlockSpec((ROW_TILE, HEAD_DIM), lambda c, i: (i % tiles_per_seq, 0)),
        ],
        out_specs=[
            pl.BlockSpec(att_block, att_idx(0)),
            pl.BlockSpec(att_block, att_idx(1)),
            pl.BlockSpec(att_block, att_idx(2)),
            pl.BlockSpec((final_rows, 1, HEADS_PER_STEP, HEAD_DIM), final_idx),
        ],
        out_shape=[att_shape, att_shape, att_shape,
                   jax.ShapeDtypeStruct((batch * keep, 2, HEADS, HEAD_DIM), F32)],
        scratch_shapes=[pltpu.VMEM((D_MODEL, QKV_COL_TILE), BF16),
                        pltpu.VMEM((HEADS_PER_STEP * SLAB_PITCH, HEAD_DIM), F32)],
        compiler_params=_params(("arbitrary", "arbitrary"), 48),
        name="qkv",
    )(h, w_qkv, q_gain.reshape(1, HEAD_DIM), k_gain.reshape(1, HEAD_DIM), cos, sin)


def _band_mask(n_keys):
    qi = lax.broadcasted_iota(jnp.int32, (ATTN_BLOCK, n_keys), 0)
    kj = lax.broadcasted_iota(jnp.int32, (ATTN_BLOCK, n_keys), 1)
    dist = (n_keys - ATTN_BLOCK) + qi - kj
    return (dist >= 0) & (dist <= ATTN_BLOCK)


def _attn_body(*refs, seq):
    qkv_refs, o_ref, og_ref, lg_ref = refs[:3 * N_GROUPS], refs[3 * N_GROUPS], refs[-2], refs[-1]
    masks = {n: _band_mask(n) for n in (ATTN_BLOCK, 2 * ATTN_BLOCK)}
    for gi, (_, dil) in enumerate(ATTN_GROUPS):
        q_ref, k_ref, v_ref = qkv_refs[3 * gi:3 * gi + 3]
        n_k = ROW_TILE // dil
        n_blocks = seq // dil // ATTN_BLOCK

        def rows_of(ref, r, blk):
            if n_k >= ATTN_BLOCK:
                per_tile = n_k // ATTN_BLOCK
                start = (blk // per_tile) * ROW_TILE + r * n_k + (blk % per_tile) * ATTN_BLOCK
                return ref[0, start:start + ATTN_BLOCK, :]
            parts = [ref[0, tl * ROW_TILE + r * n_k:tl * ROW_TILE + (r + 1) * n_k, :]
                     for tl in range(blk * (ATTN_BLOCK // n_k), (blk + 1) * (ATTN_BLOCK // n_k))]
            return jnp.concatenate(parts, axis=0)

        for r in range(dil):
            for blk in range(n_blocks):
                q = rows_of(q_ref, r, blk)
                if blk == 0:
                    k, v = rows_of(k_ref, r, 0), rows_of(v_ref, r, 0)
                else:
                    k = jnp.concatenate([rows_of(k_ref, r, blk - 1), rows_of(k_ref, r, blk)], axis=0)
                    v = jnp.concatenate([rows_of(v_ref, r, blk - 1), rows_of(v_ref, r, blk)], axis=0)
                s = jnp.where(masks[k.shape[0]], _dot_nt(q, k), NEG)
                mx = jnp.max(s, axis=-1, keepdims=True)
                p = jnp.exp(s - mx)
                den = jnp.sum(p, axis=-1, keepdims=True)
                o = _dot(p.astype(BF16), v) / den
                lse = jnp.broadcast_to(mx + jnp.log(den), (ATTN_BLOCK, HEAD_DIM))
                if dil == 1:
                    dst = pl.ds(blk * ATTN_BLOCK, ATTN_BLOCK)
                else:
                    dst = pl.ds(blk * ATTN_BLOCK * dil + r, ATTN_BLOCK, stride=dil)
                og_ref[gi, dst, :] = o
                lg_ref[gi, dst, :] = lse
    for blk in range(seq // ATTN_BLOCK):
        rows = slice(blk * ATTN_BLOCK, (blk + 1) * ATTN_BLOCK)
        ls = [lg_ref[gi, rows, :] for gi in range(N_GROUPS)]
        mx = functools.reduce(jnp.maximum, ls)
        es = [jnp.exp(l - mx) for l in ls]
        num = sum(e * og_ref[gi, rows, :] for gi, e in enumerate(es))
        o_ref[rows, :] = (num / sum(es)).astype(BF16)


def _attn(qkvs, batch, seq):
    flat = [a for qkv in qkvs for a in qkv]
    blk = pl.BlockSpec((1, seq, HEAD_DIM), lambda b, hd: (hd, b, 0))
    return pl.pallas_call(
        functools.partial(_attn_body, seq=seq),
        grid=(batch, HEADS),
        in_specs=[blk] * len(flat),
        out_specs=pl.BlockSpec((seq, HEAD_DIM), lambda b, hd: (b, hd)),
        out_shape=jax.ShapeDtypeStruct((batch * seq, ATTN_W), BF16),
        scratch_shapes=[pltpu.VMEM((N_GROUPS, seq, HEAD_DIM), F32),
                        pltpu.VMEM((N_GROUPS, seq, HEAD_DIM), F32)],
        compiler_params=_params(("arbitrary", "arbitrary"), 40),
        name="attn",
    )(*flat)


def _out_proj_body(x_ref, o_ref, w_ref, y_ref, wbf_ref):
    @pl.when(pl.program_id(0) == 0)
    def _():
        wbf_ref[...] = w_ref[...].astype(BF16)

    y_ref[...] = x_ref[...] + _dot(o_ref[...], wbf_ref[...])


def _out_proj(x, o, w_out):
    m = x.shape[0]
    tm = min(m, ROW_TILE)
    return pl.pallas_call(
        _out_proj_body,
        grid=(m // tm,),
        in_specs=[pl.BlockSpec((tm, D_MODEL), lambda i: (i, 0)),
                  pl.BlockSpec((tm, ATTN_W), lambda i: (i, 0)),
                  pl.BlockSpec((ATTN_W, D_MODEL), lambda i: (0, 0), pipeline_mode=pl.Buffered(1))],
        out_specs=pl.BlockSpec((tm, D_MODEL), lambda i: (i, 0)),
        out_shape=jax.ShapeDtypeStruct((m, D_MODEL), F32),
        scratch_shapes=[pltpu.VMEM((ATTN_W, D_MODEL), BF16)],
        compiler_params=_params(("arbitrary",), 48),
        name="attn_out_proj",
    )(x, o, w_out)


def _qkv_rows_body(x_ref, g_ref, w_ref, y_ref):
    h = _rms(x_ref[...], g_ref[...]).astype(BF16)
    y_ref[...] = _dot(h, w_ref[...].astype(BF16))


def _qkv_rows(x, g, w_qkv):
    m = x.shape[0]
    n = w_qkv.shape[1]
    return pl.pallas_call(
        _qkv_rows_body,
        grid=(n // QKV_COL_TILE,),
        in_specs=[pl.BlockSpec((m, D_MODEL), lambda c: (0, 0)),
                  pl.BlockSpec((1, D_MODEL), lambda c: (0, 0)),
                  pl.BlockSpec((D_MODEL, QKV_COL_TILE), lambda c: (0, c))],
        out_specs=pl.BlockSpec((m, QKV_COL_TILE), lambda c: (0, c)),
        out_shape=jax.ShapeDtypeStruct((m, n), F32),
        compiler_params=_params(("arbitrary",), 32),
        name="qkv_rows",
    )(x, g.reshape(1, D_MODEL), w_qkv)


def _cache_attn_body(y_ref, c0_ref, c1_ref, c2_ref, qn_ref, kn_ref, cos_ref, sin_ref,
                     o_ref, n0_ref, n1_ref, n2_ref, *, t_len):
    cache_refs = (c0_ref, c1_ref, c2_ref)
    new_refs = (n0_ref, n1_ref, n2_ref)

    def norm_rope(val, gain_ref, t):
        return _rope(_rms(val, gain_ref[...]), cos_ref[t:t + 1, :], sin_ref[t:t + 1, :])

    k_new = [[norm_rope(y_ref[0, t, 1, gi], kn_ref, t) for t in range(t_len)]
             for gi in range(N_GROUPS)]
    v_new = [[y_ref[0, t, 2, gi] for t in range(t_len)] for gi in range(N_GROUPS)]
    for gi in range(N_GROUPS):
        for t in range(t_len):
            new_refs[gi][0, t, 0] = k_new[gi][t]
            new_refs[gi][0, t, 1] = v_new[gi][t]

    cached_pos = lax.broadcasted_iota(jnp.int32, (ATTN_BLOCK, 1, 1), 0)
    for t in range(t_len):
        outs, lses = [], []
        for gi, (_, dil) in enumerate(ATTN_GROUPS):
            c_ref = cache_refs[gi]
            q = norm_rope(y_ref[0, t, 0, gi], qn_ref, t) * ATTN_SCALE
            if dil == 1:
                kc, vc = c_ref[0, :, 0], c_ref[0, :, 1]
                new_rows = range(t + 1)
            else:
                kc, vc = c_ref[0, :, t, 0], c_ref[0, :, t, 1]
                new_rows = [t]
            s_c = jnp.sum(kc * q[None], axis=-1, keepdims=True)
            if dil == 1:
                s_c = jnp.where(cached_pos >= t, s_c, NEG)
            s_n = [jnp.sum(k_new[gi][tt] * q, axis=-1, keepdims=True) for tt in new_rows]
            mx = functools.reduce(jnp.maximum, s_n, jnp.max(s_c, axis=0))
            p_c = jnp.exp(s_c - mx[None])
            p_n = [jnp.exp(sn - mx) for sn in s_n]
            den = jnp.sum(p_c, axis=0) + sum(p_n)
            acc = jnp.sum(p_c * vc, axis=0) + sum(p * v_new[gi][tt] for p, tt in zip(p_n, new_rows))
            outs.append(acc / den)
            lses.append(mx + jnp.log(den))
        lmax = functools.reduce(jnp.maximum, lses)
        es = [jnp.exp(l - lmax) for l in lses]
        o_ref[0, t] = sum(e * o for e, o in zip(es, outs)) / sum(es)


def _cache_attn(y, caches, q_gain, k_gain, cos, sin, batch, t_len):
    y6 = y.reshape(batch, t_len, 3, N_GROUPS, HEADS, HEAD_DIM)
    views, specs = [], []
    for cache, (window, dil) in zip(caches, ATTN_GROUPS):
        assert cache.shape[1] == window and window // dil == ATTN_BLOCK and (dil == 1 or t_len <= dil)
        if dil == 1:
            views.append(cache)
            specs.append(pl.BlockSpec((1, ATTN_BLOCK, 2, HEADS, HEAD_DIM), lambda b: (b, 0, 0, 0, 0)))
        else:
            views.append(cache.reshape(batch, ATTN_BLOCK, dil, 2, HEADS, HEAD_DIM))
            specs.append(pl.BlockSpec((1, ATTN_BLOCK, t_len, 2, HEADS, HEAD_DIM),
                                      lambda b: (b, 0, 0, 0, 0, 0)))
    new_shape = jax.ShapeDtypeStruct((batch, t_len, 2, HEADS, HEAD_DIM), F32)
    new_spec = pl.BlockSpec((1, t_len, 2, HEADS, HEAD_DIM), lambda b: (b, 0, 0, 0, 0))
    vec = pl.BlockSpec((1, HEAD_DIM), lambda b: (0, 0))
    tab = pl.BlockSpec((t_len, HEAD_DIM), lambda b: (0, 0))
    return pl.pallas_call(
        functools.partial(_cache_attn_body, t_len=t_len),
        grid=(batch,),
        in_specs=[pl.BlockSpec((1, t_len, 3, N_GROUPS, HEADS, HEAD_DIM), lambda b: (b, 0, 0, 0, 0, 0))]
        + specs + [vec, vec, tab, tab],
        out_specs=[pl.BlockSpec((1, t_len, HEADS, HEAD_DIM), lambda b: (b, 0, 0, 0))] + [new_spec] * N_GROUPS,
        out_shape=[jax.ShapeDtypeStruct((batch, t_len, HEADS, HEAD_DIM), F32)] + [new_shape] * N_GROUPS,
        compiler_params=_params(("arbitrary",), 56),
        name="cache_attn",
    )(y6, *views, q_gain.reshape(1, HEAD_DIM), k_gain.reshape(1, HEAD_DIM), cos, sin)


def kernel(x_prompt, x_sample, state_pool, cache_kv_g0, cache_kv_g1, cache_kv_g2, norm_ffn1, ffn1_w_in, ffn1_w_out, norm_mix, norm_ffn2, ffn2_w_in, ffn2_w_out, pool_w, pool_scale, chunk_w_in, chunk_v_norm, chunk_w_s, chunk_b_s, chunk_w_out, attn_w_qkv, attn_q_norm, attn_k_norm, attn_w_out):
    caches = (cache_kv_g0, cache_kv_g1, cache_kv_g2)
    batch, seq, _ = x_prompt.shape
    dec_batch, dec_seq, _ = x_sample.shape
    depth = norm_ffn1.shape[0]
    xp = x_prompt.reshape(batch * seq, D_MODEL)
    xs = x_sample.reshape(dec_batch * dec_seq, D_MODEL)
    pool_p, pool_s, chunk_s = [], [], []
    kv_p = [[] for _ in ATTN_GROUPS]
    kv_s = [[] for _ in ATTN_GROUPS]
    for i in range(depth):
        kind, j = i % 3, i // 3
        xp, xs = _ffn(xp, xs, norm_ffn1[i], ffn1_w_in[i], ffn1_w_out[i])
        if kind == 0:
            zero_buf = jnp.zeros((batch, POOL_BUF, D_MODEL), F32)
            yp, st_p = _pool(xp.reshape(batch, seq, D_MODEL), zero_buf, norm_mix[i], pool_w[j],
                             pool_scale[j], 0)
            ys, st_s = _pool(xs.reshape(dec_batch, dec_seq, D_MODEL), state_pool[j], norm_mix[i],
                             pool_w[j], pool_scale[j], PAST_LEN)
            xp = yp.reshape(batch * seq, D_MODEL)
            xs = ys.reshape(dec_batch * dec_seq, D_MODEL)
            pool_p.append(st_p)
            pool_s.append(st_s)
        elif kind == 1:
            args = (norm_mix[i], chunk_w_in[j], chunk_v_norm[j], chunk_w_s[j], chunk_b_s[j],
                    chunk_w_out[j])
            (xp,) = _chunk(xp, *args, emit_v=False)
            xs_pad = jnp.pad(xs.reshape(dec_batch, dec_seq, D_MODEL),
                             ((0, 0), (0, CHUNK - dec_seq), (0, 0)))
            ys_pad, v_pad = _chunk(xs_pad.reshape(dec_batch * CHUNK, D_MODEL), *args, emit_v=True)
            xs = ys_pad.reshape(dec_batch, CHUNK, D_MODEL)[:, :dec_seq].reshape(-1, D_MODEL)
            chunk_s.append(v_pad.reshape(dec_batch, CHUNK, D_MODEL)[:, :dec_seq])
        else:
            hs = _norm_perm(xp, norm_mix[i])
            qkvs = []
            for gi, (window, dil) in enumerate(ATTN_GROUPS):
                pos = (np.arange(seq // ROW_TILE)[:, None] * ROW_TILE + _tile_perm(dil)[None, :]).reshape(-1)
                cos_p, sin_p = _rope_tables(jnp.asarray(pos))
                q, k, v, kv_final = _qkv(hs[gi], attn_w_qkv[j], attn_q_norm[j], attn_k_norm[j],
                                         cos_p, sin_p, gi, seq)
                qkvs.append((q, k, v))
                kv_p[gi].append(kv_final.reshape(batch, min(window, seq), 2, HEADS, HEAD_DIM))
            xp = _out_proj(xp, _attn(qkvs, batch, seq), attn_w_out[j])

            cos_s, sin_s = _rope_tables(PAST_LEN + jnp.arange(dec_seq))
            y_s = _qkv_rows(xs, norm_mix[i], attn_w_qkv[j])
            o_s, *new_rows = _cache_attn(y_s, tuple(c[j] for c in caches), attn_q_norm[j],
                                         attn_k_norm[j], cos_s, sin_s, dec_batch, dec_seq)
            for gi in range(N_GROUPS):
                kv_s[gi].append(new_rows[gi])
            xs = _out_proj(xs, o_s.reshape(dec_batch * dec_seq, ATTN_W).astype(BF16), attn_w_out[j])
        xp, xs = _ffn(xp, xs, norm_ffn2[i], ffn2_w_in[i], ffn2_w_out[i])
    return (xp.reshape(batch, seq, D_MODEL), xs.reshape(dec_batch, dec_seq, D_MODEL),
            jnp.stack(pool_p), jnp.stack(pool_s), jnp.stack(chunk_s),
            jnp.stack(kv_p[0]), jnp.stack(kv_s[0]), jnp.stack(kv_p[1]), jnp.stack(kv_s[1]),
            jnp.stack(kv_p[2]), jnp.stack(kv_s[2]))
```

```python
import functools

import numpy as np
import jax
import jax.numpy as jnp
from jax import lax
from jax.experimental import pallas as pl
from jax.experimental.pallas import tpu as pltpu

F32 = jnp.float32
BF16 = jnp.bfloat16

D_MODEL = 2048
D_FF = 5504
RMS_EPS = 1e-6
POOL_WINDOWS = (2, 4, 8, 16)
POOL_GROUP = D_MODEL // len(POOL_WINDOWS)
POOL_BUF = max(POOL_WINDOWS) - 1
CHUNK = 128
CHUNK_GROUPS = 8
CHUNK_GROUP_W = D_MODEL // CHUNK_GROUPS
ATTN_GROUPS = ((128, 1), (512, 4), (2048, 16))
N_GROUPS = len(ATTN_GROUPS)
HEADS = 16
HEAD_DIM = 128
ATTN_W = HEADS * HEAD_DIM
ROPE_THETA = 10000.0
ATTN_SCALE = HEAD_DIM ** -0.5
NEG = float(np.finfo(np.float32).min)
PAST_LEN = 16384

LANE = 128
SUBLANE = 8
FF_TILE = 512
N_FF_TILES = -(-D_FF // FF_TILE)
FFN_ROW_TILE = 1024
ROW_TILE = 512
CHUNK_ROW_TILE = 256
ATTN_BLOCK = 128
ATTN_INTERLEAVE = 8
QKV_COL_TILE = 1024
HEADS_PER_STEP = QKV_COL_TILE // HEAD_DIM
COL_TILES_PER_ROLE = ATTN_W // QKV_COL_TILE
SLAB_PITCH = ROW_TILE + SUBLANE
MIB = 1024 * 1024


def _params(semantics, vmem_mib):
    return pltpu.CompilerParams(dimension_semantics=semantics, vmem_limit_bytes=vmem_mib * MIB)


def _rms(x, g):
    ms = jnp.mean(x * x, axis=-1, keepdims=True)
    return x * lax.rsqrt(ms + RMS_EPS) * g


def _dot(a, b):
    return jnp.dot(a, b, preferred_element_type=F32)


def _dot_nt(a, b):
    return lax.dot_general(a, b, (((1,), (1,)), ((), ())), preferred_element_type=F32)


def _ff_start(f, base=0):
    return (base // LANE + jnp.minimum(f * (FF_TILE // LANE), (D_FF - FF_TILE) // LANE)) * LANE


def _ffn_body(xp_ref, xs_ref, g_ref, wg_ref, wu_ref, wo_ref, yp_ref, ys_ref, hp_ref, hs_ref):
    m = pl.program_id(0)
    f = pl.program_id(1)

    def init(x_ref, h_ref, y_ref):
        x = x_ref[...]
        h_ref[...] = _rms(x, g_ref[...]).astype(BF16)
        y_ref[...] = x

    col = lax.broadcasted_iota(jnp.int32, (1, FF_TILE), 1)
    fresh = col >= f * FF_TILE - _ff_start(f)

    def accumulate(h_ref, y_ref):
        h = h_ref[...]
        gate = _dot(h, wg_ref[...])
        up = _dot(h, wu_ref[...])
        hid = jnp.where(fresh, gate * jax.nn.sigmoid(gate) * up * 0.5, 0.0).astype(BF16)
        y_ref[...] += _dot(hid, wo_ref[...])

    @pl.when(f == 0)
    def _():
        init(xp_ref, hp_ref, yp_ref)

    @pl.when((f == 0) & (m == 0))
    def _():
        init(xs_ref, hs_ref, ys_ref)

    accumulate(hp_ref, yp_ref)

    @pl.when(m == 0)
    def _():
        accumulate(hs_ref, ys_ref)


def _ffn(xp, xs, g, w_in, w_out):
    mp, ms = xp.shape[0], xs.shape[0]
    elem = pl.Element
    w_in = w_in.astype(BF16)
    w_out = w_out.astype(BF16)
    return pl.pallas_call(
        _ffn_body,
        grid=(mp // FFN_ROW_TILE, N_FF_TILES),
        in_specs=[
            pl.BlockSpec((FFN_ROW_TILE, D_MODEL), lambda m, f: (m, 0)),
            pl.BlockSpec((ms, D_MODEL), lambda m, f: (0, 0)),
            pl.BlockSpec((1, D_MODEL), lambda m, f: (0, 0)),
            pl.BlockSpec((elem(D_MODEL), elem(FF_TILE)), lambda m, f: (0, _ff_start(f))),
            pl.BlockSpec((elem(D_MODEL), elem(FF_TILE)), lambda m, f: (0, _ff_start(f, D_FF))),
            pl.BlockSpec((elem(FF_TILE), elem(D_MODEL)), lambda m, f: (_ff_start(f), 0)),
        ],
        out_specs=[
            pl.BlockSpec((FFN_ROW_TILE, D_MODEL), lambda m, f: (m, 0)),
            pl.BlockSpec((ms, D_MODEL), lambda m, f: (0, 0)),
        ],
        out_shape=[jax.ShapeDtypeStruct(xp.shape, F32), jax.ShapeDtypeStruct(xs.shape, F32)],
        scratch_shapes=[pltpu.VMEM((FFN_ROW_TILE, D_MODEL), BF16), pltpu.VMEM((ms, D_MODEL), BF16)],
        compiler_params=_params(("arbitrary", "arbitrary"), 60),
        name="ffn",
    )(xp, xs, g.reshape(1, D_MODEL), w_in, w_in, w_out)


def _pool_body(x_ref, buf_ref, g_ref, pw_ref, sc_ref, y_ref, st_ref, hb_ref, *, tt, start, nt):
    t = pl.program_id(1)
    halo = POOL_BUF + 1

    @pl.when(t == 0)
    def _():
        hb_ref[0:halo, :] = buf_ref[0]

    x = x_ref[0]
    h = _rms(x, g_ref[...])
    hb_ref[halo:halo + tt, :] = h
    pos = start + t * tt + lax.broadcasted_iota(jnp.int32, (tt, 1), 0)
    for gi, w in enumerate(POOL_WINDOWS):
        sl = slice(gi * POOL_GROUP, (gi + 1) * POOL_GROUP)
        win = hb_ref[halo:halo + tt, sl]
        for k in range(1, w):
            win = win + hb_ref[halo - k:halo - k + tt, sl]
        count = jnp.minimum(w, pos + 1).astype(F32)
        pooled = win / count - h[:, sl]
        mixed = _dot(pooled.astype(BF16), pw_ref[gi])
        y_ref[0, :, sl] = x[:, sl] + mixed * sc_ref[:, sl]
    st_ref[0] = hb_ref[tt + 1:tt + halo, :]
    if nt > 1:
        hb_ref[0:halo, :] = hb_ref[tt:tt + halo, :]


def _pool(x, buf, g, pool_w, scale, start):
    b, t_len, _ = x.shape
    tt = min(t_len, ROW_TILE)
    nt = t_len // tt
    halo = POOL_BUF + 1
    buf16 = jnp.pad(buf, ((0, 0), (1, 0), (0, 0)))
    body = functools.partial(_pool_body, tt=tt, start=start, nt=nt)
    return pl.pallas_call(
        body,
        grid=(b, nt),
        in_specs=[
            pl.BlockSpec((1, tt, D_MODEL), lambda i, t: (i, t, 0)),
            pl.BlockSpec((1, halo, D_MODEL), lambda i, t: (i, 0, 0)),
            pl.BlockSpec((1, D_MODEL), lambda i, t: (0, 0)),
            pl.BlockSpec((len(POOL_WINDOWS), POOL_GROUP, POOL_GROUP), lambda i, t: (0, 0, 0)),
            pl.BlockSpec((1, D_MODEL), lambda i, t: (0, 0)),
        ],
        out_specs=[
            pl.BlockSpec((1, tt, D_MODEL), lambda i, t: (i, t, 0)),
            pl.BlockSpec((1, POOL_BUF, D_MODEL), lambda i, t: (i, 0, 0)),
        ],
        out_shape=[jax.ShapeDtypeStruct(x.shape, F32),
                   jax.ShapeDtypeStruct((b, POOL_BUF, D_MODEL), F32)],
        scratch_shapes=[pltpu.VMEM((halo + tt, D_MODEL), F32)],
        compiler_params=_params(("arbitrary", "arbitrary"), 40),
        name="pool",
    )(x, buf16, g.reshape(1, D_MODEL), pool_w.astype(BF16), scale.reshape(1, D_MODEL))


def _chunk_body(x_ref, g_ref, win_ref, vg_ref, ws_ref, bst_ref, wout_ref, *rest, tm, emit_v):
    if emit_v:
        y_ref, vn_ref, mix_ref = rest
    else:
        y_ref, mix_ref = rest
    x = x_ref[...]
    h = _rms(x, g_ref[...]).astype(BF16)
    uv = _dot(h, win_ref[...])
    uv = 0.5 * uv * (1.0 + lax.erf(uv * np.float32(np.sqrt(0.5))))
    u = uv[:, :D_MODEL]
    vn = _rms(uv[:, D_MODEL:], vg_ref[...])
    if emit_v:
        vn_ref[...] = vn
    q_idx = lax.broadcasted_iota(jnp.int32, (CHUNK, CHUNK), 0)
    c_idx = lax.broadcasted_iota(jnp.int32, (CHUNK, CHUNK), 1)
    causal = c_idx <= q_idx
    for gi in range(CHUNK_GROUPS):
        cols = slice(gi * CHUNK_GROUP_W, (gi + 1) * CHUNK_GROUP_W)
        ws = jnp.where(causal, ws_ref[gi], 0.0).astype(BF16)
        bias = bst_ref[:, gi:gi + 1]
        for c in range(tm // CHUNK):
            rows = slice(c * CHUNK, (c + 1) * CHUNK)
            mixed = _dot(ws, vn[rows, cols].astype(BF16)) + bias
            mix_ref[rows, cols] = (u[rows, cols] * mixed).astype(BF16)
    y_ref[...] = x + _dot(mix_ref[...], wout_ref[...])


def _chunk(x, g, w_in, v_gain, w_s, b_s, w_out, emit_v):
    m = x.shape[0]
    tm = CHUNK_ROW_TILE
    body = functools.partial(_chunk_body, tm=tm, emit_v=emit_v)
    row_spec = pl.BlockSpec((tm, D_MODEL), lambda i: (i, 0))
    once = pl.Buffered(1)
    out_specs = [row_spec]
    out_shape = [jax.ShapeDtypeStruct((m, D_MODEL), F32)]
    if emit_v:
        out_specs.append(row_spec)
        out_shape.append(jax.ShapeDtypeStruct((m, D_MODEL), F32))
    return pl.pallas_call(
        body,
        grid=(m // tm,),
        in_specs=[
            row_spec,
            pl.BlockSpec((1, D_MODEL), lambda i: (0, 0)),
            pl.BlockSpec((D_MODEL, 2 * D_MODEL), lambda i: (0, 0), pipeline_mode=once),
            pl.BlockSpec((1, D_MODEL), lambda i: (0, 0)),
            pl.BlockSpec((CHUNK_GROUPS, CHUNK, CHUNK), lambda i: (0, 0, 0)),
            pl.BlockSpec((CHUNK, CHUNK_GROUPS), lambda i: (0, 0)),
            pl.BlockSpec((D_MODEL, D_MODEL), lambda i: (0, 0), pipeline_mode=once),
        ],
        out_specs=out_specs,
        out_shape=out_shape,
        scratch_shapes=[pltpu.VMEM((tm, D_MODEL), BF16)],
        compiler_params=_params(("arbitrary",), 56),
        name="chunk",
    )(x, g.reshape(1, D_MODEL), w_in.astype(BF16), v_gain.reshape(1, D_MODEL), w_s, b_s.T,
      w_out.astype(BF16))


def _tile_perm(dil):
    u = np.arange(ROW_TILE)
    n_k = ROW_TILE // dil
    return (u % n_k) * dil + u // n_k


def _rope_tables(pos):
    half = HEAD_DIM // 2
    freqs = ROPE_THETA ** (-2.0 * jnp.arange(half, dtype=F32) / HEAD_DIM)
    ang = pos.astype(F32)[:, None] * freqs[None, :]
    cos, sin = jnp.cos(ang), jnp.sin(ang)
    return jnp.concatenate([cos, cos], axis=-1), jnp.concatenate([-sin, sin], axis=-1)


def _rope(x, cos, sin_signed):
    return x * cos + pltpu.roll(x, HEAD_DIM // 2, 1) * sin_signed


def _norm_perm_body(x_ref, g_ref, *h_refs):
    h = _rms(x_ref[...], g_ref[...]).astype(BF16)
    u = lax.broadcasted_iota(jnp.int32, (ROW_TILE, ROW_TILE), 0)
    t = lax.broadcasted_iota(jnp.int32, (ROW_TILE, ROW_TILE), 1)
    for h_ref, (_, dil) in zip(h_refs, ATTN_GROUPS):
        if dil == 1:
            h_ref[...] = h
        else:
            n_k = ROW_TILE // dil
            src = (u & (n_k - 1)) * dil + lax.shift_right_logical(u, n_k.bit_length() - 1)
            select = jnp.where(t == src, 1.0, 0.0).astype(BF16)
            h_ref[...] = _dot(select, h).astype(BF16)


def _norm_perm(x, g):
    m = x.shape[0]
    row_spec = pl.BlockSpec((ROW_TILE, D_MODEL), lambda i: (i, 0))
    return pl.pallas_call(
        _norm_perm_body,
        grid=(m // ROW_TILE,),
        in_specs=[row_spec, pl.BlockSpec((1, D_MODEL), lambda i: (0, 0))],
        out_specs=[row_spec] * N_GROUPS,
        out_shape=[jax.ShapeDtypeStruct((m, D_MODEL), BF16)] * N_GROUPS,
        compiler_params=_params(("arbitrary",), 40),
        name="norm_perm",
    )(x, g.reshape(1, D_MODEL))


def _qkv_body(h_ref, w_ref, qn_ref, kn_ref, cos_ref, sin_ref, q_ref, k_ref, v_ref, kvf_ref,
              wbf_ref, ya_ref, yb_ref, slab_ref, *, dil, final_rows, tiles_per_seq, n_m, n_steps):
    p = pl.program_id(0)
    done = jnp.maximum(p - 1, 0)
    role = (done // n_m) // COL_TILES_PER_ROLE
    row_tile = done % n_m

    @pl.when(p == 0)
    def _():
        yb_ref[...] = jnp.zeros_like(yb_ref)

    @pl.when((jnp.minimum(p, n_steps - 1) % n_m == 0) & (p < n_steps))
    def _():
        wbf_ref[...] = w_ref[...].astype(BF16)

    def to_slab(hl, val):
        slab_ref[hl * SLAB_PITCH:hl * SLAB_PITCH + ROW_TILE, :] = val

    def gather_final():
        n_k = ROW_TILE // dil
        for t in range(final_rows):
            t_nat = (ROW_TILE - final_rows) + t
            u = (t_nat % dil) * n_k + t_nat // dil
            kvf_ref[t, 0] = slab_ref[pl.ds(u, HEADS_PER_STEP, stride=SLAB_PITCH), :]

    def emit_final():
        if tiles_per_seq == 1:
            gather_final()
        else:
            pl.when(row_tile % tiles_per_seq == tiles_per_seq - 1)(gather_final)

    def finish(which, y_ref):
        for hl in range(HEADS_PER_STEP):
            yh = y_ref[:, hl * HEAD_DIM:(hl + 1) * HEAD_DIM]
            if which == 0:
                qn = _rope(_rms(yh, qn_ref[...]), cos_ref[...], sin_ref[...])
                q_ref[hl] = (qn * ATTN_SCALE).astype(BF16)
            elif which == 1:
                kn = _rope(_rms(yh, kn_ref[...]), cos_ref[...], sin_ref[...])
                k_ref[hl] = kn.astype(BF16)
                to_slab(hl, kn)
            else:
                v_ref[hl] = yh.astype(BF16)
                to_slab(hl, yh)
        if which > 0:
            emit_final()

    for parity, (cur_ref, prev_ref) in enumerate(((ya_ref, yb_ref), (yb_ref, ya_ref))):
        for which in range(3):
            @pl.when((p % 2 == parity) & (role == which))
            def _(which=which, cur_ref=cur_ref, prev_ref=prev_ref):
                cur_ref[...] = _dot(h_ref[...], wbf_ref[...])
                finish(which, prev_ref)


def _qkv(h, w_qkv, q_gain, k_gain, cos, sin, group, seq):
    window, dil = ATTN_GROUPS[group]
    m = h.shape[0]
    n_m = m // ROW_TILE
    tiles_per_seq = seq // ROW_TILE
    keep = min(window, seq)
    final_rows = min(keep, ROW_TILE)
    every_tile = keep == seq
    halves = COL_TILES_PER_ROLE
    n_steps = 3 * halves * n_m

    def project_tile(p):
        q = jnp.minimum(p, n_steps - 1)
        return q // n_m, q % n_m

    def finish_tile(p):
        e = jnp.maximum(p - 1, 0)
        return e // n_m, e % n_m

    def att_idx(s_own):
        def idx(p):
            c, i = finish_tile(p)
            s, hh = c // halves, c % halves
            before, after = s < s_own, s > s_own
            return (jnp.where(before, 0, jnp.where(after, halves - 1, hh)),
                    jnp.where(before, 0, jnp.where(after, n_m - 1, i)), 0)
        return idx

    def final_idx(p):
        c, i = finish_tile(p)
        s, hh = c // halves, c % halves
        rb = i if every_tile else i // tiles_per_seq
        live = s > 0
        return (jnp.where(live, rb, 0), jnp.where(live, s - 1, 0), jnp.where(live, hh, 0), 0)

    def w_idx(p):
        c, _ = project_tile(p)
        return (0, ((c // halves) * N_GROUPS + group) * halves + c % halves)

    body = functools.partial(_qkv_body, dil=dil, final_rows=final_rows, n_m=n_m, n_steps=n_steps,
                             tiles_per_seq=1 if every_tile else tiles_per_seq)
    att_shape = jax.ShapeDtypeStruct((HEADS, m, HEAD_DIM), BF16)
    att_block = (HEADS_PER_STEP, ROW_TILE, HEAD_DIM)
    table = pl.BlockSpec((ROW_TILE, HEAD_DIM), lambda p: (finish_tile(p)[1] % tiles_per_seq, 0))
    batch = m // seq
    return pl.pallas_call(
        body,
        grid=(n_steps + 1,),
        in_specs=[
            pl.BlockSpec((ROW_TILE, D_MODEL), lambda p: (project_tile(p)[1], 0)),
            pl.BlockSpec((D_MODEL, QKV_COL_TILE), w_idx),
            pl.BlockSpec((1, HEAD_DIM), lambda p: (0, 0)),
            pl.BlockSpec((1, HEAD_DIM), lambda p: (0, 0)),
            table,
            table,
        ],
        out_specs=[
            pl.BlockSpec(att_block, att_idx(0)),
            pl.BlockSpec(att_block, att_idx(1)),
            pl.BlockSpec(att_block, att_idx(2)),
            pl.BlockSpec((final_rows, 1, HEADS_PER_STEP, HEAD_DIM), final_idx),
        ],
        out_shape=[att_shape, att_shape, att_shape,
                   jax.ShapeDtypeStruct((batch * keep, 2, HEADS, HEAD_DIM), F32)],
        scratch_shapes=[pltpu.VMEM((D_MODEL, QKV_COL_TILE), BF16),
                        pltpu.VMEM((ROW_TILE, QKV_COL_TILE), F32),
                        pltpu.VMEM((ROW_TILE, QKV_COL_TILE), F32),
                        pltpu.VMEM((HEADS_PER_STEP * SLAB_PITCH, HEAD_DIM), F32)],
        compiler_params=_params(("arbitrary",), 48),
        name="qkv",
    )(h, w_qkv, q_gain.reshape(1, HEAD_DIM), k_gain.reshape(1, HEAD_DIM), cos, sin)


def _band_mask(n_keys):
    qi = lax.broadcasted_iota(jnp.int32, (ATTN_BLOCK, n_keys), 0)
    kj = lax.broadcasted_iota(jnp.int32, (ATTN_BLOCK, n_keys), 1)
    dist = (n_keys - ATTN_BLOCK) + qi - kj
    return (dist >= 0) & (dist <= ATTN_BLOCK)


def _attn_body(*refs, seq):
    qkv_refs, o_ref, og_ref, lg_ref = refs[:3 * N_GROUPS], refs[3 * N_GROUPS], refs[-2], refs[-1]
    masks = {n: _band_mask(n) for n in (ATTN_BLOCK, 2 * ATTN_BLOCK)}
    for gi, (_, dil) in enumerate(ATTN_GROUPS):
        q_ref, k_ref, v_ref = qkv_refs[3 * gi:3 * gi + 3]
        n_k = ROW_TILE // dil
        n_blocks = seq // dil // ATTN_BLOCK

        def rows_of(ref, r, blk):
            if n_k >= ATTN_BLOCK:
                per_tile = n_k // ATTN_BLOCK
                start = (blk // per_tile) * ROW_TILE + r * n_k + (blk % per_tile) * ATTN_BLOCK
                return ref[0, start:start + ATTN_BLOCK, :]
            parts = [ref[0, tl * ROW_TILE + r * n_k:tl * ROW_TILE + (r + 1) * n_k, :]
                     for tl in range(blk * (ATTN_BLOCK // n_k), (blk + 1) * (ATTN_BLOCK // n_k))]
            return jnp.concatenate(parts, axis=0)

        def keys_of(ref, r, blk):
            if blk == 0:
                return rows_of(ref, r, 0)
            return jnp.concatenate([rows_of(ref, r, blk - 1), rows_of(ref, r, blk)], axis=0)

        todo = [(r, blk) for r in range(dil) for blk in range(n_blocks)]
        for i in range(0, len(todo), ATTN_INTERLEAVE):
            batch = todo[i:i + ATTN_INTERLEAVE]
            ks = [keys_of(k_ref, r, blk) for r, blk in batch]
            ss = [jnp.where(masks[k.shape[0]], _dot_nt(rows_of(q_ref, r, blk), k), NEG)
                  for (r, blk), k in zip(batch, ks)]
            mxs = [jnp.max(s, axis=-1, keepdims=True) for s in ss]
            ps = [jnp.exp(s - mx) for s, mx in zip(ss, mxs)]
            dens = [jnp.sum(p, axis=-1, keepdims=True) for p in ps]
            accs = [_dot(p.astype(BF16), keys_of(v_ref, r, blk)) for p, (r, blk) in zip(ps, batch)]
            for (r, blk), acc, mx, den in zip(batch, accs, mxs, dens):
                if dil == 1:
                    dst = pl.ds(blk * ATTN_BLOCK, ATTN_BLOCK)
                else:
                    dst = pl.ds(blk * ATTN_BLOCK * dil + r, ATTN_BLOCK, stride=dil)
                og_ref[gi, dst, :] = acc / den
                lg_ref[gi, dst, :] = jnp.broadcast_to(mx + jnp.log(den), (ATTN_BLOCK, HEAD_DIM))
    for blk in range(seq // ATTN_BLOCK):
        rows = slice(blk * ATTN_BLOCK, (blk + 1) * ATTN_BLOCK)
        ls = [lg_ref[gi, rows, :] for gi in range(N_GROUPS)]
        mx = functools.reduce(jnp.maximum, ls)
        es = [jnp.exp(l - mx) for l in ls]
        num = sum(e * og_ref[gi, rows, :] for gi, e in enumerate(es))
        o_ref[rows, :] = (num / sum(es)).astype(BF16)


def _attn(qkvs, batch, seq):
    flat = [a for qkv in qkvs for a in qkv]
    blk = pl.BlockSpec((1, seq, HEAD_DIM), lambda b, hd: (hd, b, 0))
    return pl.pallas_call(
        functools.partial(_attn_body, seq=seq),
        grid=(batch, HEADS),
        in_specs=[blk] * len(flat),
        out_specs=pl.BlockSpec((seq, HEAD_DIM), lambda b, hd: (b, hd)),
        out_shape=jax.ShapeDtypeStruct((batch * seq, ATTN_W), BF16),
        scratch_shapes=[pltpu.VMEM((N_GROUPS, seq, HEAD_DIM), F32),
                        pltpu.VMEM((N_GROUPS, seq, HEAD_DIM), F32)],
        compiler_params=_params(("arbitrary", "arbitrary"), 40),
        name="attn",
    )(*flat)


def _out_proj_body(x_ref, o_ref, w_ref, y_ref, wbf_ref):
    @pl.when(pl.program_id(0) == 0)
    def _():
        wbf_ref[...] = w_ref[...].astype(BF16)

    y_ref[...] = x_ref[...] + _dot(o_ref[...], wbf_ref[...])


def _out_proj(x, o, w_out):
    m = x.shape[0]
    tm = min(m, ROW_TILE)
    return pl.pallas_call(
        _out_proj_body,
        grid=(m // tm,),
        in_specs=[pl.BlockSpec((tm, D_MODEL), lambda i: (i, 0)),
                  pl.BlockSpec((tm, ATTN_W), lambda i: (i, 0)),
                  pl.BlockSpec((ATTN_W, D_MODEL), lambda i: (0, 0), pipeline_mode=pl.Buffered(1))],
        out_specs=pl.BlockSpec((tm, D_MODEL), lambda i: (i, 0)),
        out_shape=jax.ShapeDtypeStruct((m, D_MODEL), F32),
        scratch_shapes=[pltpu.VMEM((ATTN_W, D_MODEL), BF16)],
        compiler_params=_params(("arbitrary",), 48),
        name="attn_out_proj",
    )(x, o, w_out)


def _qkv_rows_body(x_ref, g_ref, w_ref, y_ref):
    h = _rms(x_ref[...], g_ref[...]).astype(BF16)
    y_ref[...] = _dot(h, w_ref[...].astype(BF16))


def _qkv_rows(x, g, w_qkv):
    m = x.shape[0]
    n = w_qkv.shape[1]
    return pl.pallas_call(
        _qkv_rows_body,
        grid=(n // QKV_COL_TILE,),
        in_specs=[pl.BlockSpec((m, D_MODEL), lambda c: (0, 0)),
                  pl.BlockSpec((1, D_MODEL), lambda c: (0, 0)),
                  pl.BlockSpec((D_MODEL, QKV_COL_TILE), lambda c: (0, c))],
        out_specs=pl.BlockSpec((m, QKV_COL_TILE), lambda c: (0, c)),
        out_shape=jax.ShapeDtypeStruct((m, n), F32),
        compiler_params=_params(("arbitrary",), 32),
        name="qkv_rows",
    )(x, g.reshape(1, D_MODEL), w_qkv)


def _cache_attn_body(y_ref, c0_ref, c1_ref, c2_ref, qn_ref, kn_ref, cos_ref, sin_ref,
                     o_ref, n0_ref, n1_ref, n2_ref, *, t_len):
    cache_refs = (c0_ref, c1_ref, c2_ref)
    new_refs = (n0_ref, n1_ref, n2_ref)

    def norm_rope(val, gain_ref, t):
        return _rope(_rms(val, gain_ref[...]), cos_ref[t:t + 1, :], sin_ref[t:t + 1, :])

    k_new = [[norm_rope(y_ref[0, t, 1, gi], kn_ref, t) for t in range(t_len)]
             for gi in range(N_GROUPS)]
    v_new = [[y_ref[0, t, 2, gi] for t in range(t_len)] for gi in range(N_GROUPS)]
    for gi in range(N_GROUPS):
        for t in range(t_len):
            new_refs[gi][0, t, 0] = k_new[gi][t]
            new_refs[gi][0, t, 1] = v_new[gi][t]

    cached_pos = lax.broadcasted_iota(jnp.int32, (ATTN_BLOCK, 1, 1), 0)
    for t in range(t_len):
        outs, lses = [], []
        for gi, (_, dil) in enumerate(ATTN_GROUPS):
            c_ref = cache_refs[gi]
            q = norm_rope(y_ref[0, t, 0, gi], qn_ref, t) * ATTN_SCALE
            if dil == 1:
                kc, vc = c_ref[0, :, 0], c_ref[0, :, 1]
                new_rows = range(t + 1)
            else:
                kc, vc = c_ref[0, :, t, 0], c_ref[0, :, t, 1]
                new_rows = [t]
            s_c = jnp.sum(kc * q[None], axis=-1, keepdims=True)
            if dil == 1:
                s_c = jnp.where(cached_pos >= t, s_c, NEG)
            s_n = [jnp.sum(k_new[gi][tt] * q, axis=-1, keepdims=True) for tt in new_rows]
            mx = functools.reduce(jnp.maximum, s_n, jnp.max(s_c, axis=0))
            p_c = jnp.exp(s_c - mx[None])
            p_n = [jnp.exp(sn - mx) for sn in s_n]
            den = jnp.sum(p_c, axis=0) + sum(p_n)
            acc = jnp.sum(p_c * vc, axis=0) + sum(p * v_new[gi][tt] for p, tt in zip(p_n, new_rows))
            outs.append(acc / den)
            lses.append(mx + jnp.log(den))
        lmax = functools.reduce(jnp.maximum, lses)
        es = [jnp.exp(l - lmax) for l in lses]
        o_ref[0, t] = sum(e * o for e, o in zip(es, outs)) / sum(es)


def _cache_attn(y, caches, q_gain, k_gain, cos, sin, batch, t_len):
    y6 = y.reshape(batch, t_len, 3, N_GROUPS, HEADS, HEAD_DIM)
    views, specs = [], []
    for cache, (window, dil) in zip(caches, ATTN_GROUPS):
        assert cache.shape[1] == window and window // dil == ATTN_BLOCK and (dil == 1 or t_len <= dil)
        if dil == 1:
            views.append(cache)
            specs.append(pl.BlockSpec((1, ATTN_BLOCK, 2, HEADS, HEAD_DIM), lambda b: (b, 0, 0, 0, 0)))
        else:
            views.append(cache.reshape(batch, ATTN_BLOCK, dil, 2, HEADS, HEAD_DIM))
            specs.append(pl.BlockSpec((1, ATTN_BLOCK, t_len, 2, HEADS, HEAD_DIM),
                                      lambda b: (b, 0, 0, 0, 0, 0)))
    new_shape = jax.ShapeDtypeStruct((batch, t_len, 2, HEADS, HEAD_DIM), F32)
    new_spec = pl.BlockSpec((1, t_len, 2, HEADS, HEAD_DIM), lambda b: (b, 0, 0, 0, 0))
    vec = pl.BlockSpec((1, HEAD_DIM), lambda b: (0, 0))
    tab = pl.BlockSpec((t_len, HEAD_DIM), lambda b: (0, 0))
    return pl.pallas_call(
        functools.partial(_cache_attn_body, t_len=t_len),
        grid=(batch,),
        in_specs=[pl.BlockSpec((1, t_len, 3, N_GROUPS, HEADS, HEAD_DIM), lambda b: (b, 0, 0, 0, 0, 0))]
        + specs + [vec, vec, tab, tab],
        out_specs=[pl.BlockSpec((1, t_len, HEADS, HEAD_DIM), lambda b: (b, 0, 0, 0))] + [new_spec] * N_GROUPS,
        out_shape=[jax.ShapeDtypeStruct((batch, t_len, HEADS, HEAD_DIM), F32)] + [new_shape] * N_GROUPS,
        compiler_params=_params(("arbitrary",), 56),
        name="cache_attn",
    )(y6, *views, q_gain.reshape(1, HEAD_DIM), k_gain.reshape(1, HEAD_DIM), cos, sin)


def kernel(x_prompt, x_sample, state_pool, cache_kv_g0, cache_kv_g1, cache_kv_g2, norm_ffn1, ffn1_w_in, ffn1_w_out, norm_mix, norm_ffn2, ffn2_w_in, ffn2_w_out, pool_w, pool_scale, chunk_w_in, chunk_v_norm, chunk_w_s, chunk_b_s, chunk_w_out, attn_w_qkv, attn_q_norm, attn_k_norm, attn_w_out):
    caches = (cache_kv_g0, cache_kv_g1, cache_kv_g2)
    batch, seq, _ = x_prompt.shape
    dec_batch, dec_seq, _ = x_sample.shape
    depth = norm_ffn1.shape[0]
    xp = x_prompt.reshape(batch * seq, D_MODEL)
    xs = x_sample.reshape(dec_batch * dec_seq, D_MODEL)
    pool_p, pool_s, chunk_s = [], [], []
    kv_p = [[] for _ in ATTN_GROUPS]
    kv_s = [[] for _ in ATTN_GROUPS]
    for i in range(depth):
        kind, j = i % 3, i // 3
        xp, xs = _ffn(xp, xs, norm_ffn1[i], ffn1_w_in[i], ffn1_w_out[i])
        if kind == 0:
            zero_buf = jnp.zeros((batch, POOL_BUF, D_MODEL), F32)
            yp, st_p = _pool(xp.reshape(batch, seq, D_MODEL), zero_buf, norm_mix[i], pool_w[j],
                             pool_scale[j], 0)
            ys, st_s = _pool(xs.reshape(dec_batch, dec_seq, D_MODEL), state_pool[j], norm_mix[i],
                             pool_w[j], pool_scale[j], PAST_LEN)
            xp = yp.reshape(batch * seq, D_MODEL)
            xs = ys.reshape(dec_batch * dec_seq, D_MODEL)
            pool_p.append(st_p)
            pool_s.append(st_s)
        elif kind == 1:
            args = (norm_mix[i], chunk_w_in[j], chunk_v_norm[j], chunk_w_s[j], chunk_b_s[j],
                    chunk_w_out[j])
            (xp,) = _chunk(xp, *args, emit_v=False)
            xs_pad = jnp.pad(xs.reshape(dec_batch, dec_seq, D_MODEL),
                             ((0, 0), (0, CHUNK - dec_seq), (0, 0)))
            ys_pad, v_pad = _chunk(xs_pad.reshape(dec_batch * CHUNK, D_MODEL), *args, emit_v=True)
            xs = ys_pad.reshape(dec_batch, CHUNK, D_MODEL)[:, :dec_seq].reshape(-1, D_MODEL)
            chunk_s.append(v_pad.reshape(dec_batch, CHUNK, D_MODEL)[:, :dec_seq])
        else:
            hs = _norm_perm(xp, norm_mix[i])
            qkvs = []
            for gi, (window, dil) in enumerate(ATTN_GROUPS):
                pos = (np.arange(seq // ROW_TILE)[:, None] * ROW_TILE + _tile_perm(dil)[None, :]).reshape(-1)
                cos_p, sin_p = _rope_tables(jnp.asarray(pos))
                q, k, v, kv_final = _qkv(hs[gi], attn_w_qkv[j], attn_q_norm[j], attn_k_norm[j],
                                         cos_p, sin_p, gi, seq)
                qkvs.append((q, k, v))
                kv_p[gi].append(kv_final.reshape(batch, min(window, seq), 2, HEADS, HEAD_DIM))
            xp = _out_proj(xp, _attn(qkvs, batch, seq), attn_w_out[j])

            cos_s, sin_s = _rope_tables(PAST_LEN + jnp.arange(dec_seq))
            y_s = _qkv_rows(xs, norm_mix[i], attn_w_qkv[j])
            o_s, *new_rows = _cache_attn(y_s, tuple(c[j] for c in caches), attn_q_norm[j],
                                         attn_k_norm[j], cos_s, sin_s, dec_batch, dec_seq)
            for gi in range(N_GROUPS):
                kv_s[gi].append(new_rows[gi])
            xs = _out_proj(xs, o_s.reshape(dec_batch * dec_seq, ATTN_W).astype(BF16), attn_w_out[j])
        xp, xs = _ffn(xp, xs, norm_ffn2[i], ffn2_w_in[i], ffn2_w_out[i])
    return (xp.reshape(batch, seq, D_MODEL), xs.reshape(dec_batch, dec_seq, D_MODEL),
            jnp.stack(pool_p), jnp.stack(pool_s), jnp.stack(chunk_s),
            jnp.stack(kv_p[0]), jnp.stack(kv_s[0]), jnp.stack(kv_p[1]), jnp.stack(kv_s[1]),
            jnp.stack(kv_p[2]), jnp.stack(kv_s[2]))
```

```python
import functools

import numpy as np
import jax
import jax.numpy as jnp
from jax import lax
from jax.experimental import pallas as pl
from jax.experimental.pallas import tpu as pltpu

F32 = jnp.float32
BF16 = jnp.bfloat16

D_MODEL = 2048
D_FF = 5504
RMS_EPS = 1e-6
POOL_WINDOWS = (2, 4, 8, 16)
POOL_GROUP = D_MODEL // len(POOL_WINDOWS)
POOL_BUF = max(POOL_WINDOWS) - 1
CHUNK = 128
CHUNK_GROUPS = 8
CHUNK_GROUP_W = D_MODEL // CHUNK_GROUPS
ATTN_GROUPS = ((128, 1), (512, 4), (2048, 16))
N_GROUPS = len(ATTN_GROUPS)
HEADS = 16
HEAD_DIM = 128
ATTN_W = HEADS * HEAD_DIM
ROPE_THETA = 10000.0
ATTN_SCALE = HEAD_DIM ** -0.5
NEG = float(np.finfo(np.float32).min)
PAST_LEN = 16384

LANE = 128
SUBLANE = 8
FF_TILE = 512
N_FF_TILES = -(-D_FF // FF_TILE)
FFN_ROW_TILE = 1024
CONVERT_IN_ROWS = 32
CONVERT_OUT_ROWS = 64
ROW_TILE = 512
CHUNK_ROW_TILE = 256
ATTN_BLOCK = 128
ATTN_INTERLEAVE = 8
QKV_COL_TILE = 1024
HEADS_PER_STEP = QKV_COL_TILE // HEAD_DIM
COL_TILES_PER_ROLE = ATTN_W // QKV_COL_TILE
SLAB_PITCH = ROW_TILE + SUBLANE
MIB = 1024 * 1024


def _params(semantics, vmem_mib):
    return pltpu.CompilerParams(dimension_semantics=semantics, vmem_limit_bytes=vmem_mib * MIB)


def _rms(x, g):
    ms = jnp.mean(x * x, axis=-1, keepdims=True)
    return x * lax.rsqrt(ms + RMS_EPS) * g


def _dot(a, b):
    return jnp.dot(a, b, preferred_element_type=F32)


def _dot_nt(a, b):
    return lax.dot_general(a, b, (((1,), (1,)), ((), ())), preferred_element_type=F32)


def _ff_start(f, base=0):
    return (base // LANE + jnp.minimum(f * (FF_TILE // LANE), (D_FF - FF_TILE) // LANE)) * LANE


def _ffn_body(xp_ref, xs_ref, g_ref, wg_ref, wu_ref, wo_ref, *rest, convert_next):
    if convert_next:
        nwi_ref, nwo_ref, yp_ref, ys_ref, nwi_bf_ref, nwo_bf_ref, hp_ref, hs_ref = rest
    else:
        yp_ref, ys_ref, hp_ref, hs_ref = rest
    m = pl.program_id(0)
    f = pl.program_id(1)

    def init(x_ref, h_ref, y_ref):
        x = x_ref[...]
        h_ref[...] = _rms(x, g_ref[...]).astype(BF16)
        y_ref[...] = x

    col = lax.broadcasted_iota(jnp.int32, (1, FF_TILE), 1)
    fresh = col >= f * FF_TILE - _ff_start(f)

    def accumulate(h_ref, y_ref):
        h = h_ref[...]
        gate = _dot(h, wg_ref[...])
        up = _dot(h, wu_ref[...])
        hid = jnp.where(fresh, gate * jax.nn.sigmoid(gate) * up * 0.5, 0.0).astype(BF16)
        y_ref[...] += _dot(hid, wo_ref[...])

    @pl.when(f == 0)
    def _():
        init(xp_ref, hp_ref, yp_ref)

    @pl.when((f == 0) & (m == 0))
    def _():
        init(xs_ref, hs_ref, ys_ref)

    accumulate(hp_ref, yp_ref)
    if convert_next:
        nwi_bf_ref[...] = nwi_ref[...].astype(BF16)
        nwo_bf_ref[...] = nwo_ref[...].astype(BF16)

    @pl.when(m == 0)
    def _():
        accumulate(hs_ref, ys_ref)


def _ffn(xp, xs, g, w_in, w_out, nxt=None):
    mp, ms = xp.shape[0], xs.shape[0]
    n_m = mp // FFN_ROW_TILE
    elem = pl.Element
    in_specs = [
        pl.BlockSpec((FFN_ROW_TILE, D_MODEL), lambda m, f: (m, 0)),
        pl.BlockSpec((ms, D_MODEL), lambda m, f: (0, 0)),
        pl.BlockSpec((1, D_MODEL), lambda m, f: (0, 0)),
        pl.BlockSpec((elem(D_MODEL), elem(FF_TILE)), lambda m, f: (0, _ff_start(f))),
        pl.BlockSpec((elem(D_MODEL), elem(FF_TILE)), lambda m, f: (0, _ff_start(f, D_FF))),
        pl.BlockSpec((elem(FF_TILE), elem(D_MODEL)), lambda m, f: (_ff_start(f), 0)),
    ]
    out_specs = [
        pl.BlockSpec((FFN_ROW_TILE, D_MODEL), lambda m, f: (m, 0)),
        pl.BlockSpec((ms, D_MODEL), lambda m, f: (0, 0)),
    ]
    out_shape = [jax.ShapeDtypeStruct(xp.shape, F32), jax.ShapeDtypeStruct(xs.shape, F32)]
    args = [xp, xs, g.reshape(1, D_MODEL), w_in, w_in, w_out]
    if nxt is not None:
        nwi, nwo, layer = nxt
        n_in, n_out = D_MODEL // CONVERT_IN_ROWS, D_FF // CONVERT_OUT_ROWS
        assert max(n_in, n_out) <= n_m * N_FF_TILES and D_FF % CONVERT_OUT_ROWS == 0
        slab = lambda m, f, n: jnp.minimum(m * N_FF_TILES + f, n - 1)
        in_specs += [
            pl.BlockSpec((None, CONVERT_IN_ROWS, 2 * D_FF), lambda m, f: (layer, slab(m, f, n_in), 0)),
            pl.BlockSpec((None, CONVERT_OUT_ROWS, D_MODEL), lambda m, f: (layer, slab(m, f, n_out), 0)),
        ]
        out_specs += [
            pl.BlockSpec((CONVERT_IN_ROWS, 2 * D_FF), lambda m, f: (slab(m, f, n_in), 0)),
            pl.BlockSpec((CONVERT_OUT_ROWS, D_MODEL), lambda m, f: (slab(m, f, n_out), 0)),
        ]
        out_shape += [jax.ShapeDtypeStruct((D_MODEL, 2 * D_FF), BF16),
                      jax.ShapeDtypeStruct((D_FF, D_MODEL), BF16)]
        args += [nwi, nwo]
    return pl.pallas_call(
        functools.partial(_ffn_body, convert_next=nxt is not None),
        grid=(n_m, N_FF_TILES),
        in_specs=in_specs,
        out_specs=out_specs,
        out_shape=out_shape,
        scratch_shapes=[pltpu.VMEM((FFN_ROW_TILE, D_MODEL), BF16), pltpu.VMEM((ms, D_MODEL), BF16)],
        compiler_params=_params(("arbitrary", "arbitrary"), 62),
        name="ffn",
    )(*args)


def _pool_body(x_ref, buf_ref, g_ref, pw_ref, sc_ref, y_ref, st_ref, hb_ref, *, tt, start, nt):
    t = pl.program_id(1)
    halo = POOL_BUF + 1

    @pl.when(t == 0)
    def _():
        hb_ref[0:halo, :] = buf_ref[0]

    x = x_ref[0]
    h = _rms(x, g_ref[...])
    hb_ref[halo:halo + tt, :] = h
    pos = start + t * tt + lax.broadcasted_iota(jnp.int32, (tt, 1), 0)
    for gi, w in enumerate(POOL_WINDOWS):
        sl = slice(gi * POOL_GROUP, (gi + 1) * POOL_GROUP)
        win = hb_ref[halo:halo + tt, sl]
        for k in range(1, w):
            win = win + hb_ref[halo - k:halo - k + tt, sl]
        count = jnp.minimum(w, pos + 1).astype(F32)
        pooled = win / count - h[:, sl]
        mixed = _dot(pooled.astype(BF16), pw_ref[gi])
        y_ref[0, :, sl] = x[:, sl] + mixed * sc_ref[:, sl]
    st_ref[0] = hb_ref[tt + 1:tt + halo, :]
    if nt > 1:
        hb_ref[0:halo, :] = hb_ref[tt:tt + halo, :]


def _pool(x, buf, g, pool_w, scale, start):
    b, t_len, _ = x.shape
    tt = min(t_len, ROW_TILE)
    nt = t_len // tt
    halo = POOL_BUF + 1
    buf16 = jnp.pad(buf, ((0, 0), (1, 0), (0, 0)))
    body = functools.partial(_pool_body, tt=tt, start=start, nt=nt)
    return pl.pallas_call(
        body,
        grid=(b, nt),
        in_specs=[
            pl.BlockSpec((1, tt, D_MODEL), lambda i, t: (i, t, 0)),
            pl.BlockSpec((1, halo, D_MODEL), lambda i, t: (i, 0, 0)),
            pl.BlockSpec((1, D_MODEL), lambda i, t: (0, 0)),
            pl.BlockSpec((len(POOL_WINDOWS), POOL_GROUP, POOL_GROUP), lambda i, t: (0, 0, 0)),
            pl.BlockSpec((1, D_MODEL), lambda i, t: (0, 0)),
        ],
        out_specs=[
            pl.BlockSpec((1, tt, D_MODEL), lambda i, t: (i, t, 0)),
            pl.BlockSpec((1, POOL_BUF, D_MODEL), lambda i, t: (i, 0, 0)),
        ],
        out_shape=[jax.ShapeDtypeStruct(x.shape, F32),
                   jax.ShapeDtypeStruct((b, POOL_BUF, D_MODEL), F32)],
        scratch_shapes=[pltpu.VMEM((halo + tt, D_MODEL), F32)],
        compiler_params=_params(("arbitrary", "arbitrary"), 40),
        name="pool",
    )(x, buf16, g.reshape(1, D_MODEL), pool_w.astype(BF16), scale.reshape(1, D_MODEL))


def _chunk_body(x_ref, g_ref, win_ref, vg_ref, ws_ref, bst_ref, wout_ref, *rest, tm, emit_v):
    if emit_v:
        y_ref, vn_ref, mix_ref = rest
    else:
        y_ref, mix_ref = rest
    x = x_ref[...]
    h = _rms(x, g_ref[...]).astype(BF16)
    uv = _dot(h, win_ref[...])
    uv = 0.5 * uv * (1.0 + lax.erf(uv * np.float32(np.sqrt(0.5))))
    u = uv[:, :D_MODEL]
    vn = _rms(uv[:, D_MODEL:], vg_ref[...])
    if emit_v:
        vn_ref[...] = vn
    q_idx = lax.broadcasted_iota(jnp.int32, (CHUNK, CHUNK), 0)
    c_idx = lax.broadcasted_iota(jnp.int32, (CHUNK, CHUNK), 1)
    causal = c_idx <= q_idx
    for gi in range(CHUNK_GROUPS):
        cols = slice(gi * CHUNK_GROUP_W, (gi + 1) * CHUNK_GROUP_W)
        ws = jnp.where(causal, ws_ref[gi], 0.0).astype(BF16)
        bias = bst_ref[:, gi:gi + 1]
        for c in range(tm // CHUNK):
            rows = slice(c * CHUNK, (c + 1) * CHUNK)
            mixed = _dot(ws, vn[rows, cols].astype(BF16)) + bias
            mix_ref[rows, cols] = (u[rows, cols] * mixed).astype(BF16)
    y_ref[...] = x + _dot(mix_ref[...], wout_ref[...])


def _chunk(x, g, w_in, v_gain, w_s, b_s, w_out, emit_v):
    m = x.shape[0]
    tm = CHUNK_ROW_TILE
    body = functools.partial(_chunk_body, tm=tm, emit_v=emit_v)
    row_spec = pl.BlockSpec((tm, D_MODEL), lambda i: (i, 0))
    once = pl.Buffered(1)
    out_specs = [row_spec]
    out_shape = [jax.ShapeDtypeStruct((m, D_MODEL), F32)]
    if emit_v:
        out_specs.append(row_spec)
        out_shape.append(jax.ShapeDtypeStruct((m, D_MODEL), F32))
    return pl.pallas_call(
        body,
        grid=(m // tm,),
        in_specs=[
            row_spec,
            pl.BlockSpec((1, D_MODEL), lambda i: (0, 0)),
            pl.BlockSpec((D_MODEL, 2 * D_MODEL), lambda i: (0, 0), pipeline_mode=once),
            pl.BlockSpec((1, D_MODEL), lambda i: (0, 0)),
            pl.BlockSpec((CHUNK_GROUPS, CHUNK, CHUNK), lambda i: (0, 0, 0)),
            pl.BlockSpec((CHUNK, CHUNK_GROUPS), lambda i: (0, 0)),
            pl.BlockSpec((D_MODEL, D_MODEL), lambda i: (0, 0), pipeline_mode=once),
        ],
        out_specs=out_specs,
        out_shape=out_shape,
        scratch_shapes=[pltpu.VMEM((tm, D_MODEL), BF16)],
        compiler_params=_params(("arbitrary",), 56),
        name="chunk",
    )(x, g.reshape(1, D_MODEL), w_in.astype(BF16), v_gain.reshape(1, D_MODEL), w_s, b_s.T,
      w_out.astype(BF16))


def _tile_perm(dil):
    u = np.arange(ROW_TILE)
    n_k = ROW_TILE // dil
    return (u % n_k) * dil + u // n_k


def _rope_tables(pos):
    half = HEAD_DIM // 2
    freqs = ROPE_THETA ** (-2.0 * jnp.arange(half, dtype=F32) / HEAD_DIM)
    ang = pos.astype(F32)[:, None] * freqs[None, :]
    cos, sin = jnp.cos(ang), jnp.sin(ang)
    return jnp.concatenate([cos, cos], axis=-1), jnp.concatenate([-sin, sin], axis=-1)


def _rope(x, cos, sin_signed):
    return x * cos + pltpu.roll(x, HEAD_DIM // 2, 1) * sin_signed


def _norm_perm_body(x_ref, g_ref, *h_refs):
    h = _rms(x_ref[...], g_ref[...]).astype(BF16)
    u = lax.broadcasted_iota(jnp.int32, (ROW_TILE, ROW_TILE), 0)
    t = lax.broadcasted_iota(jnp.int32, (ROW_TILE, ROW_TILE), 1)
    for h_ref, (_, dil) in zip(h_refs, ATTN_GROUPS):
        if dil == 1:
            h_ref[...] = h
        else:
            n_k = ROW_TILE // dil
            src = (u & (n_k - 1)) * dil + lax.shift_right_logical(u, n_k.bit_length() - 1)
            select = jnp.where(t == src, 1.0, 0.0).astype(BF16)
            h_ref[...] = _dot(select, h).astype(BF16)


def _norm_perm(x, g):
    m = x.shape[0]
    row_spec = pl.BlockSpec((ROW_TILE, D_MODEL), lambda i: (i, 0))
    return pl.pallas_call(
        _norm_perm_body,
        grid=(m // ROW_TILE,),
        in_specs=[row_spec, pl.BlockSpec((1, D_MODEL), lambda i: (0, 0))],
        out_specs=[row_spec] * N_GROUPS,
        out_shape=[jax.ShapeDtypeStruct((m, D_MODEL), BF16)] * N_GROUPS,
        compiler_params=_params(("arbitrary",), 40),
        name="norm_perm",
    )(x, g.reshape(1, D_MODEL))


def _qkv_body(h_ref, w_ref, qn_ref, kn_ref, cos_ref, sin_ref, q_ref, k_ref, v_ref, kvf_ref,
              wbf_ref, ya_ref, yb_ref, slab_ref, *, dil, final_rows, tiles_per_seq, n_m, n_steps):
    p = pl.program_id(0)
    done = jnp.maximum(p - 1, 0)
    role = (done // n_m) // COL_TILES_PER_ROLE
    row_tile = done % n_m

    @pl.when(p == 0)
    def _():
        yb_ref[...] = jnp.zeros_like(yb_ref)

    @pl.when((jnp.minimum(p, n_steps - 1) % n_m == 0) & (p < n_steps))
    def _():
        wbf_ref[...] = w_ref[...].astype(BF16)

    def to_slab(hl, val):
        slab_ref[hl * SLAB_PITCH:hl * SLAB_PITCH + ROW_TILE, :] = val

    def gather_final():
        n_k = ROW_TILE // dil
        for t in range(final_rows):
            t_nat = (ROW_TILE - final_rows) + t
            u = (t_nat % dil) * n_k + t_nat // dil
            kvf_ref[t, 0] = slab_ref[pl.ds(u, HEADS_PER_STEP, stride=SLAB_PITCH), :]

    def emit_final():
        if tiles_per_seq == 1:
            gather_final()
        else:
            pl.when(row_tile % tiles_per_seq == tiles_per_seq - 1)(gather_final)

    def finish(which, y_ref):
        for hl in range(HEADS_PER_STEP):
            yh = y_ref[:, hl * HEAD_DIM:(hl + 1) * HEAD_DIM]
            if which == 0:
                qn = _rope(_rms(yh, qn_ref[...]), cos_ref[...], sin_ref[...])
                q_ref[hl] = (qn * ATTN_SCALE).astype(BF16)
            elif which == 1:
                kn = _rope(_rms(yh, kn_ref[...]), cos_ref[...], sin_ref[...])
                k_ref[hl] = kn.astype(BF16)
                to_slab(hl, kn)
            else:
                v_ref[hl] = yh.astype(BF16)
                to_slab(hl, yh)
        if which > 0:
            emit_final()

    for parity, (cur_ref, prev_ref) in enumerate(((ya_ref, yb_ref), (yb_ref, ya_ref))):
        for which in range(3):
            @pl.when((p % 2 == parity) & (role == which))
            def _(which=which, cur_ref=cur_ref, prev_ref=prev_ref):
                cur_ref[...] = _dot(h_ref[...], wbf_ref[...])
                finish(which, prev_ref)


def _qkv(h, w_qkv, q_gain, k_gain, cos, sin, group, seq):
    window, dil = ATTN_GROUPS[group]
    m = h.shape[0]
    n_m = m // ROW_TILE
    tiles_per_seq = seq // ROW_TILE
    keep = min(window, seq)
    final_rows = min(keep, ROW_TILE)
    every_tile = keep == seq
    halves = COL_TILES_PER_ROLE
    n_steps = 3 * halves * n_m

    def project_tile(p):
        q = jnp.minimum(p, n_steps - 1)
        return q // n_m, q % n_m

    def finish_tile(p):
        e = jnp.maximum(p - 1, 0)
        return e // n_m, e % n_m

    def att_idx(s_own):
        def idx(p):
            c, i = finish_tile(p)
            s, hh = c // halves, c % halves
            before, after = s < s_own, s > s_own
            return (jnp.where(before, 0, jnp.where(after, halves - 1, hh)),
                    jnp.where(before, 0, jnp.where(after, n_m - 1, i)), 0)
        return idx

    def final_idx(p):
        c, i = finish_tile(p)
        s, hh = c // halves, c % halves
        rb = i if every_tile else i // tiles_per_seq
        live = s > 0
        return (jnp.where(live, rb, 0), jnp.where(live, s - 1, 0), jnp.where(live, hh, 0), 0)

    def w_idx(p):
        c, _ = project_tile(p)
        return (0, ((c // halves) * N_GROUPS + group) * halves + c % halves)

    body = functools.partial(_qkv_body, dil=dil, final_rows=final_rows, n_m=n_m, n_steps=n_steps,
                             tiles_per_seq=1 if every_tile else tiles_per_seq)
    att_shape = jax.ShapeDtypeStruct((HEADS, m, HEAD_DIM), BF16)
    att_block = (HEADS_PER_STEP, ROW_TILE, HEAD_DIM)
    table = pl.BlockSpec((ROW_TILE, HEAD_DIM), lambda p: (finish_tile(p)[1] % tiles_per_seq, 0))
    batch = m // seq
    return pl.pallas_call(
        body,
        grid=(n_steps + 1,),
        in_specs=[
            pl.BlockSpec((ROW_TILE, D_MODEL), lambda p: (project_tile(p)[1], 0)),
            pl.BlockSpec((D_MODEL, QKV_COL_TILE), w_idx),
            pl.BlockSpec((1, HEAD_DIM), lambda p: (0, 0)),
            pl.BlockSpec((1, HEAD_DIM), lambda p: (0, 0)),
            table,
            table,
        ],
        out_specs=[
            pl.BlockSpec(att_block, att_idx(0)),
            pl.BlockSpec(att_block, att_idx(1)),
            pl.BlockSpec(att_block, att_idx(2)),
            pl.BlockSpec((final_rows, 1, HEADS_PER_STEP, HEAD_DIM), final_idx),
        ],
        out_shape=[att_shape, att_shape, att_shape,
                   jax.ShapeDtypeStruct((batch * keep, 2, HEADS, HEAD_DIM), F32)],
        scratch_shapes=[pltpu.VMEM((D_MODEL, QKV_COL_TILE), BF16),
                        pltpu.VMEM((ROW_TILE, QKV_COL_TILE), F32),
                        pltpu.VMEM((ROW_TILE, QKV_COL_TILE), F32),
                        pltpu.VMEM((HEADS_PER_STEP * SLAB_PITCH, HEAD_DIM), F32)],
        compiler_params=_params(("arbitrary",), 48),
        name="qkv",
    )(h, w_qkv, q_gain.reshape(1, HEAD_DIM), k_gain.reshape(1, HEAD_DIM), cos, sin)


def _band_mask(n_keys):
    qi = lax.broadcasted_iota(jnp.int32, (ATTN_BLOCK, n_keys), 0)
    kj = lax.broadcasted_iota(jnp.int32, (ATTN_BLOCK, n_keys), 1)
    dist = (n_keys - ATTN_BLOCK) + qi - kj
    return (dist >= 0) & (dist <= ATTN_BLOCK)


def _attn_body(*refs, seq):
    qkv_refs, o_ref, og_ref, lg_ref = refs[:3 * N_GROUPS], refs[3 * N_GROUPS], refs[-2], refs[-1]
    masks = {n: _band_mask(n) for n in (ATTN_BLOCK, 2 * ATTN_BLOCK)}
    for gi, (_, dil) in enumerate(ATTN_GROUPS):
        q_ref, k_ref, v_ref = qkv_refs[3 * gi:3 * gi + 3]
        n_k = ROW_TILE // dil
        n_blocks = seq // dil // ATTN_BLOCK

        def rows_of(ref, r, blk):
            if n_k >= ATTN_BLOCK:
                per_tile = n_k // ATTN_BLOCK
                start = (blk // per_tile) * ROW_TILE + r * n_k + (blk % per_tile) * ATTN_BLOCK
                return ref[0, start:start + ATTN_BLOCK, :]
            parts = [ref[0, tl * ROW_TILE + r * n_k:tl * ROW_TILE + (r + 1) * n_k, :]
                     for tl in range(blk * (ATTN_BLOCK // n_k), (blk + 1) * (ATTN_BLOCK // n_k))]
            return jnp.concatenate(parts, axis=0)

        def keys_of(ref, r, blk):
            if blk == 0:
                return rows_of(ref, r, 0)
            return jnp.concatenate([rows_of(ref, r, blk - 1), rows_of(ref, r, blk)], axis=0)

        todo = [(r, blk) for r in range(dil) for blk in range(n_blocks)]
        for i in range(0, len(todo), ATTN_INTERLEAVE):
            batch = todo[i:i + ATTN_INTERLEAVE]
            ks = [keys_of(k_ref, r, blk) for r, blk in batch]
            ss = [jnp.where(masks[k.shape[0]], _dot_nt(rows_of(q_ref, r, blk), k), NEG)
                  for (r, blk), k in zip(batch, ks)]
            mxs = [jnp.max(s, axis=-1, keepdims=True) for s in ss]
            ps = [jnp.exp(s - mx) for s, mx in zip(ss, mxs)]
            dens = [jnp.sum(p, axis=-1, keepdims=True) for p in ps]
            accs = [_dot(p.astype(BF16), keys_of(v_ref, r, blk)) for p, (r, blk) in zip(ps, batch)]
            for (r, blk), acc, mx, den in zip(batch, accs, mxs, dens):
                if dil == 1:
                    dst = pl.ds(blk * ATTN_BLOCK, ATTN_BLOCK)
                else:
                    dst = pl.ds(blk * ATTN_BLOCK * dil + r, ATTN_BLOCK, stride=dil)
                og_ref[gi, dst, :] = acc / den
                lg_ref[gi, dst, :] = jnp.broadcast_to(mx + jnp.log(den), (ATTN_BLOCK, HEAD_DIM))
    for blk in range(seq // ATTN_BLOCK):
        rows = slice(blk * ATTN_BLOCK, (blk + 1) * ATTN_BLOCK)
        ls = [lg_ref[gi, rows, :] for gi in range(N_GROUPS)]
        mx = functools.reduce(jnp.maximum, ls)
        es = [jnp.exp(l - mx) for l in ls]
        num = sum(e * og_ref[gi, rows, :] for gi, e in enumerate(es))
        o_ref[rows, :] = (num / sum(es)).astype(BF16)


def _attn(qkvs, batch, seq):
    flat = [a for qkv in qkvs for a in qkv]
    blk = pl.BlockSpec((1, seq, HEAD_DIM), lambda b, hd: (hd, b, 0))
    return pl.pallas_call(
        functools.partial(_attn_body, seq=seq),
        grid=(batch, HEADS),
        in_specs=[blk] * len(flat),
        out_specs=pl.BlockSpec((seq, HEAD_DIM), lambda b, hd: (b, hd)),
        out_shape=jax.ShapeDtypeStruct((batch * seq, ATTN_W), BF16),
        scratch_shapes=[pltpu.VMEM((N_GROUPS, seq, HEAD_DIM), F32),
                        pltpu.VMEM((N_GROUPS, seq, HEAD_DIM), F32)],
        compiler_params=_params(("arbitrary", "arbitrary"), 40),
        name="attn",
    )(*flat)


def _out_proj_body(x_ref, o_ref, w_ref, y_ref, wbf_ref):
    @pl.when(pl.program_id(0) == 0)
    def _():
        wbf_ref[...] = w_ref[...].astype(BF16)

    y_ref[...] = x_ref[...] + _dot(o_ref[...], wbf_ref[...])


def _out_proj(x, o, w_out):
    m = x.shape[0]
    tm = min(m, ROW_TILE)
    return pl.pallas_call(
        _out_proj_body,
        grid=(m // tm,),
        in_specs=[pl.BlockSpec((tm, D_MODEL), lambda i: (i, 0)),
                  pl.BlockSpec((tm, ATTN_W), lambda i: (i, 0)),
                  pl.BlockSpec((ATTN_W, D_MODEL), lambda i: (0, 0), pipeline_mode=pl.Buffered(1))],
        out_specs=pl.BlockSpec((tm, D_MODEL), lambda i: (i, 0)),
        out_shape=jax.ShapeDtypeStruct((m, D_MODEL), F32),
        scratch_shapes=[pltpu.VMEM((ATTN_W, D_MODEL), BF16)],
        compiler_params=_params(("arbitrary",), 48),
        name="attn_out_proj",
    )(x, o, w_out)


def _qkv_rows_body(x_ref, g_ref, w_ref, y_ref):
    h = _rms(x_ref[...], g_ref[...]).astype(BF16)
    y_ref[...] = _dot(h, w_ref[...].astype(BF16))


def _qkv_rows(x, g, w_qkv):
    m = x.shape[0]
    n = w_qkv.shape[1]
    return pl.pallas_call(
        _qkv_rows_body,
        grid=(n // QKV_COL_TILE,),
        in_specs=[pl.BlockSpec((m, D_MODEL), lambda c: (0, 0)),
                  pl.BlockSpec((1, D_MODEL), lambda c: (0, 0)),
                  pl.BlockSpec((D_MODEL, QKV_COL_TILE), lambda c: (0, c))],
        out_specs=pl.BlockSpec((m, QKV_COL_TILE), lambda c: (0, c)),
        out_shape=jax.ShapeDtypeStruct((m, n), F32),
        compiler_params=_params(("arbitrary",), 32),
        name="qkv_rows",
    )(x, g.reshape(1, D_MODEL), w_qkv)


def _cache_attn_body(y_ref, c0_ref, c1_ref, c2_ref, qn_ref, kn_ref, cos_ref, sin_ref,
                     o_ref, n0_ref, n1_ref, n2_ref, *, t_len):
    cache_refs = (c0_ref, c1_ref, c2_ref)
    new_refs = (n0_ref, n1_ref, n2_ref)

    def norm_rope(val, gain_ref, t):
        return _rope(_rms(val, gain_ref[...]), cos_ref[t:t + 1, :], sin_ref[t:t + 1, :])

    k_new = [[norm_rope(y_ref[0, t, 1, gi], kn_ref, t) for t in range(t_len)]
             for gi in range(N_GROUPS)]
    v_new = [[y_ref[0, t, 2, gi] for t in range(t_len)] for gi in range(N_GROUPS)]
    for gi in range(N_GROUPS):
        for t in range(t_len):
            new_refs[gi][0, t, 0] = k_new[gi][t]
            new_refs[gi][0, t, 1] = v_new[gi][t]

    cached_pos = lax.broadcasted_iota(jnp.int32, (ATTN_BLOCK, 1, 1), 0)
    for t in range(t_len):
        outs, lses = [], []
        for gi, (_, dil) in enumerate(ATTN_GROUPS):
            c_ref = cache_refs[gi]
            q = norm_rope(y_ref[0, t, 0, gi], qn_ref, t) * ATTN_SCALE
            if dil == 1:
                kc, vc = c_ref[0, :, 0], c_ref[0, :, 1]
                new_rows = range(t + 1)
            else:
                kc, vc = c_ref[0, :, t, 0], c_ref[0, :, t, 1]
                new_rows = [t]
            s_c = jnp.sum(kc * q[None], axis=-1, keepdims=True)
            if dil == 1:
                s_c = jnp.where(cached_pos >= t, s_c, NEG)
            s_n = [jnp.sum(k_new[gi][tt] * q, axis=-1, keepdims=True) for tt in new_rows]
            mx = functools.reduce(jnp.maximum, s_n, jnp.max(s_c, axis=0))
            p_c = jnp.exp(s_c - mx[None])
            p_n = [jnp.exp(sn - mx) for sn in s_n]
            den = jnp.sum(p_c, axis=0) + sum(p_n)
            acc = jnp.sum(p_c * vc, axis=0) + sum(p * v_new[gi][tt] for p, tt in zip(p_n, new_rows))
            outs.append(acc / den)
            lses.append(mx + jnp.log(den))
        lmax = functools.reduce(jnp.maximum, lses)
        es = [jnp.exp(l - lmax) for l in lses]
        o_ref[0, t] = sum(e * o for e, o in zip(es, outs)) / sum(es)


def _cache_attn(y, caches, q_gain, k_gain, cos, sin, batch, t_len):
    y6 = y.reshape(batch, t_len, 3, N_GROUPS, HEADS, HEAD_DIM)
    views, specs = [], []
    for cache, (window, dil) in zip(caches, ATTN_GROUPS):
        assert cache.shape[1] == window and window // dil == ATTN_BLOCK and (dil == 1 or t_len <= dil)
        if dil == 1:
            views.append(cache)
            specs.append(pl.BlockSpec((1, ATTN_BLOCK, 2, HEADS, HEAD_DIM), lambda b: (b, 0, 0, 0, 0)))
        else:
            views.append(cache.reshape(batch, ATTN_BLOCK, dil, 2, HEADS, HEAD_DIM))
            specs.append(pl.BlockSpec((1, ATTN_BLOCK, t_len, 2, HEADS, HEAD_DIM),
                                      lambda b: (b, 0, 0, 0, 0, 0)))
    new_shape = jax.ShapeDtypeStruct((batch, t_len, 2, HEADS, HEAD_DIM), F32)
    new_spec = pl.BlockSpec((1, t_len, 2, HEADS, HEAD_DIM), lambda b: (b, 0, 0, 0, 0))
    vec = pl.BlockSpec((1, HEAD_DIM), lambda b: (0, 0))
    tab = pl.BlockSpec((t_len, HEAD_DIM), lambda b: (0, 0))
    return pl.pallas_call(
        functools.partial(_cache_attn_body, t_len=t_len),
        grid=(batch,),
        in_specs=[pl.BlockSpec((1, t_len, 3, N_GROUPS, HEADS, HEAD_DIM), lambda b: (b, 0, 0, 0, 0, 0))]
        + specs + [vec, vec, tab, tab],
        out_specs=[pl.BlockSpec((1, t_len, HEADS, HEAD_DIM), lambda b: (b, 0, 0, 0))] + [new_spec] * N_GROUPS,
        out_shape=[jax.ShapeDtypeStruct((batch, t_len, HEADS, HEAD_DIM), F32)] + [new_shape] * N_GROUPS,
        compiler_params=_params(("arbitrary",), 56),
        name="cache_attn",
    )(y6, *views, q_gain.reshape(1, HEAD_DIM), k_gain.reshape(1, HEAD_DIM), cos, sin)


def kernel(x_prompt, x_sample, state_pool, cache_kv_g0, cache_kv_g1, cache_kv_g2, norm_ffn1, ffn1_w_in, ffn1_w_out, norm_mix, norm_ffn2, ffn2_w_in, ffn2_w_out, pool_w, pool_scale, chunk_w_in, chunk_v_norm, chunk_w_s, chunk_b_s, chunk_w_out, attn_w_qkv, attn_q_norm, attn_k_norm, attn_w_out):
    caches = (cache_kv_g0, cache_kv_g1, cache_kv_g2)
    batch, seq, _ = x_prompt.shape
    dec_batch, dec_seq, _ = x_sample.shape
    depth = norm_ffn1.shape[0]
    xp = x_prompt.reshape(batch * seq, D_MODEL)
    xs = x_sample.reshape(dec_batch * dec_seq, D_MODEL)
    pool_p, pool_s, chunk_s = [], [], []
    kv_p = [[] for _ in ATTN_GROUPS]
    kv_s = [[] for _ in ATTN_GROUPS]
    half_steps = [(w_in, w_out, i) for i in range(depth)
                  for w_in, w_out in ((ffn1_w_in, ffn1_w_out), (ffn2_w_in, ffn2_w_out))]
    w_bf = (ffn1_w_in[0].astype(BF16), ffn1_w_out[0].astype(BF16))

    def ffn(xp, xs, g, w_bf, step):
        nxt = half_steps[step + 1] if step + 1 < len(half_steps) else None
        out = _ffn(xp, xs, g, *w_bf, nxt=nxt)
        return out[0], out[1], tuple(out[2:])

    for i in range(depth):
        kind, j = i % 3, i // 3
        xp, xs, w_bf = ffn(xp, xs, norm_ffn1[i], w_bf, 2 * i)
        if kind == 0:
            zero_buf = jnp.zeros((batch, POOL_BUF, D_MODEL), F32)
            yp, st_p = _pool(xp.reshape(batch, seq, D_MODEL), zero_buf, norm_mix[i], pool_w[j],
                             pool_scale[j], 0)
            ys, st_s = _pool(xs.reshape(dec_batch, dec_seq, D_MODEL), state_pool[j], norm_mix[i],
                             pool_w[j], pool_scale[j], PAST_LEN)
            xp = yp.reshape(batch * seq, D_MODEL)
            xs = ys.reshape(dec_batch * dec_seq, D_MODEL)
            pool_p.append(st_p)
            pool_s.append(st_s)
        elif kind == 1:
            args = (norm_mix[i], chunk_w_in[j], chunk_v_norm[j], chunk_w_s[j], chunk_b_s[j],
                    chunk_w_out[j])
            (xp,) = _chunk(xp, *args, emit_v=False)
            xs_pad = jnp.pad(xs.reshape(dec_batch, dec_seq, D_MODEL),
                             ((0, 0), (0, CHUNK - dec_seq), (0, 0)))
            ys_pad, v_pad = _chunk(xs_pad.reshape(dec_batch * CHUNK, D_MODEL), *args, emit_v=True)
            xs = ys_pad.reshape(dec_batch, CHUNK, D_MODEL)[:, :dec_seq].reshape(-1, D_MODEL)
            chunk_s.append(v_pad.reshape(dec_batch, CHUNK, D_MODEL)[:, :dec_seq])
        else:
            hs = _norm_perm(xp, norm_mix[i])
            qkvs = []
            for gi, (window, dil) in enumerate(ATTN_GROUPS):
                pos = (np.arange(seq // ROW_TILE)[:, None] * ROW_TILE + _tile_perm(dil)[None, :]).reshape(-1)
                cos_p, sin_p = _rope_tables(jnp.asarray(pos))
                q, k, v, kv_final = _qkv(hs[gi], attn_w_qkv[j], attn_q_norm[j], attn_k_norm[j],
                                         cos_p, sin_p, gi, seq)
                qkvs.append((q, k, v))
                kv_p[gi].append(kv_final.reshape(batch, min(window, seq), 2, HEADS, HEAD_DIM))
            xp = _out_proj(xp, _attn(qkvs, batch, seq), attn_w_out[j])

            cos_s, sin_s = _rope_tables(PAST_LEN + jnp.arange(dec_seq))
            y_s = _qkv_rows(xs, norm_mix[i], attn_w_qkv[j])
            o_s, *new_rows = _cache_attn(y_s, tuple(c[j] for c in caches), attn_q_norm[j],
                                         attn_k_norm[j], cos_s, sin_s, dec_batch, dec_seq)
            for gi in range(N_GROUPS):
                kv_s[gi].append(new_rows[gi])
            xs = _out_proj(xs, o_s.reshape(dec_batch * dec_seq, ATTN_W).astype(BF16), attn_w_out[j])
        xp, xs, w_bf = ffn(xp, xs, norm_ffn2[i], w_bf, 2 * i + 1)
    return (xp.reshape(batch, seq, D_MODEL), xs.reshape(dec_batch, dec_seq, D_MODEL),
            jnp.stack(pool_p), jnp.stack(pool_s), jnp.stack(chunk_s),
            jnp.stack(kv_p[0]), jnp.stack(kv_s[0]), jnp.stack(kv_p[1]), jnp.stack(kv_s[1]),
            jnp.stack(kv_p[2]), jnp.stack(kv_s[2]))
```

```python
import functools

import numpy as np
import jax
import jax.numpy as jnp
from jax import lax
from jax.experimental import pallas as pl
from jax.experimental.pallas import tpu as pltpu

F32 = jnp.float32
BF16 = jnp.bfloat16

D_MODEL = 2048
D_FF = 5504
RMS_EPS = 1e-6
POOL_WINDOWS = (2, 4, 8, 16)
POOL_GROUP = D_MODEL // len(POOL_WINDOWS)
POOL_BUF = max(POOL_WINDOWS) - 1
CHUNK = 128
CHUNK_GROUPS = 8
CHUNK_GROUP_W = D_MODEL // CHUNK_GROUPS
ATTN_GROUPS = ((128, 1), (512, 4), (2048, 16))
N_GROUPS = len(ATTN_GROUPS)
HEADS = 16
HEAD_DIM = 128
ATTN_W = HEADS * HEAD_DIM
ROPE_THETA = 10000.0
ATTN_SCALE = HEAD_DIM ** -0.5
NEG = float(np.finfo(np.float32).min)
PAST_LEN = 16384

LANE = 128
SUBLANE = 8
FF_TILE = 512
N_FF_TILES = -(-D_FF // FF_TILE)
FFN_ROW_TILE = 1024
CONVERT_IN_ROWS = 32
CONVERT_OUT_ROWS = 64
ROW_TILE = 512
CHUNK_ROW_TILE = 256
ATTN_BLOCK = 128
ATTN_INTERLEAVE = 8
QKV_COL_TILE = 1024
HEADS_PER_STEP = QKV_COL_TILE // HEAD_DIM
COL_TILES_PER_ROLE = ATTN_W // QKV_COL_TILE
SLAB_PITCH = ROW_TILE + SUBLANE
MIB = 1024 * 1024


def _params(semantics, vmem_mib):
    return pltpu.CompilerParams(dimension_semantics=semantics, vmem_limit_bytes=vmem_mib * MIB)


def _rms(x, g):
    ms = jnp.mean(x * x, axis=-1, keepdims=True)
    return x * lax.rsqrt(ms + RMS_EPS) * g


def _dot(a, b):
    return jnp.dot(a, b, preferred_element_type=F32)


def _dot_nt(a, b):
    return lax.dot_general(a, b, (((1,), (1,)), ((), ())), preferred_element_type=F32)


def _ff_start(f, base=0):
    return (base // LANE + jnp.minimum(f * (FF_TILE // LANE), (D_FF - FF_TILE) // LANE)) * LANE


def _ffn_body(xp_ref, xs_ref, g_ref, wg_ref, wu_ref, wo_ref, *rest, convert_next):
    if convert_next:
        nwi_ref, nwo_ref, yp_ref, ys_ref, nwi_bf_ref, nwo_bf_ref, h_ref = rest
    else:
        yp_ref, ys_ref, h_ref = rest
    m = pl.program_id(0)
    f = pl.program_id(1)
    tm, ms = xp_ref.shape[0], xs_ref.shape[0]

    def init(x_ref, rows, y_ref):
        x = x_ref[...]
        h_ref[rows, :] = _rms(x, g_ref[...]).astype(BF16)
        y_ref[...] = x

    col = lax.broadcasted_iota(jnp.int32, (1, FF_TILE), 1)
    fresh = col >= f * FF_TILE - _ff_start(f)

    def accumulate(n_rows):
        h = h_ref[:n_rows]
        gate = _dot(h, wg_ref[...])
        up = _dot(h, wu_ref[...])
        hid = jnp.where(fresh, gate * jax.nn.sigmoid(gate) * up * 0.5, 0.0).astype(BF16)
        out = _dot(hid, wo_ref[...])
        yp_ref[...] += out[:tm]
        if n_rows > tm:
            ys_ref[...] += out[tm:]
        if convert_next:
            nwi_bf_ref[...] = nwi_ref[...].astype(BF16)
            nwo_bf_ref[...] = nwo_ref[...].astype(BF16)

    @pl.when(f == 0)
    def _():
        init(xp_ref, slice(0, tm), yp_ref)

    @pl.when((f == 0) & (m == 0))
    def _():
        init(xs_ref, slice(tm, tm + ms), ys_ref)

    @pl.when(m == 0)
    def _():
        accumulate(tm + ms)

    @pl.when(m != 0)
    def _():
        accumulate(tm)


def _ffn(xp, xs, g, w_in, w_out, nxt=None):
    mp, ms = xp.shape[0], xs.shape[0]
    n_m = mp // FFN_ROW_TILE
    elem = pl.Element
    in_specs = [
        pl.BlockSpec((FFN_ROW_TILE, D_MODEL), lambda m, f: (m, 0)),
        pl.BlockSpec((ms, D_MODEL), lambda m, f: (0, 0)),
        pl.BlockSpec((1, D_MODEL), lambda m, f: (0, 0)),
        pl.BlockSpec((elem(D_MODEL), elem(FF_TILE)), lambda m, f: (0, _ff_start(f))),
        pl.BlockSpec((elem(D_MODEL), elem(FF_TILE)), lambda m, f: (0, _ff_start(f, D_FF))),
        pl.BlockSpec((elem(FF_TILE), elem(D_MODEL)), lambda m, f: (_ff_start(f), 0)),
    ]
    out_specs = [
        pl.BlockSpec((FFN_ROW_TILE, D_MODEL), lambda m, f: (m, 0)),
        pl.BlockSpec((ms, D_MODEL), lambda m, f: (0, 0)),
    ]
    out_shape = [jax.ShapeDtypeStruct(xp.shape, F32), jax.ShapeDtypeStruct(xs.shape, F32)]
    args = [xp, xs, g.reshape(1, D_MODEL), w_in, w_in, w_out]
    if nxt is not None:
        nwi, nwo, layer = nxt
        n_in, n_out = D_MODEL // CONVERT_IN_ROWS, D_FF // CONVERT_OUT_ROWS
        assert max(n_in, n_out) <= n_m * N_FF_TILES and D_FF % CONVERT_OUT_ROWS == 0
        slab = lambda m, f, n: jnp.minimum(m * N_FF_TILES + f, n - 1)
        in_specs += [
            pl.BlockSpec((None, CONVERT_IN_ROWS, 2 * D_FF), lambda m, f: (layer, slab(m, f, n_in), 0)),
            pl.BlockSpec((None, CONVERT_OUT_ROWS, D_MODEL), lambda m, f: (layer, slab(m, f, n_out), 0)),
        ]
        out_specs += [
            pl.BlockSpec((CONVERT_IN_ROWS, 2 * D_FF), lambda m, f: (slab(m, f, n_in), 0)),
            pl.BlockSpec((CONVERT_OUT_ROWS, D_MODEL), lambda m, f: (slab(m, f, n_out), 0)),
        ]
        out_shape += [jax.ShapeDtypeStruct((D_MODEL, 2 * D_FF), BF16),
                      jax.ShapeDtypeStruct((D_FF, D_MODEL), BF16)]
        args += [nwi, nwo]
    return pl.pallas_call(
        functools.partial(_ffn_body, convert_next=nxt is not None),
        grid=(n_m, N_FF_TILES),
        in_specs=in_specs,
        out_specs=out_specs,
        out_shape=out_shape,
        scratch_shapes=[pltpu.VMEM((FFN_ROW_TILE + ms, D_MODEL), BF16)],
        compiler_params=_params(("arbitrary", "arbitrary"), 62),
        name="ffn",
    )(*args)


def _pool_body(x_ref, buf_ref, g_ref, pw_ref, sc_ref, y_ref, st_ref, hb_ref, *, tt, start, nt):
    t = pl.program_id(1)
    halo = POOL_BUF + 1

    @pl.when(t == 0)
    def _():
        hb_ref[0:halo, :] = buf_ref[0]

    x = x_ref[0]
    h = _rms(x, g_ref[...])
    hb_ref[halo:halo + tt, :] = h
    pos = start + t * tt + lax.broadcasted_iota(jnp.int32, (tt, 1), 0)
    for gi, w in enumerate(POOL_WINDOWS):
        sl = slice(gi * POOL_GROUP, (gi + 1) * POOL_GROUP)
        win = hb_ref[halo:halo + tt, sl]
        for k in range(1, w):
            win = win + hb_ref[halo - k:halo - k + tt, sl]
        count = jnp.minimum(w, pos + 1).astype(F32)
        pooled = win / count - h[:, sl]
        mixed = _dot(pooled.astype(BF16), pw_ref[gi])
        y_ref[0, :, sl] = x[:, sl] + mixed * sc_ref[:, sl]
    st_ref[0] = hb_ref[tt + 1:tt + halo, :]
    if nt > 1:
        hb_ref[0:halo, :] = hb_ref[tt:tt + halo, :]


def _pool(x, buf, g, pool_w, scale, start):
    b, t_len, _ = x.shape
    tt = min(t_len, ROW_TILE)
    nt = t_len // tt
    halo = POOL_BUF + 1
    buf16 = jnp.pad(buf, ((0, 0), (1, 0), (0, 0)))
    body = functools.partial(_pool_body, tt=tt, start=start, nt=nt)
    return pl.pallas_call(
        body,
        grid=(b, nt),
        in_specs=[
            pl.BlockSpec((1, tt, D_MODEL), lambda i, t: (i, t, 0)),
            pl.BlockSpec((1, halo, D_MODEL), lambda i, t: (i, 0, 0)),
            pl.BlockSpec((1, D_MODEL), lambda i, t: (0, 0)),
            pl.BlockSpec((len(POOL_WINDOWS), POOL_GROUP, POOL_GROUP), lambda i, t: (0, 0, 0)),
            pl.BlockSpec((1, D_MODEL), lambda i, t: (0, 0)),
        ],
        out_specs=[
            pl.BlockSpec((1, tt, D_MODEL), lambda i, t: (i, t, 0)),
            pl.BlockSpec((1, POOL_BUF, D_MODEL), lambda i, t: (i, 0, 0)),
        ],
        out_shape=[jax.ShapeDtypeStruct(x.shape, F32),
                   jax.ShapeDtypeStruct((b, POOL_BUF, D_MODEL), F32)],
        scratch_shapes=[pltpu.VMEM((halo + tt, D_MODEL), F32)],
        compiler_params=_params(("arbitrary", "arbitrary"), 40),
        name="pool",
    )(x, buf16, g.reshape(1, D_MODEL), pool_w.astype(BF16), scale.reshape(1, D_MODEL))


def _chunk_body(x_ref, g_ref, win_ref, vg_ref, ws_ref, bst_ref, wout_ref, *rest, tm, emit_v):
    if emit_v:
        y_ref, vn_ref, mix_ref = rest
    else:
        y_ref, mix_ref = rest
    x = x_ref[...]
    h = _rms(x, g_ref[...]).astype(BF16)
    uv = _dot(h, win_ref[...])
    uv = 0.5 * uv * (1.0 + lax.erf(uv * np.float32(np.sqrt(0.5))))
    u = uv[:, :D_MODEL]
    vn = _rms(uv[:, D_MODEL:], vg_ref[...])
    if emit_v:
        vn_ref[...] = vn
    q_idx = lax.broadcasted_iota(jnp.int32, (CHUNK, CHUNK), 0)
    c_idx = lax.broadcasted_iota(jnp.int32, (CHUNK, CHUNK), 1)
    causal = c_idx <= q_idx
    for gi in range(CHUNK_GROUPS):
        cols = slice(gi * CHUNK_GROUP_W, (gi + 1) * CHUNK_GROUP_W)
        ws = jnp.where(causal, ws_ref[gi], 0.0).astype(BF16)
        bias = bst_ref[:, gi:gi + 1]
        for c in range(tm // CHUNK):
            rows = slice(c * CHUNK, (c + 1) * CHUNK)
            mixed = _dot(ws, vn[rows, cols].astype(BF16)) + bias
            mix_ref[rows, cols] = (u[rows, cols] * mixed).astype(BF16)
    y_ref[...] = x + _dot(mix_ref[...], wout_ref[...])


def _chunk(x, g, w_in, v_gain, w_s, b_s, w_out, emit_v):
    m = x.shape[0]
    tm = CHUNK_ROW_TILE
    body = functools.partial(_chunk_body, tm=tm, emit_v=emit_v)
    row_spec = pl.BlockSpec((tm, D_MODEL), lambda i: (i, 0))
    once = pl.Buffered(1)
    out_specs = [row_spec]
    out_shape = [jax.ShapeDtypeStruct((m, D_MODEL), F32)]
    if emit_v:
        out_specs.append(row_spec)
        out_shape.append(jax.ShapeDtypeStruct((m, D_MODEL), F32))
    return pl.pallas_call(
        body,
        grid=(m // tm,),
        in_specs=[
            row_spec,
            pl.BlockSpec((1, D_MODEL), lambda i: (0, 0)),
            pl.BlockSpec((D_MODEL, 2 * D_MODEL), lambda i: (0, 0), pipeline_mode=once),
            pl.BlockSpec((1, D_MODEL), lambda i: (0, 0)),
            pl.BlockSpec((CHUNK_GROUPS, CHUNK, CHUNK), lambda i: (0, 0, 0)),
            pl.BlockSpec((CHUNK, CHUNK_GROUPS), lambda i: (0, 0)),
            pl.BlockSpec((D_MODEL, D_MODEL), lambda i: (0, 0), pipeline_mode=once),
        ],
        out_specs=out_specs,
        out_shape=out_shape,
        scratch_shapes=[pltpu.VMEM((tm, D_MODEL), BF16)],
        compiler_params=_params(("arbitrary",), 56),
        name="chunk",
    )(x, g.reshape(1, D_MODEL), w_in.astype(BF16), v_gain.reshape(1, D_MODEL), w_s, b_s.T,
      w_out.astype(BF16))


def _tile_perm(dil):
    u = np.arange(ROW_TILE)
    n_k = ROW_TILE // dil
    return (u % n_k) * dil + u // n_k


def _rope_tables(pos):
    half = HEAD_DIM // 2
    freqs = ROPE_THETA ** (-2.0 * jnp.arange(half, dtype=F32) / HEAD_DIM)
    ang = pos.astype(F32)[:, None] * freqs[None, :]
    cos, sin = jnp.cos(ang), jnp.sin(ang)
    return jnp.concatenate([cos, cos], axis=-1), jnp.concatenate([-sin, sin], axis=-1)


def _rope(x, cos, sin_signed):
    return x * cos + pltpu.roll(x, HEAD_DIM // 2, 1) * sin_signed


def _norm_perm_body(x_ref, g_ref, *h_refs):
    h = _rms(x_ref[...], g_ref[...]).astype(BF16)
    u = lax.broadcasted_iota(jnp.int32, (ROW_TILE, ROW_TILE), 0)
    t = lax.broadcasted_iota(jnp.int32, (ROW_TILE, ROW_TILE), 1)
    for h_ref, (_, dil) in zip(h_refs, ATTN_GROUPS):
        if dil == 1:
            h_ref[...] = h
        else:
            n_k = ROW_TILE // dil
            src = (u & (n_k - 1)) * dil + lax.shift_right_logical(u, n_k.bit_length() - 1)
            select = jnp.where(t == src, 1.0, 0.0).astype(BF16)
            h_ref[...] = _dot(select, h).astype(BF16)


def _norm_perm(x, g):
    m = x.shape[0]
    row_spec = pl.BlockSpec((ROW_TILE, D_MODEL), lambda i: (i, 0))
    return pl.pallas_call(
        _norm_perm_body,
        grid=(m // ROW_TILE,),
        in_specs=[row_spec, pl.BlockSpec((1, D_MODEL), lambda i: (0, 0))],
        out_specs=[row_spec] * N_GROUPS,
        out_shape=[jax.ShapeDtypeStruct((m, D_MODEL), BF16)] * N_GROUPS,
        compiler_params=_params(("arbitrary",), 40),
        name="norm_perm",
    )(x, g.reshape(1, D_MODEL))


def _qkv_body(h_ref, w_ref, qn_ref, kn_ref, cos_ref, sin_ref, q_ref, k_ref, v_ref, kvf_ref,
              wbf_ref, ya_ref, yb_ref, slab_ref, *, dil, final_rows, tiles_per_seq, table_tiles, n_m,
              n_steps):
    p = pl.program_id(0)
    done = jnp.maximum(p - 1, 0)
    role = (done // n_m) // COL_TILES_PER_ROLE
    row_tile = done % n_m

    @pl.when(p == 0)
    def _():
        yb_ref[...] = jnp.zeros_like(yb_ref)

    @pl.when((jnp.minimum(p, n_steps - 1) % n_m == 0) & (p < n_steps))
    def _():
        wbf_ref[...] = w_ref[...].astype(BF16)

    def to_slab(hl, val):
        slab_ref[hl * SLAB_PITCH:hl * SLAB_PITCH + ROW_TILE, :] = val

    def gather_final():
        n_k = ROW_TILE // dil
        for t in range(final_rows):
            t_nat = (ROW_TILE - final_rows) + t
            u = (t_nat % dil) * n_k + t_nat // dil
            kvf_ref[t, 0] = slab_ref[pl.ds(u, HEADS_PER_STEP, stride=SLAB_PITCH), :]

    def emit_final():
        if tiles_per_seq == 1:
            gather_final()
        else:
            pl.when(row_tile % tiles_per_seq == tiles_per_seq - 1)(gather_final)

    def finish(which, y_ref):
        rows = pl.ds(pl.multiple_of((row_tile % table_tiles) * ROW_TILE, ROW_TILE), ROW_TILE)
        for hl in range(HEADS_PER_STEP):
            yh = y_ref[:, hl * HEAD_DIM:(hl + 1) * HEAD_DIM]
            if which == 0:
                qn = _rope(_rms(yh, qn_ref[...]), cos_ref[rows, :], sin_ref[rows, :])
                q_ref[hl] = (qn * ATTN_SCALE).astype(BF16)
            elif which == 1:
                kn = _rope(_rms(yh, kn_ref[...]), cos_ref[rows, :], sin_ref[rows, :])
                k_ref[hl] = kn.astype(BF16)
                to_slab(hl, kn)
            else:
                v_ref[hl] = yh.astype(BF16)
                to_slab(hl, yh)
        if which > 0:
            emit_final()

    for parity, (cur_ref, prev_ref) in enumerate(((ya_ref, yb_ref), (yb_ref, ya_ref))):
        for which in range(3):
            @pl.when((p % 2 == parity) & (role == which))
            def _(which=which, cur_ref=cur_ref, prev_ref=prev_ref):
                cur_ref[...] = _dot(h_ref[...], wbf_ref[...])
                finish(which, prev_ref)


def _qkv(h, w_qkv, q_gain, k_gain, cos, sin, group, seq):
    window, dil = ATTN_GROUPS[group]
    m = h.shape[0]
    n_m = m // ROW_TILE
    tiles_per_seq = seq // ROW_TILE
    keep = min(window, seq)
    final_rows = min(keep, ROW_TILE)
    every_tile = keep == seq
    halves = COL_TILES_PER_ROLE
    n_steps = 3 * halves * n_m

    def project_tile(p):
        q = jnp.minimum(p, n_steps - 1)
        return q // n_m, q % n_m

    def finish_tile(p):
        e = jnp.maximum(p - 1, 0)
        return e // n_m, e % n_m

    def att_idx(s_own):
        def idx(p):
            c, i = finish_tile(p)
            s, hh = c // halves, c % halves
            before, after = s < s_own, s > s_own
            return (jnp.where(before, 0, jnp.where(after, halves - 1, hh)),
                    jnp.where(before, 0, jnp.where(after, n_m - 1, i)), 0)
        return idx

    def final_idx(p):
        c, i = finish_tile(p)
        s, hh = c // halves, c % halves
        rb = i if every_tile else i // tiles_per_seq
        live = s > 0
        return (jnp.where(live, rb, 0), jnp.where(live, s - 1, 0), jnp.where(live, hh, 0), 0)

    def w_idx(p):
        c, _ = project_tile(p)
        return (0, ((c // halves) * N_GROUPS + group) * halves + c % halves)

    body = functools.partial(_qkv_body, dil=dil, final_rows=final_rows, n_m=n_m, n_steps=n_steps,
                             tiles_per_seq=1 if every_tile else tiles_per_seq,
                             table_tiles=tiles_per_seq)
    att_shape = jax.ShapeDtypeStruct((HEADS, m, HEAD_DIM), BF16)
    att_block = (HEADS_PER_STEP, ROW_TILE, HEAD_DIM)
    table = pl.BlockSpec((seq, HEAD_DIM), lambda p: (0, 0))
    batch = m // seq
    return pl.pallas_call(
        body,
        grid=(n_steps + 1,),
        in_specs=[
            pl.BlockSpec((ROW_TILE, D_MODEL), lambda p: (project_tile(p)[1], 0)),
            pl.BlockSpec((D_MODEL, QKV_COL_TILE), w_idx),
            pl.BlockSpec((1, HEAD_DIM), lambda p: (0, 0)),
            pl.BlockSpec((1, HEAD_DIM), lambda p: (0, 0)),
            table,
            table,
        ],
        out_specs=[
            pl.BlockSpec(att_block, att_idx(0)),
            pl.BlockSpec(att_block, att_idx(1)),
            pl.BlockSpec(att_block, att_idx(2)),
            pl.BlockSpec((final_rows, 1, HEADS_PER_STEP, HEAD_DIM), final_idx),
        ],
        out_shape=[att_shape, att_shape, att_shape,
                   jax.ShapeDtypeStruct((batch * keep, 2, HEADS, HEAD_DIM), F32)],
        scratch_shapes=[pltpu.VMEM((D_MODEL, QKV_COL_TILE), BF16),
                        pltpu.VMEM((ROW_TILE, QKV_COL_TILE), F32),
                        pltpu.VMEM((ROW_TILE, QKV_COL_TILE), F32),
                        pltpu.VMEM((HEADS_PER_STEP * SLAB_PITCH, HEAD_DIM), F32)],
        compiler_params=_params(("arbitrary",), 48),
        name="qkv",
    )(h, w_qkv, q_gain.reshape(1, HEAD_DIM), k_gain.reshape(1, HEAD_DIM), cos, sin)


def _band_mask(n_keys):
    qi = lax.broadcasted_iota(jnp.int32, (ATTN_BLOCK, n_keys), 0)
    kj = lax.broadcasted_iota(jnp.int32, (ATTN_BLOCK, n_keys), 1)
    dist = (n_keys - ATTN_BLOCK) + qi - kj
    return (dist >= 0) & (dist <= ATTN_BLOCK)


def _attn_body(*refs, seq):
    qkv_refs, o_ref, og_ref, lg_ref = refs[:3 * N_GROUPS], refs[3 * N_GROUPS], refs[-2], refs[-1]
    masks = {n: _band_mask(n) for n in (ATTN_BLOCK, 2 * ATTN_BLOCK)}
    for gi, (_, dil) in enumerate(ATTN_GROUPS):
        q_ref, k_ref, v_ref = qkv_refs[3 * gi:3 * gi + 3]
        n_k = ROW_TILE // dil
        n_blocks = seq // dil // ATTN_BLOCK

        def rows_of(ref, r, blk):
            if n_k >= ATTN_BLOCK:
                per_tile = n_k // ATTN_BLOCK
                start = (blk // per_tile) * ROW_TILE + r * n_k + (blk % per_tile) * ATTN_BLOCK
                return ref[0, start:start + ATTN_BLOCK, :]
            parts = [ref[0, tl * ROW_TILE + r * n_k:tl * ROW_TILE + (r + 1) * n_k, :]
                     for tl in range(blk * (ATTN_BLOCK // n_k), (blk + 1) * (ATTN_BLOCK // n_k))]
            return jnp.concatenate(parts, axis=0)

        def keys_of(ref, r, blk):
            if blk == 0:
                return rows_of(ref, r, 0)
            return jnp.concatenate([rows_of(ref, r, blk - 1), rows_of(ref, r, blk)], axis=0)

        todo = [(r, blk) for r in range(dil) for blk in range(n_blocks)]
        for i in range(0, len(todo), ATTN_INTERLEAVE):
            batch = todo[i:i + ATTN_INTERLEAVE]
            ks = [keys_of(k_ref, r, blk) for r, blk in batch]
            ss = [jnp.where(masks[k.shape[0]], _dot_nt(rows_of(q_ref, r, blk), k), NEG)
                  for (r, blk), k in zip(batch, ks)]
            mxs = [jnp.max(s, axis=-1, keepdims=True) for s in ss]
            ps = [jnp.exp(s - mx) for s, mx in zip(ss, mxs)]
            dens = [jnp.sum(p, axis=-1, keepdims=True) for p in ps]
            accs = [_dot(p.astype(BF16), keys_of(v_ref, r, blk)) for p, (r, blk) in zip(ps, batch)]
            for (r, blk), acc, mx, den in zip(batch, accs, mxs, dens):
                if dil == 1:
                    dst = pl.ds(blk * ATTN_BLOCK, ATTN_BLOCK)
                else:
                    dst = pl.ds(blk * ATTN_BLOCK * dil + r, ATTN_BLOCK, stride=dil)
                og_ref[gi, dst, :] = acc / den
                lg_ref[gi, dst, :] = jnp.broadcast_to(mx + jnp.log(den), (ATTN_BLOCK, HEAD_DIM))
    for blk in range(seq // ATTN_BLOCK):
        rows = slice(blk * ATTN_BLOCK, (blk + 1) * ATTN_BLOCK)
        ls = [lg_ref[gi, rows, :] for gi in range(N_GROUPS)]
        mx = functools.reduce(jnp.maximum, ls)
        es = [jnp.exp(l - mx) for l in ls]
        num = sum(e * og_ref[gi, rows, :] for gi, e in enumerate(es))
        o_ref[rows, :] = (num / sum(es)).astype(BF16)


def _attn(qkvs, batch, seq):
    flat = [a for qkv in qkvs for a in qkv]
    blk = pl.BlockSpec((1, seq, HEAD_DIM), lambda b, hd: (hd, b, 0))
    return pl.pallas_call(
        functools.partial(_attn_body, seq=seq),
        grid=(batch, HEADS),
        in_specs=[blk] * len(flat),
        out_specs=pl.BlockSpec((seq, HEAD_DIM), lambda b, hd: (b, hd)),
        out_shape=jax.ShapeDtypeStruct((batch * seq, ATTN_W), BF16),
        scratch_shapes=[pltpu.VMEM((N_GROUPS, seq, HEAD_DIM), F32)] * 2,
        compiler_params=_params(("arbitrary", "arbitrary"), 40),
        name="attn",
    )(*flat)


def _out_proj_body(x_ref, o_ref, w_ref, y_ref, wbf_ref):
    @pl.when(pl.program_id(0) == 0)
    def _():
        wbf_ref[...] = w_ref[...].astype(BF16)

    y_ref[...] = x_ref[...] + _dot(o_ref[...], wbf_ref[...])


def _out_proj(x, o, w_out):
    m = x.shape[0]
    tm = min(m, ROW_TILE)
    return pl.pallas_call(
        _out_proj_body,
        grid=(m // tm,),
        in_specs=[pl.BlockSpec((tm, D_MODEL), lambda i: (i, 0)),
                  pl.BlockSpec((tm, ATTN_W), lambda i: (i, 0)),
                  pl.BlockSpec((ATTN_W, D_MODEL), lambda i: (0, 0), pipeline_mode=pl.Buffered(1))],
        out_specs=pl.BlockSpec((tm, D_MODEL), lambda i: (i, 0)),
        out_shape=jax.ShapeDtypeStruct((m, D_MODEL), F32),
        scratch_shapes=[pltpu.VMEM((ATTN_W, D_MODEL), BF16)],
        compiler_params=_params(("arbitrary",), 48),
        name="attn_out_proj",
    )(x, o, w_out)


def _qkv_rows_body(x_ref, g_ref, w_ref, y_ref):
    h = _rms(x_ref[...], g_ref[...]).astype(BF16)
    y_ref[...] = _dot(h, w_ref[...].astype(BF16))


def _qkv_rows(x, g, w_qkv):
    m = x.shape[0]
    n = w_qkv.shape[1]
    return pl.pallas_call(
        _qkv_rows_body,
        grid=(n // QKV_COL_TILE,),
        in_specs=[pl.BlockSpec((m, D_MODEL), lambda c: (0, 0)),
                  pl.BlockSpec((1, D_MODEL), lambda c: (0, 0)),
                  pl.BlockSpec((D_MODEL, QKV_COL_TILE), lambda c: (0, c))],
        out_specs=pl.BlockSpec((m, QKV_COL_TILE), lambda c: (0, c)),
        out_shape=jax.ShapeDtypeStruct((m, n), F32),
        compiler_params=_params(("arbitrary",), 32),
        name="qkv_rows",
    )(x, g.reshape(1, D_MODEL), w_qkv)


def _cache_attn_body(y_ref, c0_ref, c1_ref, c2_ref, qn_ref, kn_ref, cos_ref, sin_ref,
                     o_ref, n0_ref, n1_ref, n2_ref, *, t_len):
    cache_refs = (c0_ref, c1_ref, c2_ref)
    new_refs = (n0_ref, n1_ref, n2_ref)

    def norm_rope(val, gain_ref, t):
        return _rope(_rms(val, gain_ref[...]), cos_ref[t:t + 1, :], sin_ref[t:t + 1, :])

    k_new = [[norm_rope(y_ref[0, t, 1, gi], kn_ref, t) for t in range(t_len)]
             for gi in range(N_GROUPS)]
    v_new = [[y_ref[0, t, 2, gi] for t in range(t_len)] for gi in range(N_GROUPS)]
    for gi in range(N_GROUPS):
        for t in range(t_len):
            new_refs[gi][0, t, 0] = k_new[gi][t]
            new_refs[gi][0, t, 1] = v_new[gi][t]

    cached_pos = lax.broadcasted_iota(jnp.int32, (ATTN_BLOCK, 1, 1), 0)
    for t in range(t_len):
        outs, lses = [], []
        for gi, (_, dil) in enumerate(ATTN_GROUPS):
            c_ref = cache_refs[gi]
            q = norm_rope(y_ref[0, t, 0, gi], qn_ref, t) * ATTN_SCALE
            if dil == 1:
                kc, vc = c_ref[0, :, 0], c_ref[0, :, 1]
                new_rows = range(t + 1)
            else:
                kc, vc = c_ref[0, :, t, 0], c_ref[0, :, t, 1]
                new_rows = [t]
            s_c = jnp.sum(kc * q[None], axis=-1, keepdims=True)
            if dil == 1:
                s_c = jnp.where(cached_pos >= t, s_c, NEG)
            s_n = [jnp.sum(k_new[gi][tt] * q, axis=-1, keepdims=True) for tt in new_rows]
            mx = functools.reduce(jnp.maximum, s_n, jnp.max(s_c, axis=0))
            p_c = jnp.exp(s_c - mx[None])
            p_n = [jnp.exp(sn - mx) for sn in s_n]
            den = jnp.sum(p_c, axis=0) + sum(p_n)
            acc = jnp.sum(p_c * vc, axis=0) + sum(p * v_new[gi][tt] for p, tt in zip(p_n, new_rows))
            outs.append(acc / den)
            lses.append(mx + jnp.log(den))
        lmax = functools.reduce(jnp.maximum, lses)
        es = [jnp.exp(l - lmax) for l in lses]
        o_ref[0, t] = sum(e * o for e, o in zip(es, outs)) / sum(es)


def _cache_attn(y, caches, q_gain, k_gain, cos, sin, batch, t_len):
    y6 = y.reshape(batch, t_len, 3, N_GROUPS, HEADS, HEAD_DIM)
    views, specs = [], []
    for cache, (window, dil) in zip(caches, ATTN_GROUPS):
        assert cache.shape[1] == window and window // dil == ATTN_BLOCK and (dil == 1 or t_len <= dil)
        if dil == 1:
            views.append(cache)
            specs.append(pl.BlockSpec((1, ATTN_BLOCK, 2, HEADS, HEAD_DIM), lambda b: (b, 0, 0, 0, 0)))
        else:
            views.append(cache.reshape(batch, ATTN_BLOCK, dil, 2, HEADS, HEAD_DIM))
            specs.append(pl.BlockSpec((1, ATTN_BLOCK, t_len, 2, HEADS, HEAD_DIM),
                                      lambda b: (b, 0, 0, 0, 0, 0)))
    new_shape = jax.ShapeDtypeStruct((batch, t_len, 2, HEADS, HEAD_DIM), F32)
    new_spec = pl.BlockSpec((1, t_len, 2, HEADS, HEAD_DIM), lambda b: (b, 0, 0, 0, 0))
    vec = pl.BlockSpec((1, HEAD_DIM), lambda b: (0, 0))
    tab = pl.BlockSpec((t_len, HEAD_DIM), lambda b: (0, 0))
    return pl.pallas_call(
        functools.partial(_cache_attn_body, t_len=t_len),
        grid=(batch,),
        in_specs=[pl.BlockSpec((1, t_len, 3, N_GROUPS, HEADS, HEAD_DIM), lambda b: (b, 0, 0, 0, 0, 0))]
        + specs + [vec, vec, tab, tab],
        out_specs=[pl.BlockSpec((1, t_len, HEADS, HEAD_DIM), lambda b: (b, 0, 0, 0))] + [new_spec] * N_GROUPS,
        out_shape=[jax.ShapeDtypeStruct((batch, t_len, HEADS, HEAD_DIM), F32)] + [new_shape] * N_GROUPS,
        compiler_params=_params(("arbitrary",), 56),
        name="cache_attn",
    )(y6, *views, q_gain.reshape(1, HEAD_DIM), k_gain.reshape(1, HEAD_DIM), cos, sin)


def kernel(x_prompt, x_sample, state_pool, cache_kv_g0, cache_kv_g1, cache_kv_g2, norm_ffn1, ffn1_w_in, ffn1_w_out, norm_mix, norm_ffn2, ffn2_w_in, ffn2_w_out, pool_w, pool_scale, chunk_w_in, chunk_v_norm, chunk_w_s, chunk_b_s, chunk_w_out, attn_w_qkv, attn_q_norm, attn_k_norm, attn_w_out):
    caches = (cache_kv_g0, cache_kv_g1, cache_kv_g2)
    batch, seq, _ = x_prompt.shape
    dec_batch, dec_seq, _ = x_sample.shape
    depth = norm_ffn1.shape[0]
    xp = x_prompt.reshape(batch * seq, D_MODEL)
    xs = x_sample.reshape(dec_batch * dec_seq, D_MODEL)
    pool_p, pool_s, chunk_s = [], [], []
    kv_p = [[] for _ in ATTN_GROUPS]
    kv_s = [[] for _ in ATTN_GROUPS]
    half_steps = [(w_in, w_out, i) for i in range(depth)
                  for w_in, w_out in ((ffn1_w_in, ffn1_w_out), (ffn2_w_in, ffn2_w_out))]
    w_bf = (ffn1_w_in[0].astype(BF16), ffn1_w_out[0].astype(BF16))

    def ffn(xp, xs, g, w_bf, step):
        nxt = half_steps[step + 1] if step + 1 < len(half_steps) else None
        out = _ffn(xp, xs, g, *w_bf, nxt=nxt)
        return out[0], out[1], tuple(out[2:])

    for i in range(depth):
        kind, j = i % 3, i // 3
        xp, xs, w_bf = ffn(xp, xs, norm_ffn1[i], w_bf, 2 * i)
        if kind == 0:
            zero_buf = jnp.zeros((batch, POOL_BUF, D_MODEL), F32)
            yp, st_p = _pool(xp.reshape(batch, seq, D_MODEL), zero_buf, norm_mix[i], pool_w[j],
                             pool_scale[j], 0)
            ys, st_s = _pool(xs.reshape(dec_batch, dec_seq, D_MODEL), state_pool[j], norm_mix[i],
                             pool_w[j], pool_scale[j], PAST_LEN)
            xp = yp.reshape(batch * seq, D_MODEL)
            xs = ys.reshape(dec_batch * dec_seq, D_MODEL)
            pool_p.append(st_p)
            pool_s.append(st_s)
        elif kind == 1:
            args = (norm_mix[i], chunk_w_in[j], chunk_v_norm[j], chunk_w_s[j], chunk_b_s[j],
                    chunk_w_out[j])
            (xp,) = _chunk(xp, *args, emit_v=False)
            xs_pad = jnp.pad(xs.reshape(dec_batch, dec_seq, D_MODEL),
                             ((0, 0), (0, CHUNK - dec_seq), (0, 0)))
            ys_pad, v_pad = _chunk(xs_pad.reshape(dec_batch * CHUNK, D_MODEL), *args, emit_v=True)
            xs = ys_pad.reshape(dec_batch, CHUNK, D_MODEL)[:, :dec_seq].reshape(-1, D_MODEL)
            chunk_s.append(v_pad.reshape(dec_batch, CHUNK, D_MODEL)[:, :dec_seq])
        else:
            hs = _norm_perm(xp, norm_mix[i])
            qkvs = []
            for gi, (window, dil) in enumerate(ATTN_GROUPS):
                pos = (np.arange(seq // ROW_TILE)[:, None] * ROW_TILE + _tile_perm(dil)[None, :]).reshape(-1)
                cos_p, sin_p = _rope_tables(jnp.asarray(pos))
                q, k, v, kv_final = _qkv(hs[gi], attn_w_qkv[j], attn_q_norm[j], attn_k_norm[j],
                                         cos_p, sin_p, gi, seq)
                qkvs.append((q, k, v))
                kv_p[gi].append(kv_final.reshape(batch, min(window, seq), 2, HEADS, HEAD_DIM))
            xp = _out_proj(xp, _attn(qkvs, batch, seq), attn_w_out[j])

            cos_s, sin_s = _rope_tables(PAST_LEN + jnp.arange(dec_seq))
            y_s = _qkv_rows(xs, norm_mix[i], attn_w_qkv[j])
            o_s, *new_rows = _cache_attn(y_s, tuple(c[j] for c in caches), attn_q_norm[j],
                                         attn_k_norm[j], cos_s, sin_s, dec_batch, dec_seq)
            for gi in range(N_GROUPS):
                kv_s[gi].append(new_rows[gi])
            xs = _out_proj(xs, o_s.reshape(dec_batch * dec_seq, ATTN_W).astype(BF16), attn_w_out[j])
        xp, xs, w_bf = ffn(xp, xs, norm_ffn2[i], w_bf, 2 * i + 1)
    return (xp.reshape(batch, seq, D_MODEL), xs.reshape(dec_batch, dec_seq, D_MODEL),
            jnp.stack(pool_p), jnp.stack(pool_s), jnp.stack(chunk_s),
            jnp.stack(kv_p[0]), jnp.stack(kv_s[0]), jnp.stack(kv_p[1]), jnp.stack(kv_s[1]),
            jnp.stack(kv_p[2]), jnp.stack(kv_s[2]))
```

```python
import functools

import numpy as np
import jax
import jax.numpy as jnp
from jax import lax
from jax.experimental import pallas as pl
from jax.experimental.pallas import tpu as pltpu

F32 = jnp.float32
BF16 = jnp.bfloat16

D_MODEL = 2048
D_FF = 5504
RMS_EPS = 1e-6
POOL_WINDOWS = (2, 4, 8, 16)
POOL_GROUP = D_MODEL // len(POOL_WINDOWS)
POOL_BUF = max(POOL_WINDOWS) - 1
CHUNK = 128
CHUNK_GROUPS = 8
CHUNK_GROUP_W = D_MODEL // CHUNK_GROUPS
ATTN_GROUPS = ((128, 1), (512, 4), (2048, 16))
N_GROUPS = len(ATTN_GROUPS)
HEADS = 16
HEAD_DIM = 128
ATTN_W = HEADS * HEAD_DIM
ROPE_THETA = 10000.0
ATTN_SCALE = HEAD_DIM ** -0.5
NEG = float(np.finfo(np.float32).min)
PAST_LEN = 16384

LANE = 128
SUBLANE = 8
FF_TILE = 512
N_FF_TILES = -(-D_FF // FF_TILE)
FFN_ROW_TILE = 1024
CONVERT_IN_ROWS = 32
CONVERT_OUT_ROWS = 64
ROW_TILE = 512
CHUNK_ROW_TILE = 256
POOL_HALO = SUBLANE * len(POOL_WINDOWS)
ATTN_BLOCK = 128
ATTN_INTERLEAVE = 8
QKV_COL_TILE = 1024
HEADS_PER_STEP = QKV_COL_TILE // HEAD_DIM
COL_TILES_PER_ROLE = ATTN_W // QKV_COL_TILE
SLAB_PITCH = ROW_TILE + SUBLANE
MIB = 1024 * 1024


def _params(semantics, vmem_mib):
    return pltpu.CompilerParams(dimension_semantics=semantics, vmem_limit_bytes=vmem_mib * MIB)


def _rms(x, g):
    ms = jnp.mean(x * x, axis=-1, keepdims=True)
    return x * lax.rsqrt(ms + RMS_EPS) * g


def _dot(a, b):
    return jnp.dot(a, b, preferred_element_type=F32)


def _dot_nt(a, b):
    return lax.dot_general(a, b, (((1,), (1,)), ((), ())), preferred_element_type=F32)


def _ff_start(f, base=0):
    return (base // LANE + jnp.minimum(f * (FF_TILE // LANE), (D_FF - FF_TILE) // LANE)) * LANE


def _ffn_body(xp_ref, xs_ref, g_ref, wg_ref, wu_ref, wo_ref, *rest, convert_next):
    if convert_next:
        nwi_ref, nwo_ref, yp_ref, ys_ref, nwi_bf_ref, nwo_bf_ref, h_ref = rest
    else:
        yp_ref, ys_ref, h_ref = rest
    m = pl.program_id(0)
    f = pl.program_id(1)
    tm, ms = xp_ref.shape[0], xs_ref.shape[0]

    def init(x_ref, rows, y_ref):
        x = x_ref[...]
        h_ref[rows, :] = _rms(x, g_ref[...]).astype(BF16)
        y_ref[...] = x

    col = lax.broadcasted_iota(jnp.int32, (1, FF_TILE), 1)
    fresh = col >= f * FF_TILE - _ff_start(f)

    def accumulate(n_rows):
        h = h_ref[:n_rows]
        gate = _dot(h, wg_ref[...])
        up = _dot(h, wu_ref[...])
        hid = jnp.where(fresh, gate * jax.nn.sigmoid(gate) * up * 0.5, 0.0).astype(BF16)
        out = _dot(hid, wo_ref[...])
        yp_ref[...] += out[:tm]
        if n_rows > tm:
            ys_ref[...] += out[tm:]
        if convert_next:
            nwi_bf_ref[...] = nwi_ref[...].astype(BF16)
            nwo_bf_ref[...] = nwo_ref[...].astype(BF16)

    @pl.when(f == 0)
    def _():
        init(xp_ref, slice(0, tm), yp_ref)

    @pl.when((f == 0) & (m == 0))
    def _():
        init(xs_ref, slice(tm, tm + ms), ys_ref)

    @pl.when(m == 0)
    def _():
        accumulate(tm + ms)

    @pl.when(m != 0)
    def _():
        accumulate(tm)


def _ffn(xp, xs, g, w_in, w_out, nxt=None):
    mp, ms = xp.shape[0], xs.shape[0]
    n_m = mp // FFN_ROW_TILE
    elem = pl.Element
    in_specs = [
        pl.BlockSpec((FFN_ROW_TILE, D_MODEL), lambda m, f: (m, 0)),
        pl.BlockSpec((ms, D_MODEL), lambda m, f: (0, 0)),
        pl.BlockSpec((1, D_MODEL), lambda m, f: (0, 0)),
        pl.BlockSpec((elem(D_MODEL), elem(FF_TILE)), lambda m, f: (0, _ff_start(f))),
        pl.BlockSpec((elem(D_MODEL), elem(FF_TILE)), lambda m, f: (0, _ff_start(f, D_FF))),
        pl.BlockSpec((elem(FF_TILE), elem(D_MODEL)), lambda m, f: (_ff_start(f), 0)),
    ]
    out_specs = [
        pl.BlockSpec((FFN_ROW_TILE, D_MODEL), lambda m, f: (m, 0)),
        pl.BlockSpec((ms, D_MODEL), lambda m, f: (0, 0)),
    ]
    out_shape = [jax.ShapeDtypeStruct(xp.shape, F32), jax.ShapeDtypeStruct(xs.shape, F32)]
    args = [xp, xs, g.reshape(1, D_MODEL), w_in, w_in, w_out]
    if nxt is not None:
        nwi, nwo, layer = nxt
        n_in, n_out = D_MODEL // CONVERT_IN_ROWS, D_FF // CONVERT_OUT_ROWS
        assert max(n_in, n_out) <= n_m * N_FF_TILES and D_FF % CONVERT_OUT_ROWS == 0
        slab = lambda m, f, n: jnp.minimum(m * N_FF_TILES + f, n - 1)
        in_specs += [
            pl.BlockSpec((None, CONVERT_IN_ROWS, 2 * D_FF), lambda m, f: (layer, slab(m, f, n_in), 0)),
            pl.BlockSpec((None, CONVERT_OUT_ROWS, D_MODEL), lambda m, f: (layer, slab(m, f, n_out), 0)),
        ]
        out_specs += [
            pl.BlockSpec((CONVERT_IN_ROWS, 2 * D_FF), lambda m, f: (slab(m, f, n_in), 0)),
            pl.BlockSpec((CONVERT_OUT_ROWS, D_MODEL), lambda m, f: (slab(m, f, n_out), 0)),
        ]
        out_shape += [jax.ShapeDtypeStruct((D_MODEL, 2 * D_FF), BF16),
                      jax.ShapeDtypeStruct((D_FF, D_MODEL), BF16)]
        args += [nwi, nwo]
    return pl.pallas_call(
        functools.partial(_ffn_body, convert_next=nxt is not None),
        grid=(n_m, N_FF_TILES),
        in_specs=in_specs,
        out_specs=out_specs,
        out_shape=out_shape,
        scratch_shapes=[pltpu.VMEM((FFN_ROW_TILE + ms, D_MODEL), BF16)],
        compiler_params=_params(("arbitrary", "arbitrary"), 62),
        name="ffn",
    )(*args)


def _pool_body(x_ref, buf_ref, g_ref, pw_ref, sc_ref, y_ref, st_ref, hb_ref, sa_ref, sb_ref, *,
               tt, start, nt):
    t = pl.program_id(1)
    halo = POOL_HALO
    n = halo + tt

    @pl.when(t == 0)
    def _():
        hb_ref[0:halo, :] = buf_ref[0]

    x = x_ref[0]
    h = _rms(x, g_ref[...])
    hb_ref[halo:n, :] = h
    refs = (hb_ref, sa_ref, sb_ref, sa_ref, sb_ref)
    for k, w in enumerate(POOL_WINDOWS):
        src, dst = refs[k], refs[k + 1]
        lo, cols = SUBLANE * (k + 1), slice(k * POOL_GROUP, D_MODEL)
        dst[lo:n, cols] = src[lo:n, cols] + src[lo - w // 2:n - w // 2, cols]
    pos = start + t * tt + lax.broadcasted_iota(jnp.int32, (tt, 1), 0)
    for gi, w in enumerate(POOL_WINDOWS):
        sl = slice(gi * POOL_GROUP, (gi + 1) * POOL_GROUP)
        win = refs[gi + 1][halo:n, sl]
        count = jnp.minimum(w, pos + 1).astype(F32)
        pooled = win / count - h[:, sl]
        mixed = _dot(pooled.astype(BF16), pw_ref[gi])
        y_ref[0, :, sl] = x[:, sl] + mixed * sc_ref[:, sl]
    st_ref[0] = hb_ref[n - POOL_BUF:n, :]
    if nt > 1:
        hb_ref[0:halo, :] = hb_ref[tt:n, :]


def _pool(x, buf, g, pool_w, scale, start):
    b, t_len, _ = x.shape
    tt = min(t_len, ROW_TILE)
    nt = t_len // tt
    halo = POOL_HALO
    assert POOL_WINDOWS == tuple(2 ** (k + 1) for k in range(len(POOL_WINDOWS))) and halo > POOL_BUF
    buf16 = jnp.pad(buf, ((0, 0), (halo - POOL_BUF, 0), (0, 0)))
    body = functools.partial(_pool_body, tt=tt, start=start, nt=nt)
    return pl.pallas_call(
        body,
        grid=(b, nt),
        in_specs=[
            pl.BlockSpec((1, tt, D_MODEL), lambda i, t: (i, t, 0)),
            pl.BlockSpec((1, halo, D_MODEL), lambda i, t: (i, 0, 0)),
            pl.BlockSpec((1, D_MODEL), lambda i, t: (0, 0)),
            pl.BlockSpec((len(POOL_WINDOWS), POOL_GROUP, POOL_GROUP), lambda i, t: (0, 0, 0)),
            pl.BlockSpec((1, D_MODEL), lambda i, t: (0, 0)),
        ],
        out_specs=[
            pl.BlockSpec((1, tt, D_MODEL), lambda i, t: (i, t, 0)),
            pl.BlockSpec((1, POOL_BUF, D_MODEL), lambda i, t: (i, 0, 0)),
        ],
        out_shape=[jax.ShapeDtypeStruct(x.shape, F32),
                   jax.ShapeDtypeStruct((b, POOL_BUF, D_MODEL), F32)],
        scratch_shapes=[pltpu.VMEM((halo + tt, D_MODEL), F32)] * 3,
        compiler_params=_params(("arbitrary", "arbitrary"), 40),
        name="pool",
    )(x, buf16, g.reshape(1, D_MODEL), pool_w.astype(BF16), scale.reshape(1, D_MODEL))


def _chunk_body(x_ref, g_ref, win_ref, vg_ref, ws_ref, bst_ref, wout_ref, *rest, tm, emit_v):
    if emit_v:
        y_ref, vn_ref, mix_ref = rest
    else:
        y_ref, mix_ref = rest
    x = x_ref[...]
    h = _rms(x, g_ref[...]).astype(BF16)
    uv = _dot(h, win_ref[...])
    uv = 0.5 * uv * (1.0 + lax.erf(uv * np.float32(np.sqrt(0.5))))
    u = uv[:, :D_MODEL]
    vn = _rms(uv[:, D_MODEL:], vg_ref[...])
    if emit_v:
        vn_ref[...] = vn
    q_idx = lax.broadcasted_iota(jnp.int32, (CHUNK, CHUNK), 0)
    c_idx = lax.broadcasted_iota(jnp.int32, (CHUNK, CHUNK), 1)
    causal = c_idx <= q_idx
    for gi in range(CHUNK_GROUPS):
        cols = slice(gi * CHUNK_GROUP_W, (gi + 1) * CHUNK_GROUP_W)
        ws = jnp.where(causal, ws_ref[gi], 0.0).astype(BF16)
        bias = bst_ref[:, gi:gi + 1]
        for c in range(tm // CHUNK):
            rows = slice(c * CHUNK, (c + 1) * CHUNK)
            mixed = _dot(ws, vn[rows, cols].astype(BF16)) + bias
            mix_ref[rows, cols] = (u[rows, cols] * mixed).astype(BF16)
    y_ref[...] = x + _dot(mix_ref[...], wout_ref[...])


def _chunk(x, g, w_in, v_gain, w_s, b_s, w_out, emit_v):
    m = x.shape[0]
    tm = CHUNK_ROW_TILE
    body = functools.partial(_chunk_body, tm=tm, emit_v=emit_v)
    row_spec = pl.BlockSpec((tm, D_MODEL), lambda i: (i, 0))
    once = pl.Buffered(1)
    out_specs = [row_spec]
    out_shape = [jax.ShapeDtypeStruct((m, D_MODEL), F32)]
    if emit_v:
        out_specs.append(row_spec)
        out_shape.append(jax.ShapeDtypeStruct((m, D_MODEL), F32))
    return pl.pallas_call(
        body,
        grid=(m // tm,),
        in_specs=[
            row_spec,
            pl.BlockSpec((1, D_MODEL), lambda i: (0, 0)),
            pl.BlockSpec((D_MODEL, 2 * D_MODEL), lambda i: (0, 0), pipeline_mode=once),
            pl.BlockSpec((1, D_MODEL), lambda i: (0, 0)),
            pl.BlockSpec((CHUNK_GROUPS, CHUNK, CHUNK), lambda i: (0, 0, 0)),
            pl.BlockSpec((CHUNK, CHUNK_GROUPS), lambda i: (0, 0)),
            pl.BlockSpec((D_MODEL, D_MODEL), lambda i: (0, 0), pipeline_mode=once),
        ],
        out_specs=out_specs,
        out_shape=out_shape,
        scratch_shapes=[pltpu.VMEM((tm, D_MODEL), BF16)],
        compiler_params=_params(("arbitrary",), 56),
        name="chunk",
    )(x, g.reshape(1, D_MODEL), w_in.astype(BF16), v_gain.reshape(1, D_MODEL), w_s, b_s.T,
      w_out.astype(BF16))


def _tile_perm(dil):
    u = np.arange(ROW_TILE)
    n_k = ROW_TILE // dil
    return (u % n_k) * dil + u // n_k


def _rope_tables(pos):
    half = HEAD_DIM // 2
    freqs = ROPE_THETA ** (-2.0 * jnp.arange(half, dtype=F32) / HEAD_DIM)
    ang = pos.astype(F32)[:, None] * freqs[None, :]
    cos, sin = jnp.cos(ang), jnp.sin(ang)
    return jnp.concatenate([cos, cos], axis=-1), jnp.concatenate([-sin, sin], axis=-1)


def _rope(x, cos, sin_signed):
    return x * cos + pltpu.roll(x, HEAD_DIM // 2, 1) * sin_signed


def _norm_perm_body(x_ref, g_ref, *h_refs):
    h = _rms(x_ref[...], g_ref[...]).astype(BF16)
    u = lax.broadcasted_iota(jnp.int32, (ROW_TILE, ROW_TILE), 0)
    t = lax.broadcasted_iota(jnp.int32, (ROW_TILE, ROW_TILE), 1)
    for h_ref, (_, dil) in zip(h_refs, ATTN_GROUPS):
        if dil == 1:
            h_ref[...] = h
        else:
            n_k = ROW_TILE // dil
            src = (u & (n_k - 1)) * dil + lax.shift_right_logical(u, n_k.bit_length() - 1)
            select = jnp.where(t == src, 1.0, 0.0).astype(BF16)
            h_ref[...] = _dot(select, h).astype(BF16)


def _norm_perm(x, g):
    m = x.shape[0]
    row_spec = pl.BlockSpec((ROW_TILE, D_MODEL), lambda i: (i, 0))
    return pl.pallas_call(
        _norm_perm_body,
        grid=(m // ROW_TILE,),
        in_specs=[row_spec, pl.BlockSpec((1, D_MODEL), lambda i: (0, 0))],
        out_specs=[row_spec] * N_GROUPS,
        out_shape=[jax.ShapeDtypeStruct((m, D_MODEL), BF16)] * N_GROUPS,
        compiler_params=_params(("arbitrary",), 40),
        name="norm_perm",
    )(x, g.reshape(1, D_MODEL))


def _qkv_body(h_ref, w_ref, qn_ref, kn_ref, cos_ref, sin_ref, q_ref, k_ref, v_ref, kvf_ref,
              wbf_ref, ya_ref, yb_ref, slab_ref, *, dil, final_rows, tiles_per_seq, table_tiles, n_m,
              n_steps):
    p = pl.program_id(0)
    done = jnp.maximum(p - 1, 0)
    role = (done // n_m) // COL_TILES_PER_ROLE
    row_tile = done % n_m

    @pl.when(p == 0)
    def _():
        yb_ref[...] = jnp.zeros_like(yb_ref)

    @pl.when((jnp.minimum(p, n_steps - 1) % n_m == 0) & (p < n_steps))
    def _():
        wbf_ref[...] = w_ref[...].astype(BF16)

    def to_slab(hl, val):
        slab_ref[hl * SLAB_PITCH:hl * SLAB_PITCH + ROW_TILE, :] = val

    def gather_final():
        n_k = ROW_TILE // dil
        for t in range(final_rows):
            t_nat = (ROW_TILE - final_rows) + t
            u = (t_nat % dil) * n_k + t_nat // dil
            kvf_ref[t, 0] = slab_ref[pl.ds(u, HEADS_PER_STEP, stride=SLAB_PITCH), :]

    def emit_final():
        if tiles_per_seq == 1:
            gather_final()
        else:
            pl.when(row_tile % tiles_per_seq == tiles_per_seq - 1)(gather_final)

    def finish(which, y_ref):
        rows = pl.ds(pl.multiple_of((row_tile % table_tiles) * ROW_TILE, ROW_TILE), ROW_TILE)
        for hl in range(HEADS_PER_STEP):
            yh = y_ref[:, hl * HEAD_DIM:(hl + 1) * HEAD_DIM]
            if which == 0:
                qn = _rope(_rms(yh, qn_ref[...]), cos_ref[rows, :], sin_ref[rows, :])
                q_ref[hl] = (qn * ATTN_SCALE).astype(BF16)
            elif which == 1:
                kn = _rope(_rms(yh, kn_ref[...]), cos_ref[rows, :], sin_ref[rows, :])
                k_ref[hl] = kn.astype(BF16)
                to_slab(hl, kn)
            else:
                v_ref[hl] = yh.astype(BF16)
                to_slab(hl, yh)
        if which > 0:
            emit_final()

    for parity, (cur_ref, prev_ref) in enumerate(((ya_ref, yb_ref), (yb_ref, ya_ref))):
        for which in range(3):
            @pl.when((p % 2 == parity) & (role == which))
            def _(which=which, cur_ref=cur_ref, prev_ref=prev_ref):
                cur_ref[...] = _dot(h_ref[...], wbf_ref[...])
                finish(which, prev_ref)


def _qkv(h, w_qkv, q_gain, k_gain, cos, sin, group, seq):
    window, dil = ATTN_GROUPS[group]
    m = h.shape[0]
    n_m = m // ROW_TILE
    tiles_per_seq = seq // ROW_TILE
    keep = min(window, seq)
    final_rows = min(keep, ROW_TILE)
    every_tile = keep == seq
    halves = COL_TILES_PER_ROLE
    n_steps = 3 * halves * n_m

    def project_tile(p):
        q = jnp.minimum(p, n_steps - 1)
        return q // n_m, q % n_m

    def finish_tile(p):
        e = jnp.maximum(p - 1, 0)
        return e // n_m, e % n_m

    def att_idx(s_own):
        def idx(p):
            c, i = finish_tile(p)
            s, hh = c // halves, c % halves
            before, after = s < s_own, s > s_own
            return (jnp.where(before, 0, jnp.where(after, halves - 1, hh)),
                    jnp.where(before, 0, jnp.where(after, n_m - 1, i)), 0)
        return idx

    def final_idx(p):
        c, i = finish_tile(p)
        s, hh = c // halves, c % halves
        rb = i if every_tile else i // tiles_per_seq
        live = s > 0
        return (jnp.where(live, rb, 0), jnp.where(live, s - 1, 0), jnp.where(live, hh, 0), 0)

    def w_idx(p):
        c, _ = project_tile(p)
        return (0, ((c // halves) * N_GROUPS + group) * halves + c % halves)

    body = functools.partial(_qkv_body, dil=dil, final_rows=final_rows, n_m=n_m, n_steps=n_steps,
                             tiles_per_seq=1 if every_tile else tiles_per_seq,
                             table_tiles=tiles_per_seq)
    att_shape = jax.ShapeDtypeStruct((HEADS, m, HEAD_DIM), BF16)
    att_block = (HEADS_PER_STEP, ROW_TILE, HEAD_DIM)
    table = pl.BlockSpec((seq, HEAD_DIM), lambda p: (0, 0))
    batch = m // seq
    return pl.pallas_call(
        body,
        grid=(n_steps + 1,),
        in_specs=[
            pl.BlockSpec((ROW_TILE, D_MODEL), lambda p: (project_tile(p)[1], 0)),
            pl.BlockSpec((D_MODEL, QKV_COL_TILE), w_idx),
            pl.BlockSpec((1, HEAD_DIM), lambda p: (0, 0)),
            pl.BlockSpec((1, HEAD_DIM), lambda p: (0, 0)),
            table,
            table,
        ],
        out_specs=[
            pl.BlockSpec(att_block, att_idx(0)),
            pl.BlockSpec(att_block, att_idx(1)),
            pl.BlockSpec(att_block, att_idx(2)),
            pl.BlockSpec((final_rows, 1, HEADS_PER_STEP, HEAD_DIM), final_idx),
        ],
        out_shape=[att_shape, att_shape, att_shape,
                   jax.ShapeDtypeStruct((batch * keep, 2, HEADS, HEAD_DIM), F32)],
        scratch_shapes=[pltpu.VMEM((D_MODEL, QKV_COL_TILE), BF16),
                        pltpu.VMEM((ROW_TILE, QKV_COL_TILE), F32),
                        pltpu.VMEM((ROW_TILE, QKV_COL_TILE), F32),
                        pltpu.VMEM((HEADS_PER_STEP * SLAB_PITCH, HEAD_DIM), F32)],
        compiler_params=_params(("arbitrary",), 48),
        name="qkv",
    )(h, w_qkv, q_gain.reshape(1, HEAD_DIM), k_gain.reshape(1, HEAD_DIM), cos, sin)


def _band_mask(n_keys):
    qi = lax.broadcasted_iota(jnp.int32, (ATTN_BLOCK, n_keys), 0)
    kj = lax.broadcasted_iota(jnp.int32, (ATTN_BLOCK, n_keys), 1)
    dist = (n_keys - ATTN_BLOCK) + qi - kj
    return (dist >= 0) & (dist <= ATTN_BLOCK)


def _attn_body(*refs, seq):
    qkv_refs, o_ref, og_ref, lg_ref = refs[:3 * N_GROUPS], refs[3 * N_GROUPS], refs[-2], refs[-1]
    masks = {n: _band_mask(n) for n in (ATTN_BLOCK, 2 * ATTN_BLOCK)}
    for gi, (_, dil) in enumerate(ATTN_GROUPS):
        q_ref, k_ref, v_ref = qkv_refs[3 * gi:3 * gi + 3]
        n_k = ROW_TILE // dil
        n_blocks = seq // dil // ATTN_BLOCK

        def rows_of(ref, r, blk):
            if n_k >= ATTN_BLOCK:
                per_tile = n_k // ATTN_BLOCK
                start = (blk // per_tile) * ROW_TILE + r * n_k + (blk % per_tile) * ATTN_BLOCK
                return ref[0, start:start + ATTN_BLOCK, :]
            parts = [ref[0, tl * ROW_TILE + r * n_k:tl * ROW_TILE + (r + 1) * n_k, :]
                     for tl in range(blk * (ATTN_BLOCK // n_k), (blk + 1) * (ATTN_BLOCK // n_k))]
            return jnp.concatenate(parts, axis=0)

        def keys_of(ref, r, blk):
            if blk == 0:
                return rows_of(ref, r, 0)
            return jnp.concatenate([rows_of(ref, r, blk - 1), rows_of(ref, r, blk)], axis=0)

        todo = [(r, blk) for r in range(dil) for blk in range(n_blocks)]
        for i in range(0, len(todo), ATTN_INTERLEAVE):
            batch = todo[i:i + ATTN_INTERLEAVE]
            ks = [keys_of(k_ref, r, blk) for r, blk in batch]
            ss = [jnp.where(masks[k.shape[0]], _dot_nt(rows_of(q_ref, r, blk), k), NEG)
                  for (r, blk), k in zip(batch, ks)]
            mxs = [jnp.max(s, axis=-1, keepdims=True) for s in ss]
            ps = [jnp.exp(s - mx) for s, mx in zip(ss, mxs)]
            dens = [jnp.sum(p, axis=-1, keepdims=True) for p in ps]
            accs = [_dot(p.astype(BF16), keys_of(v_ref, r, blk)) for p, (r, blk) in zip(ps, batch)]
            for (r, blk), acc, mx, den in zip(batch, accs, mxs, dens):
                if dil == 1:
                    dst = pl.ds(blk * ATTN_BLOCK, ATTN_BLOCK)
                else:
                    dst = pl.ds(blk * ATTN_BLOCK * dil + r, ATTN_BLOCK, stride=dil)
                og_ref[gi, dst, :] = acc / den
                lg_ref[gi, dst, :] = jnp.broadcast_to(mx + jnp.log(den), (ATTN_BLOCK, HEAD_DIM))
    for blk in range(seq // ATTN_BLOCK):
        rows = slice(blk * ATTN_BLOCK, (blk + 1) * ATTN_BLOCK)
        ls = [lg_ref[gi, rows, :] for gi in range(N_GROUPS)]
        mx = functools.reduce(jnp.maximum, ls)
        es = [jnp.exp(l - mx) for l in ls]
        num = sum(e * og_ref[gi, rows, :] for gi, e in enumerate(es))
        o_ref[rows, :] = (num / sum(es)).astype(BF16)


def _attn(qkvs, batch, seq):
    flat = [a for qkv in qkvs for a in qkv]
    blk = pl.BlockSpec((1, seq, HEAD_DIM), lambda b, hd: (hd, b, 0))
    return pl.pallas_call(
        functools.partial(_attn_body, seq=seq),
        grid=(batch, HEADS),
        in_specs=[blk] * len(flat),
        out_specs=pl.BlockSpec((seq, HEAD_DIM), lambda b, hd: (b, hd)),
        out_shape=jax.ShapeDtypeStruct((batch * seq, ATTN_W), BF16),
        scratch_shapes=[pltpu.VMEM((N_GROUPS, seq, HEAD_DIM), F32)] * 2,
        compiler_params=_params(("arbitrary", "arbitrary"), 40),
        name="attn",
    )(*flat)


def _out_proj_body(x_ref, o_ref, w_ref, y_ref, wbf_ref):
    @pl.when(pl.program_id(0) == 0)
    def _():
        wbf_ref[...] = w_ref[...].astype(BF16)

    y_ref[...] = x_ref[...] + _dot(o_ref[...], wbf_ref[...])


def _out_proj(x, o, w_out):
    m = x.shape[0]
    tm = min(m, ROW_TILE)
    return pl.pallas_call(
        _out_proj_body,
        grid=(m // tm,),
        in_specs=[pl.BlockSpec((tm, D_MODEL), lambda i: (i, 0)),
                  pl.BlockSpec((tm, ATTN_W), lambda i: (i, 0)),
                  pl.BlockSpec((ATTN_W, D_MODEL), lambda i: (0, 0), pipeline_mode=pl.Buffered(1))],
        out_specs=pl.BlockSpec((tm, D_MODEL), lambda i: (i, 0)),
        out_shape=jax.ShapeDtypeStruct((m, D_MODEL), F32),
        scratch_shapes=[pltpu.VMEM((ATTN_W, D_MODEL), BF16)],
        compiler_params=_params(("arbitrary",), 48),
        name="attn_out_proj",
    )(x, o, w_out)


def _qkv_rows_body(x_ref, g_ref, w_ref, y_ref):
    h = _rms(x_ref[...], g_ref[...]).astype(BF16)
    y_ref[...] = _dot(h, w_ref[...].astype(BF16))


def _qkv_rows(x, g, w_qkv):
    m = x.shape[0]
    n = w_qkv.shape[1]
    return pl.pallas_call(
        _qkv_rows_body,
        grid=(n // QKV_COL_TILE,),
        in_specs=[pl.BlockSpec((m, D_MODEL), lambda c: (0, 0)),
                  pl.BlockSpec((1, D_MODEL), lambda c: (0, 0)),
                  pl.BlockSpec((D_MODEL, QKV_COL_TILE), lambda c: (0, c))],
        out_specs=pl.BlockSpec((m, QKV_COL_TILE), lambda c: (0, c)),
        out_shape=jax.ShapeDtypeStruct((m, n), F32),
        compiler_params=_params(("arbitrary",), 32),
        name="qkv_rows",
    )(x, g.reshape(1, D_MODEL), w_qkv)


def _cache_attn_body(y_ref, c0_ref, c1_ref, c2_ref, qn_ref, kn_ref, cos_ref, sin_ref,
                     o_ref, n0_ref, n1_ref, n2_ref, *, t_len):
    cache_refs = (c0_ref, c1_ref, c2_ref)
    new_refs = (n0_ref, n1_ref, n2_ref)

    def norm_rope(val, gain_ref, t):
        return _rope(_rms(val, gain_ref[...]), cos_ref[t:t + 1, :], sin_ref[t:t + 1, :])

    k_new = [[norm_rope(y_ref[0, t, 1, gi], kn_ref, t) for t in range(t_len)]
             for gi in range(N_GROUPS)]
    v_new = [[y_ref[0, t, 2, gi] for t in range(t_len)] for gi in range(N_GROUPS)]
    for gi in range(N_GROUPS):
        for t in range(t_len):
            new_refs[gi][0, t, 0] = k_new[gi][t]
            new_refs[gi][0, t, 1] = v_new[gi][t]

    cached_pos = lax.broadcasted_iota(jnp.int32, (ATTN_BLOCK, 1, 1), 0)
    for t in range(t_len):
        outs, lses = [], []
        for gi, (_, dil) in enumerate(ATTN_GROUPS):
            c_ref = cache_refs[gi]
            q = norm_rope(y_ref[0, t, 0, gi], qn_ref, t) * ATTN_SCALE
            if dil == 1:
                kc, vc = c_ref[0, :, 0], c_ref[0, :, 1]
                new_rows = range(t + 1)
            else:
                kc, vc = c_ref[0, :, t, 0], c_ref[0, :, t, 1]
                new_rows = [t]
            s_c = jnp.sum(kc * q[None], axis=-1, keepdims=True)
            if dil == 1:
                s_c = jnp.where(cached_pos >= t, s_c, NEG)
            s_n = [jnp.sum(k_new[gi][tt] * q, axis=-1, keepdims=True) for tt in new_rows]
            mx = functools.reduce(jnp.maximum, s_n, jnp.max(s_c, axis=0))
            p_c = jnp.exp(s_c - mx[None])
            p_n = [jnp.exp(sn - mx) for sn in s_n]
            den = jnp.sum(p_c, axis=0) + sum(p_n)
            acc = jnp.sum(p_c * vc, axis=0) + sum(p * v_new[gi][tt] for p, tt in zip(p_n, new_rows))
            outs.append(acc / den)
            lses.append(mx + jnp.log(den))
        lmax = functools.reduce(jnp.maximum, lses)
        es = [jnp.exp(l - lmax) for l in lses]
        o_ref[0, t] = sum(e * o for e, o in zip(es, outs)) / sum(es)


def _cache_attn(y, caches, q_gain, k_gain, cos, sin, batch, t_len):
    y6 = y.reshape(batch, t_len, 3, N_GROUPS, HEADS, HEAD_DIM)
    views, specs = [], []
    for cache, (window, dil) in zip(caches, ATTN_GROUPS):
        assert cache.shape[1] == window and window // dil == ATTN_BLOCK and (dil == 1 or t_len <= dil)
        if dil == 1:
            views.append(cache)
            specs.append(pl.BlockSpec((1, ATTN_BLOCK, 2, HEADS, HEAD_DIM), lambda b: (b, 0, 0, 0, 0)))
        else:
            views.append(cache.reshape(batch, ATTN_BLOCK, dil, 2, HEADS, HEAD_DIM))
            specs.append(pl.BlockSpec((1, ATTN_BLOCK, t_len, 2, HEADS, HEAD_DIM),
                                      lambda b: (b, 0, 0, 0, 0, 0)))
    new_shape = jax.ShapeDtypeStruct((batch, t_len, 2, HEADS, HEAD_DIM), F32)
    new_spec = pl.BlockSpec((1, t_len, 2, HEADS, HEAD_DIM), lambda b: (b, 0, 0, 0, 0))
    vec = pl.BlockSpec((1, HEAD_DIM), lambda b: (0, 0))
    tab = pl.BlockSpec((t_len, HEAD_DIM), lambda b: (0, 0))
    return pl.pallas_call(
        functools.partial(_cache_attn_body, t_len=t_len),
        grid=(batch,),
        in_specs=[pl.BlockSpec((1, t_len, 3, N_GROUPS, HEADS, HEAD_DIM), lambda b: (b, 0, 0, 0, 0, 0))]
        + specs + [vec, vec, tab, tab],
        out_specs=[pl.BlockSpec((1, t_len, HEADS, HEAD_DIM), lambda b: (b, 0, 0, 0))] + [new_spec] * N_GROUPS,
        out_shape=[jax.ShapeDtypeStruct((batch, t_len, HEADS, HEAD_DIM), F32)] + [new_shape] * N_GROUPS,
        compiler_params=_params(("arbitrary",), 56),
        name="cache_attn",
    )(y6, *views, q_gain.reshape(1, HEAD_DIM), k_gain.reshape(1, HEAD_DIM), cos, sin)


def kernel(x_prompt, x_sample, state_pool, cache_kv_g0, cache_kv_g1, cache_kv_g2, norm_ffn1, ffn1_w_in, ffn1_w_out, norm_mix, norm_ffn2, ffn2_w_in, ffn2_w_out, pool_w, pool_scale, chunk_w_in, chunk_v_norm, chunk_w_s, chunk_b_s, chunk_w_out, attn_w_qkv, attn_q_norm, attn_k_norm, attn_w_out):
    caches = (cache_kv_g0, cache_kv_g1, cache_kv_g2)
    batch, seq, _ = x_prompt.shape
    dec_batch, dec_seq, _ = x_sample.shape
    depth = norm_ffn1.shape[0]
    xp = x_prompt.reshape(batch * seq, D_MODEL)
    xs = x_sample.reshape(dec_batch * dec_seq, D_MODEL)
    pool_p, pool_s, chunk_s = [], [], []
    kv_p = [[] for _ in ATTN_GROUPS]
    kv_s = [[] for _ in ATTN_GROUPS]
    half_steps = [(w_in, w_out, i) for i in range(depth)
                  for w_in, w_out in ((ffn1_w_in, ffn1_w_out), (ffn2_w_in, ffn2_w_out))]
    w_bf = (ffn1_w_in[0].astype(BF16), ffn1_w_out[0].astype(BF16))

    def ffn(xp, xs, g, w_bf, step):
        nxt = half_steps[step + 1] if step + 1 < len(half_steps) else None
        out = _ffn(xp, xs, g, *w_bf, nxt=nxt)
        return out[0], out[1], tuple(out[2:])

    for i in range(depth):
        kind, j = i % 3, i // 3
        xp, xs, w_bf = ffn(xp, xs, norm_ffn1[i], w_bf, 2 * i)
        if kind == 0:
            zero_buf = jnp.zeros((batch, POOL_BUF, D_MODEL), F32)
            yp, st_p = _pool(xp.reshape(batch, seq, D_MODEL), zero_buf, norm_mix[i], pool_w[j],
                             pool_scale[j], 0)
            ys, st_s = _pool(xs.reshape(dec_batch, dec_seq, D_MODEL), state_pool[j], norm_mix[i],
                             pool_w[j], pool_scale[j], PAST_LEN)
            xp = yp.reshape(batch * seq, D_MODEL)
            xs = ys.reshape(dec_batch * dec_seq, D_MODEL)
            pool_p.append(st_p)
            pool_s.append(st_s)
        elif kind == 1:
            args = (norm_mix[i], chunk_w_in[j], chunk_v_norm[j], chunk_w_s[j], chunk_b_s[j],
                    chunk_w_out[j])
            (xp,) = _chunk(xp, *args, emit_v=False)
            xs_pad = jnp.pad(xs.reshape(dec_batch, dec_seq, D_MODEL),
                             ((0, 0), (0, CHUNK - dec_seq), (0, 0)))
            ys_pad, v_pad = _chunk(xs_pad.reshape(dec_batch * CHUNK, D_MODEL), *args, emit_v=True)
            xs = ys_pad.reshape(dec_batch, CHUNK, D_MODEL)[:, :dec_seq].reshape(-1, D_MODEL)
            chunk_s.append(v_pad.reshape(dec_batch, CHUNK, D_MODEL)[:, :dec_seq])
        else:
            hs = _norm_perm(xp, norm_mix[i])
            qkvs = []
            for gi, (window, dil) in enumerate(ATTN_GROUPS):
                pos = (np.arange(seq // ROW_TILE)[:, None] * ROW_TILE + _tile_perm(dil)[None, :]).reshape(-1)
                cos_p, sin_p = _rope_tables(jnp.asarray(pos))
                q, k, v, kv_final = _qkv(hs[gi], attn_w_qkv[j], attn_q_norm[j], attn_k_norm[j],
                                         cos_p, sin_p, gi, seq)
                qkvs.append((q, k, v))
                kv_p[gi].append(kv_final.reshape(batch, min(window, seq), 2, HEADS, HEAD_DIM))
            xp = _out_proj(xp, _attn(qkvs, batch, seq), attn_w_out[j])

            cos_s, sin_s = _rope_tables(PAST_LEN + jnp.arange(dec_seq))
            y_s = _qkv_rows(xs, norm_mix[i], attn_w_qkv[j])
            o_s, *new_rows = _cache_attn(y_s, tuple(c[j] for c in caches), attn_q_norm[j],
                                         attn_k_norm[j], cos_s, sin_s, dec_batch, dec_seq)
            for gi in range(N_GROUPS):
                kv_s[gi].append(new_rows[gi])
            xs = _out_proj(xs, o_s.reshape(dec_batch * dec_seq, ATTN_W).astype(BF16), attn_w_out[j])
        xp, xs, w_bf = ffn(xp, xs, norm_ffn2[i], w_bf, 2 * i + 1)
    return (xp.reshape(batch, seq, D_MODEL), xs.reshape(dec_batch, dec_seq, D_MODEL),
            jnp.stack(pool_p), jnp.stack(pool_s), jnp.stack(chunk_s),
            jnp.stack(kv_p[0]), jnp.stack(kv_s[0]), jnp.stack(kv_p[1]), jnp.stack(kv_s[1]),
            jnp.stack(kv_p[2]), jnp.stack(kv_s[2]))
```

```python
import functools

import numpy as np
import jax
import jax.numpy as jnp
from jax import lax
from jax.experimental import pallas as pl
from jax.experimental.pallas import tpu as pltpu

F32 = jnp.float32
BF16 = jnp.bfloat16

D_MODEL = 2048
D_FF = 5504
RMS_EPS = 1e-6
POOL_WINDOWS = (2, 4, 8, 16)
POOL_GROUP = D_MODEL // len(POOL_WINDOWS)
POOL_BUF = max(POOL_WINDOWS) - 1
CHUNK = 128
CHUNK_GROUPS = 8
CHUNK_GROUP_W = D_MODEL // CHUNK_GROUPS
ATTN_GROUPS = ((128, 1), (512, 4), (2048, 16))
N_GROUPS = len(ATTN_GROUPS)
HEADS = 16
HEAD_DIM = 128
ATTN_W = HEADS * HEAD_DIM
ROPE_THETA = 10000.0
ATTN_SCALE = HEAD_DIM ** -0.5
NEG = float(np.finfo(np.float32).min)
PAST_LEN = 16384

LANE = 128
SUBLANE = 8
FF_TILE = 512
N_FF_TILES = -(-D_FF // FF_TILE)
FFN_ROW_TILE = 1024
CONVERT_IN_ROWS = 32
CONVERT_OUT_ROWS = 64
ROW_TILE = 512
CHUNK_ROW_TILE = 256
POOL_HALO = SUBLANE * len(POOL_WINDOWS)
ATTN_BLOCK = 128
ATTN_INTERLEAVE = 8
QKV_COL_TILE = 1024
HEADS_PER_STEP = QKV_COL_TILE // HEAD_DIM
COL_TILES_PER_ROLE = ATTN_W // QKV_COL_TILE
SLAB_PITCH = ROW_TILE + SUBLANE
MIB = 1024 * 1024


def _params(semantics, vmem_mib):
    return pltpu.CompilerParams(dimension_semantics=semantics, vmem_limit_bytes=vmem_mib * MIB)


def _rms(x, g):
    ms = jnp.mean(x * x, axis=-1, keepdims=True)
    return x * lax.rsqrt(ms + RMS_EPS) * g


def _dot(a, b):
    return jnp.dot(a, b, preferred_element_type=F32)


def _div(x, d):
    return lax.shift_right_logical(x, d.bit_length() - 1) if d & (d - 1) == 0 else x // d


def _mod(x, d):
    return x & (d - 1) if d & (d - 1) == 0 else x % d


def _dot_nt(a, b):
    return lax.dot_general(a, b, (((1,), (1,)), ((), ())), preferred_element_type=F32)


def _ff_start(f, base=0):
    return (base // LANE + jnp.minimum(f * (FF_TILE // LANE), (D_FF - FF_TILE) // LANE)) * LANE


def _ffn_body(xp_ref, xs_ref, g_ref, wg_ref, wu_ref, wo_ref, *rest, convert_next):
    if convert_next:
        nwi_ref, nwo_ref, yp_ref, ys_ref, nwi_bf_ref, nwo_bf_ref, h_ref = rest
    else:
        yp_ref, ys_ref, h_ref = rest
    m = pl.program_id(0)
    f = pl.program_id(1)
    tm, ms = xp_ref.shape[0], xs_ref.shape[0]

    def normalise(x_ref, rows):
        h_ref[rows, :] = _rms(x_ref[...], g_ref[...]).astype(BF16)

    col = lax.broadcasted_iota(jnp.int32, (1, FF_TILE), 1)
    fresh = col >= f * FF_TILE - _ff_start(f)

    def accumulate(n_rows, first):
        h = h_ref[:n_rows]
        gate = _dot(h, wg_ref[...])
        up = _dot(h, wu_ref[...])
        hid = jnp.where(fresh, gate * jax.nn.sigmoid(gate) * up * 0.5, 0.0).astype(BF16)
        out = _dot(hid, wo_ref[...])
        yp_ref[...] = (xp_ref[...] if first else yp_ref[...]) + out[:tm]
        if n_rows > tm:
            ys_ref[...] = (xs_ref[...] if first else ys_ref[...]) + out[tm:]
        if convert_next:
            nwi_bf_ref[...] = nwi_ref[...].astype(BF16)
            nwo_bf_ref[...] = nwo_ref[...].astype(BF16)

    @pl.when(f == 0)
    def _():
        normalise(xp_ref, slice(0, tm))

    @pl.when((f == 0) & (m == 0))
    def _():
        normalise(xs_ref, slice(tm, tm + ms))

    for first in (True, False):
        for first_tile, n_rows in ((True, tm + ms), (False, tm)):
            @pl.when(((f == 0) == first) & ((m == 0) == first_tile))
            def _(n_rows=n_rows, first=first):
                accumulate(n_rows, first)


def _ffn(xp, xs, g, w_in, w_out, nxt=None):
    mp, ms = xp.shape[0], xs.shape[0]
    n_m = mp // FFN_ROW_TILE
    elem = pl.Element
    in_specs = [
        pl.BlockSpec((FFN_ROW_TILE, D_MODEL), lambda m, f: (m, 0)),
        pl.BlockSpec((ms, D_MODEL), lambda m, f: (0, 0)),
        pl.BlockSpec((1, D_MODEL), lambda m, f: (0, 0)),
        pl.BlockSpec((elem(D_MODEL), elem(FF_TILE)), lambda m, f: (0, _ff_start(f))),
        pl.BlockSpec((elem(D_MODEL), elem(FF_TILE)), lambda m, f: (0, _ff_start(f, D_FF))),
        pl.BlockSpec((elem(FF_TILE), elem(D_MODEL)), lambda m, f: (_ff_start(f), 0)),
    ]
    out_specs = [
        pl.BlockSpec((FFN_ROW_TILE, D_MODEL), lambda m, f: (m, 0)),
        pl.BlockSpec((ms, D_MODEL), lambda m, f: (0, 0)),
    ]
    out_shape = [jax.ShapeDtypeStruct(xp.shape, F32), jax.ShapeDtypeStruct(xs.shape, F32)]
    args = [xp, xs, g.reshape(1, D_MODEL), w_in, w_in, w_out]
    if nxt is not None:
        nwi, nwo, layer = nxt
        n_in, n_out = D_MODEL // CONVERT_IN_ROWS, D_FF // CONVERT_OUT_ROWS
        assert max(n_in, n_out) <= n_m * N_FF_TILES and D_FF % CONVERT_OUT_ROWS == 0
        slab = lambda m, f, n: jnp.minimum(m * N_FF_TILES + f, n - 1)
        in_specs += [
            pl.BlockSpec((None, CONVERT_IN_ROWS, 2 * D_FF), lambda m, f: (layer, slab(m, f, n_in), 0)),
            pl.BlockSpec((None, CONVERT_OUT_ROWS, D_MODEL), lambda m, f: (layer, slab(m, f, n_out), 0)),
        ]
        out_specs += [
            pl.BlockSpec((CONVERT_IN_ROWS, 2 * D_FF), lambda m, f: (slab(m, f, n_in), 0)),
            pl.BlockSpec((CONVERT_OUT_ROWS, D_MODEL), lambda m, f: (slab(m, f, n_out), 0)),
        ]
        out_shape += [jax.ShapeDtypeStruct((D_MODEL, 2 * D_FF), BF16),
                      jax.ShapeDtypeStruct((D_FF, D_MODEL), BF16)]
        args += [nwi, nwo]
    return pl.pallas_call(
        functools.partial(_ffn_body, convert_next=nxt is not None),
        grid=(n_m, N_FF_TILES),
        in_specs=in_specs,
        out_specs=out_specs,
        out_shape=out_shape,
        scratch_shapes=[pltpu.VMEM((FFN_ROW_TILE + ms, D_MODEL), BF16)],
        compiler_params=_params(("arbitrary", "arbitrary"), 62),
        name="ffn",
    )(*args)


def _pool_body(x_ref, buf_ref, g_ref, pw_ref, sc_ref, y_ref, st_ref, hb_ref, sa_ref, sb_ref, *,
               tt, start, nt):
    t = pl.program_id(1)
    halo = POOL_HALO
    n = halo + tt

    @pl.when(t == 0)
    def _():
        hb_ref[0:halo, :] = buf_ref[0]

    x = x_ref[0]
    h = _rms(x, g_ref[...])
    hb_ref[halo:n, :] = h
    refs = (hb_ref, sa_ref, sb_ref, sa_ref, sb_ref)
    for k, w in enumerate(POOL_WINDOWS):
        src, dst = refs[k], refs[k + 1]
        lo, cols = SUBLANE * (k + 1), slice(k * POOL_GROUP, D_MODEL)
        dst[lo:n, cols] = src[lo:n, cols] + src[lo - w // 2:n - w // 2, cols]
    pos = start + t * tt + lax.broadcasted_iota(jnp.int32, (tt, 1), 0)
    for gi, w in enumerate(POOL_WINDOWS):
        sl = slice(gi * POOL_GROUP, (gi + 1) * POOL_GROUP)
        win = refs[gi + 1][halo:n, sl]
        count = jnp.minimum(w, pos + 1).astype(F32)
        pooled = win / count - h[:, sl]
        mixed = _dot(pooled.astype(BF16), pw_ref[gi])
        y_ref[0, :, sl] = x[:, sl] + mixed * sc_ref[:, sl]
    st_ref[0] = hb_ref[n - POOL_BUF:n, :]
    if nt > 1:
        hb_ref[0:halo, :] = hb_ref[tt:n, :]


def _pool(x, buf, g, pool_w, scale, start):
    b, t_len, _ = x.shape
    tt = min(t_len, ROW_TILE)
    nt = t_len // tt
    halo = POOL_HALO
    assert POOL_WINDOWS == tuple(2 ** (k + 1) for k in range(len(POOL_WINDOWS))) and halo > POOL_BUF
    buf16 = jnp.pad(buf, ((0, 0), (halo - POOL_BUF, 0), (0, 0)))
    body = functools.partial(_pool_body, tt=tt, start=start, nt=nt)
    return pl.pallas_call(
        body,
        grid=(b, nt),
        in_specs=[
            pl.BlockSpec((1, tt, D_MODEL), lambda i, t: (i, t, 0)),
            pl.BlockSpec((1, halo, D_MODEL), lambda i, t: (i, 0, 0)),
            pl.BlockSpec((1, D_MODEL), lambda i, t: (0, 0)),
            pl.BlockSpec((len(POOL_WINDOWS), POOL_GROUP, POOL_GROUP), lambda i, t: (0, 0, 0)),
            pl.BlockSpec((1, D_MODEL), lambda i, t: (0, 0)),
        ],
        out_specs=[
            pl.BlockSpec((1, tt, D_MODEL), lambda i, t: (i, t, 0)),
            pl.BlockSpec((1, POOL_BUF, D_MODEL), lambda i, t: (i, 0, 0)),
        ],
        out_shape=[jax.ShapeDtypeStruct(x.shape, F32),
                   jax.ShapeDtypeStruct((b, POOL_BUF, D_MODEL), F32)],
        scratch_shapes=[pltpu.VMEM((halo + tt, D_MODEL), F32)] * 3,
        compiler_params=_params(("arbitrary", "arbitrary"), 40),
        name="pool",
    )(x, buf16, g.reshape(1, D_MODEL), pool_w.astype(BF16), scale.reshape(1, D_MODEL))


def _chunk_body(x_ref, g_ref, win_ref, vg_ref, ws_ref, bst_ref, wout_ref, *rest, tm, emit_v):
    if emit_v:
        y_ref, vn_ref, mix_ref = rest
    else:
        y_ref, mix_ref = rest
    x = x_ref[...]
    h = _rms(x, g_ref[...]).astype(BF16)
    uv = _dot(h, win_ref[...])
    uv = 0.5 * uv * (1.0 + lax.erf(uv * np.float32(np.sqrt(0.5))))
    u = uv[:, :D_MODEL]
    vn = _rms(uv[:, D_MODEL:], vg_ref[...])
    if emit_v:
        vn_ref[...] = vn
    q_idx = lax.broadcasted_iota(jnp.int32, (CHUNK, CHUNK), 0)
    c_idx = lax.broadcasted_iota(jnp.int32, (CHUNK, CHUNK), 1)
    causal = c_idx <= q_idx
    for gi in range(CHUNK_GROUPS):
        cols = slice(gi * CHUNK_GROUP_W, (gi + 1) * CHUNK_GROUP_W)
        ws = jnp.where(causal, ws_ref[gi], 0.0).astype(BF16)
        bias = bst_ref[:, gi:gi + 1]
        for c in range(tm // CHUNK):
            rows = slice(c * CHUNK, (c + 1) * CHUNK)
            mixed = _dot(ws, vn[rows, cols].astype(BF16)) + bias
            mix_ref[rows, cols] = (u[rows, cols] * mixed).astype(BF16)
    y_ref[...] = x + _dot(mix_ref[...], wout_ref[...])


def _chunk(x, g, w_in, v_gain, w_s, b_s, w_out, emit_v):
    m = x.shape[0]
    tm = CHUNK_ROW_TILE
    body = functools.partial(_chunk_body, tm=tm, emit_v=emit_v)
    row_spec = pl.BlockSpec((tm, D_MODEL), lambda i: (i, 0))
    once = pl.Buffered(1)
    out_specs = [row_spec]
    out_shape = [jax.ShapeDtypeStruct((m, D_MODEL), F32)]
    if emit_v:
        out_specs.append(row_spec)
        out_shape.append(jax.ShapeDtypeStruct((m, D_MODEL), F32))
    return pl.pallas_call(
        body,
        grid=(m // tm,),
        in_specs=[
            row_spec,
            pl.BlockSpec((1, D_MODEL), lambda i: (0, 0)),
            pl.BlockSpec((D_MODEL, 2 * D_MODEL), lambda i: (0, 0), pipeline_mode=once),
            pl.BlockSpec((1, D_MODEL), lambda i: (0, 0)),
            pl.BlockSpec((CHUNK_GROUPS, CHUNK, CHUNK), lambda i: (0, 0, 0)),
            pl.BlockSpec((CHUNK, CHUNK_GROUPS), lambda i: (0, 0)),
            pl.BlockSpec((D_MODEL, D_MODEL), lambda i: (0, 0), pipeline_mode=once),
        ],
        out_specs=out_specs,
        out_shape=out_shape,
        scratch_shapes=[pltpu.VMEM((tm, D_MODEL), BF16)],
        compiler_params=_params(("arbitrary",), 56),
        name="chunk",
    )(x, g.reshape(1, D_MODEL), w_in.astype(BF16), v_gain.reshape(1, D_MODEL), w_s, b_s.T,
      w_out.astype(BF16))


def _tile_perm(dil):
    u = np.arange(ROW_TILE)
    n_k = ROW_TILE // dil
    return (u % n_k) * dil + u // n_k


def _rope_tables(pos):
    half = HEAD_DIM // 2
    freqs = ROPE_THETA ** (-2.0 * jnp.arange(half, dtype=F32) / HEAD_DIM)
    ang = pos.astype(F32)[:, None] * freqs[None, :]
    cos, sin = jnp.cos(ang), jnp.sin(ang)
    return jnp.concatenate([cos, cos], axis=-1), jnp.concatenate([-sin, sin], axis=-1)


def _rope(x, cos, sin_signed):
    return x * cos + pltpu.roll(x, HEAD_DIM // 2, 1) * sin_signed


def _norm_perm_body(x_ref, g_ref, *h_refs):
    h = _rms(x_ref[...], g_ref[...]).astype(BF16)
    u = lax.broadcasted_iota(jnp.int32, (ROW_TILE, ROW_TILE), 0)
    t = lax.broadcasted_iota(jnp.int32, (ROW_TILE, ROW_TILE), 1)
    for h_ref, (_, dil) in zip(h_refs, ATTN_GROUPS):
        if dil == 1:
            h_ref[...] = h
        else:
            n_k = ROW_TILE // dil
            src = (u & (n_k - 1)) * dil + lax.shift_right_logical(u, n_k.bit_length() - 1)
            select = jnp.where(t == src, 1.0, 0.0).astype(BF16)
            h_ref[...] = _dot(select, h).astype(BF16)


def _norm_perm(x, g):
    m = x.shape[0]
    row_spec = pl.BlockSpec((ROW_TILE, D_MODEL), lambda i: (i, 0))
    return pl.pallas_call(
        _norm_perm_body,
        grid=(m // ROW_TILE,),
        in_specs=[row_spec, pl.BlockSpec((1, D_MODEL), lambda i: (0, 0))],
        out_specs=[row_spec] * N_GROUPS,
        out_shape=[jax.ShapeDtypeStruct((m, D_MODEL), BF16)] * N_GROUPS,
        compiler_params=_params(("arbitrary",), 40),
        name="norm_perm",
    )(x, g.reshape(1, D_MODEL))


def _qkv_body(h_ref, w_ref, qn_ref, kn_ref, cos_ref, sin_ref, q_ref, k_ref, v_ref, kvf_ref,
              wbf_ref, ya_ref, yb_ref, slab_ref, *, dil, final_rows, tiles_per_seq, table_tiles, n_m,
              n_steps):
    p = pl.program_id(0)
    done = jnp.maximum(p - 1, 0)
    role = _div(_div(done, n_m), COL_TILES_PER_ROLE)
    row_tile = _mod(done, n_m)

    @pl.when(p == 0)
    def _():
        yb_ref[...] = jnp.zeros_like(yb_ref)

    @pl.when((_mod(jnp.minimum(p, n_steps - 1), n_m) == 0) & (p < n_steps))
    def _():
        wbf_ref[...] = w_ref[...].astype(BF16)

    def to_slab(hl, val):
        slab_ref[hl * SLAB_PITCH:hl * SLAB_PITCH + ROW_TILE, :] = val

    def gather_final():
        n_k = ROW_TILE // dil
        for t in range(final_rows):
            t_nat = (ROW_TILE - final_rows) + t
            u = (t_nat % dil) * n_k + t_nat // dil
            kvf_ref[t, 0] = slab_ref[pl.ds(u, HEADS_PER_STEP, stride=SLAB_PITCH), :]

    def emit_final():
        if tiles_per_seq == 1:
            gather_final()
        else:
            pl.when(_mod(row_tile, tiles_per_seq) == tiles_per_seq - 1)(gather_final)

    def finish(which, y_ref):
        rows = pl.ds(pl.multiple_of(_mod(row_tile, table_tiles) * ROW_TILE, ROW_TILE), ROW_TILE)
        for hl in range(HEADS_PER_STEP):
            yh = y_ref[:, hl * HEAD_DIM:(hl + 1) * HEAD_DIM]
            if which == 0:
                qn = _rope(_rms(yh, qn_ref[...]), cos_ref[rows, :], sin_ref[rows, :])
                q_ref[hl] = (qn * ATTN_SCALE).astype(BF16)
            elif which == 1:
                kn = _rope(_rms(yh, kn_ref[...]), cos_ref[rows, :], sin_ref[rows, :])
                k_ref[hl] = kn.astype(BF16)
                to_slab(hl, kn)
            else:
                v_ref[hl] = yh.astype(BF16)
                to_slab(hl, yh)
        if which > 0:
            emit_final()

    for parity, (cur_ref, prev_ref) in enumerate(((ya_ref, yb_ref), (yb_ref, ya_ref))):
        for which in range(3):
            @pl.when((_mod(p, 2) == parity) & (role == which))
            def _(which=which, cur_ref=cur_ref, prev_ref=prev_ref):
                cur_ref[...] = _dot(h_ref[...], wbf_ref[...])
                finish(which, prev_ref)


def _qkv(h, w_qkv, q_gain, k_gain, cos, sin, group, seq):
    window, dil = ATTN_GROUPS[group]
    m = h.shape[0]
    n_m = m // ROW_TILE
    tiles_per_seq = seq // ROW_TILE
    keep = min(window, seq)
    final_rows = min(keep, ROW_TILE)
    every_tile = keep == seq
    halves = COL_TILES_PER_ROLE
    n_steps = 3 * halves * n_m

    def project_tile(p):
        q = jnp.minimum(p, n_steps - 1)
        return _div(q, n_m), _mod(q, n_m)

    def finish_tile(p):
        e = jnp.maximum(p - 1, 0)
        return _div(e, n_m), _mod(e, n_m)

    def att_idx(s_own):
        def idx(p):
            c, i = finish_tile(p)
            s, hh = _div(c, halves), _mod(c, halves)
            before, after = s < s_own, s > s_own
            return (jnp.where(before, 0, jnp.where(after, halves - 1, hh)),
                    jnp.where(before, 0, jnp.where(after, n_m - 1, i)), 0)
        return idx

    def final_idx(p):
        c, i = finish_tile(p)
        s, hh = _div(c, halves), _mod(c, halves)
        rb = i if every_tile else _div(i, tiles_per_seq)
        live = s > 0
        return (jnp.where(live, rb, 0), jnp.where(live, s - 1, 0), jnp.where(live, hh, 0), 0)

    def w_idx(p):
        c, _ = project_tile(p)
        return (0, (_div(c, halves) * N_GROUPS + group) * halves + _mod(c, halves))

    body = functools.partial(_qkv_body, dil=dil, final_rows=final_rows, n_m=n_m, n_steps=n_steps,
                             tiles_per_seq=1 if every_tile else tiles_per_seq,
                             table_tiles=tiles_per_seq)
    att_shape = jax.ShapeDtypeStruct((HEADS, m, HEAD_DIM), BF16)
    att_block = (HEADS_PER_STEP, ROW_TILE, HEAD_DIM)
    table = pl.BlockSpec((seq, HEAD_DIM), lambda p: (0, 0))
    batch = m // seq
    return pl.pallas_call(
        body,
        grid=(n_steps + 1,),
        in_specs=[
            pl.BlockSpec((ROW_TILE, D_MODEL), lambda p: (project_tile(p)[1], 0)),
            pl.BlockSpec((D_MODEL, QKV_COL_TILE), w_idx),
            pl.BlockSpec((1, HEAD_DIM), lambda p: (0, 0)),
            pl.BlockSpec((1, HEAD_DIM), lambda p: (0, 0)),
            table,
            table,
        ],
        out_specs=[
            pl.BlockSpec(att_block, att_idx(0)),
            pl.BlockSpec(att_block, att_idx(1)),
            pl.BlockSpec(att_block, att_idx(2)),
            pl.BlockSpec((final_rows, 1, HEADS_PER_STEP, HEAD_DIM), final_idx),
        ],
        out_shape=[att_shape, att_shape, att_shape,
                   jax.ShapeDtypeStruct((batch * keep, 2, HEADS, HEAD_DIM), F32)],
        scratch_shapes=[pltpu.VMEM((D_MODEL, QKV_COL_TILE), BF16),
                        pltpu.VMEM((ROW_TILE, QKV_COL_TILE), F32),
                        pltpu.VMEM((ROW_TILE, QKV_COL_TILE), F32),
                        pltpu.VMEM((HEADS_PER_STEP * SLAB_PITCH, HEAD_DIM), F32)],
        compiler_params=_params(("arbitrary",), 48),
        name="qkv",
    )(h, w_qkv, q_gain.reshape(1, HEAD_DIM), k_gain.reshape(1, HEAD_DIM), cos, sin)


def _band_mask(n_keys):
    qi = lax.broadcasted_iota(jnp.int32, (ATTN_BLOCK, n_keys), 0)
    kj = lax.broadcasted_iota(jnp.int32, (ATTN_BLOCK, n_keys), 1)
    dist = (n_keys - ATTN_BLOCK) + qi - kj
    return (dist >= 0) & (dist <= ATTN_BLOCK)


def _attn_body(*refs, seq):
    qkv_refs, o_ref, og_ref, lg_ref = refs[:3 * N_GROUPS], refs[3 * N_GROUPS], refs[-2], refs[-1]
    masks = {n: _band_mask(n) for n in (ATTN_BLOCK, 2 * ATTN_BLOCK)}
    for gi, (_, dil) in enumerate(ATTN_GROUPS):
        q_ref, k_ref, v_ref = qkv_refs[3 * gi:3 * gi + 3]
        n_k = ROW_TILE // dil
        n_blocks = seq // dil // ATTN_BLOCK

        def rows_of(ref, r, blk):
            if n_k >= ATTN_BLOCK:
                per_tile = n_k // ATTN_BLOCK
                start = (blk // per_tile) * ROW_TILE + r * n_k + (blk % per_tile) * ATTN_BLOCK
                return ref[0, start:start + ATTN_BLOCK, :]
            parts = [ref[0, tl * ROW_TILE + r * n_k:tl * ROW_TILE + (r + 1) * n_k, :]
                     for tl in range(blk * (ATTN_BLOCK // n_k), (blk + 1) * (ATTN_BLOCK // n_k))]
            return jnp.concatenate(parts, axis=0)

        def keys_of(ref, r, blk):
            if blk == 0:
                return rows_of(ref, r, 0)
            return jnp.concatenate([rows_of(ref, r, blk - 1), rows_of(ref, r, blk)], axis=0)

        todo = [(r, blk) for r in range(dil) for blk in range(n_blocks)]
        for i in range(0, len(todo), ATTN_INTERLEAVE):
            batch = todo[i:i + ATTN_INTERLEAVE]
            ks = [keys_of(k_ref, r, blk) for r, blk in batch]
            ss = [jnp.where(masks[k.shape[0]], _dot_nt(rows_of(q_ref, r, blk), k), NEG)
                  for (r, blk), k in zip(batch, ks)]
            mxs = [jnp.max(s, axis=-1, keepdims=True) for s in ss]
            ps = [jnp.exp(s - mx) for s, mx in zip(ss, mxs)]
            dens = [jnp.sum(p, axis=-1, keepdims=True) for p in ps]
            accs = [_dot(p.astype(BF16), keys_of(v_ref, r, blk)) for p, (r, blk) in zip(ps, batch)]
            for (r, blk), acc, mx, den in zip(batch, accs, mxs, dens):
                if dil == 1:
                    dst = pl.ds(blk * ATTN_BLOCK, ATTN_BLOCK)
                else:
                    dst = pl.ds(blk * ATTN_BLOCK * dil + r, ATTN_BLOCK, stride=dil)
                og_ref[gi, dst, :] = acc / den
                lg_ref[gi, dst, :] = jnp.broadcast_to(mx + jnp.log(den), (ATTN_BLOCK, HEAD_DIM))
    for blk in range(seq // ATTN_BLOCK):
        rows = slice(blk * ATTN_BLOCK, (blk + 1) * ATTN_BLOCK)
        ls = [lg_ref[gi, rows, :] for gi in range(N_GROUPS)]
        mx = functools.reduce(jnp.maximum, ls)
        es = [jnp.exp(l - mx) for l in ls]
        num = sum(e * og_ref[gi, rows, :] for gi, e in enumerate(es))
        o_ref[rows, :] = (num / sum(es)).astype(BF16)


def _attn(qkvs, batch, seq):
    flat = [a for qkv in qkvs for a in qkv]
    blk = pl.BlockSpec((1, seq, HEAD_DIM), lambda b, hd: (hd, b, 0))
    return pl.pallas_call(
        functools.partial(_attn_body, seq=seq),
        grid=(batch, HEADS),
        in_specs=[blk] * len(flat),
        out_specs=pl.BlockSpec((seq, HEAD_DIM), lambda b, hd: (b, hd)),
        out_shape=jax.ShapeDtypeStruct((batch * seq, ATTN_W), BF16),
        scratch_shapes=[pltpu.VMEM((N_GROUPS, seq, HEAD_DIM), F32)] * 2,
        compiler_params=_params(("arbitrary", "arbitrary"), 40),
        name="attn",
    )(*flat)


def _out_proj_body(x_ref, o_ref, w_ref, y_ref, wbf_ref):
    @pl.when(pl.program_id(0) == 0)
    def _():
        wbf_ref[...] = w_ref[...].astype(BF16)

    y_ref[...] = x_ref[...] + _dot(o_ref[...], wbf_ref[...])


def _out_proj(x, o, w_out):
    m = x.shape[0]
    tm = min(m, ROW_TILE)
    return pl.pallas_call(
        _out_proj_body,
        grid=(m // tm,),
        in_specs=[pl.BlockSpec((tm, D_MODEL), lambda i: (i, 0)),
                  pl.BlockSpec((tm, ATTN_W), lambda i: (i, 0)),
                  pl.BlockSpec((ATTN_W, D_MODEL), lambda i: (0, 0), pipeline_mode=pl.Buffered(1))],
        out_specs=pl.BlockSpec((tm, D_MODEL), lambda i: (i, 0)),
        out_shape=jax.ShapeDtypeStruct((m, D_MODEL), F32),
        scratch_shapes=[pltpu.VMEM((ATTN_W, D_MODEL), BF16)],
        compiler_params=_params(("arbitrary",), 48),
        name="attn_out_proj",
    )(x, o, w_out)


def _qkv_rows_body(x_ref, g_ref, w_ref, y_ref):
    h = _rms(x_ref[...], g_ref[...]).astype(BF16)
    y_ref[...] = _dot(h, w_ref[...].astype(BF16))


def _qkv_rows(x, g, w_qkv):
    m = x.shape[0]
    n = w_qkv.shape[1]
    return pl.pallas_call(
        _qkv_rows_body,
        grid=(n // QKV_COL_TILE,),
        in_specs=[pl.BlockSpec((m, D_MODEL), lambda c: (0, 0)),
                  pl.BlockSpec((1, D_MODEL), lambda c: (0, 0)),
                  pl.BlockSpec((D_MODEL, QKV_COL_TILE), lambda c: (0, c))],
        out_specs=pl.BlockSpec((m, QKV_COL_TILE), lambda c: (0, c)),
        out_shape=jax.ShapeDtypeStruct((m, n), F32),
        compiler_params=_params(("arbitrary",), 32),
        name="qkv_rows",
    )(x, g.reshape(1, D_MODEL), w_qkv)


def _cache_attn_body(y_ref, c0_ref, c1_ref, c2_ref, qn_ref, kn_ref, cos_ref, sin_ref,
                     o_ref, n0_ref, n1_ref, n2_ref, *, t_len):
    cache_refs = (c0_ref, c1_ref, c2_ref)
    new_refs = (n0_ref, n1_ref, n2_ref)

    def norm_rope(val, gain_ref, t):
        return _rope(_rms(val, gain_ref[...]), cos_ref[t:t + 1, :], sin_ref[t:t + 1, :])

    k_new = [[norm_rope(y_ref[0, t, 1, gi], kn_ref, t) for t in range(t_len)]
             for gi in range(N_GROUPS)]
    v_new = [[y_ref[0, t, 2, gi] for t in range(t_len)] for gi in range(N_GROUPS)]
    for gi in range(N_GROUPS):
        for t in range(t_len):
            new_refs[gi][0, t, 0] = k_new[gi][t]
            new_refs[gi][0, t, 1] = v_new[gi][t]

    cached_pos = lax.broadcasted_iota(jnp.int32, (ATTN_BLOCK, 1, 1), 0)
    for t in range(t_len):
        outs, lses = [], []
        for gi, (_, dil) in enumerate(ATTN_GROUPS):
            c_ref = cache_refs[gi]
            q = norm_rope(y_ref[0, t, 0, gi], qn_ref, t) * ATTN_SCALE
            if dil == 1:
                kc, vc = c_ref[0, :, 0], c_ref[0, :, 1]
                new_rows = range(t + 1)
            else:
                kc, vc = c_ref[0, :, t, 0], c_ref[0, :, t, 1]
                new_rows = [t]
            s_c = jnp.sum(kc * q[None], axis=-1, keepdims=True)
            if dil == 1:
                s_c = jnp.where(cached_pos >= t, s_c, NEG)
            s_n = [jnp.sum(k_new[gi][tt] * q, axis=-1, keepdims=True) for tt in new_rows]
            mx = functools.reduce(jnp.maximum, s_n, jnp.max(s_c, axis=0))
            p_c = jnp.exp(s_c - mx[None])
            p_n = [jnp.exp(sn - mx) for sn in s_n]
            den = jnp.sum(p_c, axis=0) + sum(p_n)
            acc = jnp.sum(p_c * vc, axis=0) + sum(p * v_new[gi][tt] for p, tt in zip(p_n, new_rows))
            outs.append(acc / den)
            lses.append(mx + jnp.log(den))
        lmax = functools.reduce(jnp.maximum, lses)
        es = [jnp.exp(l - lmax) for l in lses]
        o_ref[0, t] = sum(e * o for e, o in zip(es, outs)) / sum(es)


def _cache_attn(y, caches, q_gain, k_gain, cos, sin, batch, t_len):
    y6 = y.reshape(batch, t_len, 3, N_GROUPS, HEADS, HEAD_DIM)
    views, specs = [], []
    for cache, (window, dil) in zip(caches, ATTN_GROUPS):
        assert cache.shape[1] == window and window // dil == ATTN_BLOCK and (dil == 1 or t_len <= dil)
        if dil == 1:
            views.append(cache)
            specs.append(pl.BlockSpec((1, ATTN_BLOCK, 2, HEADS, HEAD_DIM), lambda b: (b, 0, 0, 0, 0)))
        else:
            views.append(cache.reshape(batch, ATTN_BLOCK, dil, 2, HEADS, HEAD_DIM))
            specs.append(pl.BlockSpec((1, ATTN_BLOCK, t_len, 2, HEADS, HEAD_DIM),
                                      lambda b: (b, 0, 0, 0, 0, 0)))
    new_shape = jax.ShapeDtypeStruct((batch, t_len, 2, HEADS, HEAD_DIM), F32)
    new_spec = pl.BlockSpec((1, t_len, 2, HEADS, HEAD_DIM), lambda b: (b, 0, 0, 0, 0))
    vec = pl.BlockSpec((1, HEAD_DIM), lambda b: (0, 0))
    tab = pl.BlockSpec((t_len, HEAD_DIM), lambda b: (0, 0))
    return pl.pallas_call(
        functools.partial(_cache_attn_body, t_len=t_len),
        grid=(batch,),
        in_specs=[pl.BlockSpec((1, t_len, 3, N_GROUPS, HEADS, HEAD_DIM), lambda b: (b, 0, 0, 0, 0, 0))]
        + specs + [vec, vec, tab, tab],
        out_specs=[pl.BlockSpec((1, t_len, HEADS, HEAD_DIM), lambda b: (b, 0, 0, 0))] + [new_spec] * N_GROUPS,
        out_shape=[jax.ShapeDtypeStruct((batch, t_len, HEADS, HEAD_DIM), F32)] + [new_shape] * N_GROUPS,
        compiler_params=_params(("arbitrary",), 56),
        name="cache_attn",
    )(y6, *views, q_gain.reshape(1, HEAD_DIM), k_gain.reshape(1, HEAD_DIM), cos, sin)


def kernel(x_prompt, x_sample, state_pool, cache_kv_g0, cache_kv_g1, cache_kv_g2, norm_ffn1, ffn1_w_in, ffn1_w_out, norm_mix, norm_ffn2, ffn2_w_in, ffn2_w_out, pool_w, pool_scale, chunk_w_in, chunk_v_norm, chunk_w_s, chunk_b_s, chunk_w_out, attn_w_qkv, attn_q_norm, attn_k_norm, attn_w_out):
    caches = (cache_kv_g0, cache_kv_g1, cache_kv_g2)
    batch, seq, _ = x_prompt.shape
    dec_batch, dec_seq, _ = x_sample.shape
    depth = norm_ffn1.shape[0]
    xp = x_prompt.reshape(batch * seq, D_MODEL)
    xs = x_sample.reshape(dec_batch * dec_seq, D_MODEL)
    pool_p, pool_s, chunk_s = [], [], []
    kv_p = [[] for _ in ATTN_GROUPS]
    kv_s = [[] for _ in ATTN_GROUPS]
    half_steps = [(w_in, w_out, i) for i in range(depth)
                  for w_in, w_out in ((ffn1_w_in, ffn1_w_out), (ffn2_w_in, ffn2_w_out))]
    w_bf = (ffn1_w_in[0].astype(BF16), ffn1_w_out[0].astype(BF16))

    def ffn(xp, xs, g, w_bf, step):
        nxt = half_steps[step + 1] if step + 1 < len(half_steps) else None
        out = _ffn(xp, xs, g, *w_bf, nxt=nxt)
        return out[0], out[1], tuple(out[2:])

    for i in range(depth):
        kind, j = i % 3, i // 3
        xp, xs, w_bf = ffn(xp, xs, norm_ffn1[i], w_bf, 2 * i)
        if kind == 0:
            zero_buf = jnp.zeros((batch, POOL_BUF, D_MODEL), F32)
            yp, st_p = _pool(xp.reshape(batch, seq, D_MODEL), zero_buf, norm_mix[i], pool_w[j],
                             pool_scale[j], 0)
            ys, st_s = _pool(xs.reshape(dec_batch, dec_seq, D_MODEL), state_pool[j], norm_mix[i],
                             pool_w[j], pool_scale[j], PAST_LEN)
            xp = yp.reshape(batch * seq, D_MODEL)
            xs = ys.reshape(dec_batch * dec_seq, D_MODEL)
            pool_p.append(st_p)
            pool_s.append(st_s)
        elif kind == 1:
            args = (norm_mix[i], chunk_w_in[j], chunk_v_norm[j], chunk_w_s[j], chunk_b_s[j],
                    chunk_w_out[j])
            (xp,) = _chunk(xp, *args, emit_v=False)
            xs_pad = jnp.pad(xs.reshape(dec_batch, dec_seq, D_MODEL),
                             ((0, 0), (0, CHUNK - dec_seq), (0, 0)))
            ys_pad, v_pad = _chunk(xs_pad.reshape(dec_batch * CHUNK, D_MODEL), *args, emit_v=True)
            xs = ys_pad.reshape(dec_batch, CHUNK, D_MODEL)[:, :dec_seq].reshape(-1, D_MODEL)
            chunk_s.append(v_pad.reshape(dec_batch, CHUNK, D_MODEL)[:, :dec_seq])
        else:
            hs = _norm_perm(xp, norm_mix[i])
            qkvs = []
            for gi, (window, dil) in enumerate(ATTN_GROUPS):
                pos = (np.arange(seq // ROW_TILE)[:, None] * ROW_TILE + _tile_perm(dil)[None, :]).reshape(-1)
                cos_p, sin_p = _rope_tables(jnp.asarray(pos))
                q, k, v, kv_final = _qkv(hs[gi], attn_w_qkv[j], attn_q_norm[j], attn_k_norm[j],
                                         cos_p, sin_p, gi, seq)
                qkvs.append((q, k, v))
                kv_p[gi].append(kv_final.reshape(batch, min(window, seq), 2, HEADS, HEAD_DIM))
            xp = _out_proj(xp, _attn(qkvs, batch, seq), attn_w_out[j])

            cos_s, sin_s = _rope_tables(PAST_LEN + jnp.arange(dec_seq))
            y_s = _qkv_rows(xs, norm_mix[i], attn_w_qkv[j])
            o_s, *new_rows = _cache_attn(y_s, tuple(c[j] for c in caches), attn_q_norm[j],
                                         attn_k_norm[j], cos_s, sin_s, dec_batch, dec_seq)
            for gi in range(N_GROUPS):
                kv_s[gi].append(new_rows[gi])
            xs = _out_proj(xs, o_s.reshape(dec_batch * dec_seq, ATTN_W).astype(BF16), attn_w_out[j])
        xp, xs, w_bf = ffn(xp, xs, norm_ffn2[i], w_bf, 2 * i + 1)
    return (xp.reshape(batch, seq, D_MODEL), xs.reshape(dec_batch, dec_seq, D_MODEL),
            jnp.stack(pool_p), jnp.stack(pool_s), jnp.stack(chunk_s),
            jnp.stack(kv_p[0]), jnp.stack(kv_s[0]), jnp.stack(kv_p[1]), jnp.stack(kv_s[1]),
            jnp.stack(kv_p[2]), jnp.stack(kv_s[2]))
```

```python
import functools

import numpy as np
import jax
import jax.numpy as jnp
from jax import lax
from jax.experimental import pallas as pl
from jax.experimental.pallas import tpu as pltpu

F32 = jnp.float32
BF16 = jnp.bfloat16

D_MODEL = 2048
D_FF = 5504
RMS_EPS = 1e-6
POOL_WINDOWS = (2, 4, 8, 16)
POOL_GROUP = D_MODEL // len(POOL_WINDOWS)
POOL_BUF = max(POOL_WINDOWS) - 1
CHUNK = 128
CHUNK_GROUPS = 8
CHUNK_GROUP_W = D_MODEL // CHUNK_GROUPS
ATTN_GROUPS = ((128, 1), (512, 4), (2048, 16))
N_GROUPS = len(ATTN_GROUPS)
HEADS = 16
HEAD_DIM = 128
ATTN_W = HEADS * HEAD_DIM
ROPE_THETA = 10000.0
ATTN_SCALE = HEAD_DIM ** -0.5
NEG = float(np.finfo(np.float32).min)
PAST_LEN = 16384

LANE = 128
SUBLANE = 8
FF_TILE = 512
N_FF_TILES = -(-D_FF // FF_TILE)
FFN_ROW_TILE = 1024
CONVERT_IN_ROWS = 32
CONVERT_OUT_ROWS = 64
ROW_TILE = 512
CHUNK_ROW_TILE = 256
POOL_HALO = SUBLANE * len(POOL_WINDOWS)
ATTN_BLOCK = 128
ATTN_INTERLEAVE = 8
QKV_COL_TILE = 1024
HEADS_PER_STEP = QKV_COL_TILE // HEAD_DIM
COL_TILES_PER_ROLE = ATTN_W // QKV_COL_TILE
SLAB_PITCH = ROW_TILE + SUBLANE
MIB = 1024 * 1024


def _params(semantics, vmem_mib):
    return pltpu.CompilerParams(dimension_semantics=semantics, vmem_limit_bytes=vmem_mib * MIB)


def _rms(x, g):
    ms = jnp.mean(x * x, axis=-1, keepdims=True)
    return x * lax.rsqrt(ms + RMS_EPS) * g


def _dot(a, b):
    return jnp.dot(a, b, preferred_element_type=F32)


def _div(x, d):
    return lax.shift_right_logical(x, d.bit_length() - 1) if d & (d - 1) == 0 else x // d


def _mod(x, d):
    return x & (d - 1) if d & (d - 1) == 0 else x % d


def _dot_nt(a, b):
    return lax.dot_general(a, b, (((1,), (1,)), ((), ())), preferred_element_type=F32)


def _ff_start(f, base=0):
    return (base // LANE + jnp.minimum(f * (FF_TILE // LANE), (D_FF - FF_TILE) // LANE)) * LANE


def _ffn_body(xp_ref, xs_ref, g_ref, wg_ref, wu_ref, wo_ref, *rest, convert_next):
    if convert_next:
        nwi_ref, nwo_ref, yp_ref, ys_ref, nwi_bf_ref, nwo_bf_ref, h_ref = rest
    else:
        yp_ref, ys_ref, h_ref = rest
    m = pl.program_id(0)
    f = pl.program_id(1)
    tm, ms = xp_ref.shape[0], xs_ref.shape[0]

    def normalise(x_ref, rows):
        h_ref[rows, :] = _rms(x_ref[...], g_ref[...]).astype(BF16)

    col = lax.broadcasted_iota(jnp.int32, (1, FF_TILE), 1)
    fresh = col >= f * FF_TILE - _ff_start(f)

    def accumulate(n_rows, first):
        h = h_ref[:n_rows]
        gate = _dot(h, wg_ref[...])
        up = _dot(h, wu_ref[...])
        hid = jnp.where(fresh, gate * jax.nn.sigmoid(gate) * up * 0.5, 0.0).astype(BF16)
        out = _dot(hid, wo_ref[...])
        yp_ref[...] = (xp_ref[...] if first else yp_ref[...]) + out[:tm]
        if n_rows > tm:
            ys_ref[...] = (xs_ref[...] if first else ys_ref[...]) + out[tm:]
        if convert_next:
            nwi_bf_ref[...] = nwi_ref[...].astype(BF16)
            nwo_bf_ref[...] = nwo_ref[...].astype(BF16)

    @pl.when(f == 0)
    def _():
        normalise(xp_ref, slice(0, tm))

    @pl.when((f == 0) & (m == 0))
    def _():
        normalise(xs_ref, slice(tm, tm + ms))

    for first in (True, False):
        for first_tile, n_rows in ((True, tm + ms), (False, tm)):
            @pl.when(((f == 0) == first) & ((m == 0) == first_tile))
            def _(n_rows=n_rows, first=first):
                accumulate(n_rows, first)


def _ffn(xp, xs, g, w_in, w_out, nxt=None):
    mp, ms = xp.shape[0], xs.shape[0]
    n_m = mp // FFN_ROW_TILE
    elem = pl.Element
    in_specs = [
        pl.BlockSpec((FFN_ROW_TILE, D_MODEL), lambda m, f: (m, 0)),
        pl.BlockSpec((ms, D_MODEL), lambda m, f: (0, 0)),
        pl.BlockSpec((1, D_MODEL), lambda m, f: (0, 0)),
        pl.BlockSpec((elem(D_MODEL), elem(FF_TILE)), lambda m, f: (0, _ff_start(f))),
        pl.BlockSpec((elem(D_MODEL), elem(FF_TILE)), lambda m, f: (0, _ff_start(f, D_FF))),
        pl.BlockSpec((elem(FF_TILE), elem(D_MODEL)), lambda m, f: (_ff_start(f), 0)),
    ]
    out_specs = [
        pl.BlockSpec((FFN_ROW_TILE, D_MODEL), lambda m, f: (m, 0)),
        pl.BlockSpec((ms, D_MODEL), lambda m, f: (0, 0)),
    ]
    out_shape = [jax.ShapeDtypeStruct(xp.shape, F32), jax.ShapeDtypeStruct(xs.shape, F32)]
    args = [xp, xs, g.reshape(1, D_MODEL), w_in, w_in, w_out]
    if nxt is not None:
        nwi, nwo, layer = nxt
        n_in, n_out = D_MODEL // CONVERT_IN_ROWS, D_FF // CONVERT_OUT_ROWS
        assert max(n_in, n_out) <= n_m * N_FF_TILES and D_FF % CONVERT_OUT_ROWS == 0
        slab = lambda m, f, n: jnp.minimum(m * N_FF_TILES + f, n - 1)
        in_specs += [
            pl.BlockSpec((None, CONVERT_IN_ROWS, 2 * D_FF), lambda m, f: (layer, slab(m, f, n_in), 0)),
            pl.BlockSpec((None, CONVERT_OUT_ROWS, D_MODEL), lambda m, f: (layer, slab(m, f, n_out), 0)),
        ]
        out_specs += [
            pl.BlockSpec((CONVERT_IN_ROWS, 2 * D_FF), lambda m, f: (slab(m, f, n_in), 0)),
            pl.BlockSpec((CONVERT_OUT_ROWS, D_MODEL), lambda m, f: (slab(m, f, n_out), 0)),
        ]
        out_shape += [jax.ShapeDtypeStruct((D_MODEL, 2 * D_FF), BF16),
                      jax.ShapeDtypeStruct((D_FF, D_MODEL), BF16)]
        args += [nwi, nwo]
    return pl.pallas_call(
        functools.partial(_ffn_body, convert_next=nxt is not None),
        grid=(n_m, N_FF_TILES),
        in_specs=in_specs,
        out_specs=out_specs,
        out_shape=out_shape,
        scratch_shapes=[pltpu.VMEM((FFN_ROW_TILE + ms, D_MODEL), BF16)],
        compiler_params=_params(("arbitrary", "arbitrary"), 62),
        name="ffn",
    )(*args)


def _pool_body(x_ref, buf_ref, g_ref, pw_ref, sc_ref, y_ref, st_ref, hb_ref, sa_ref, sb_ref, *,
               tt, start, nt):
    t = pl.program_id(1)
    halo = POOL_HALO
    n = halo + tt

    @pl.when(t == 0)
    def _():
        hb_ref[0:halo, :] = buf_ref[0]

    x = x_ref[0]
    h = _rms(x, g_ref[...])
    hb_ref[halo:n, :] = h
    refs = (hb_ref, sa_ref, sb_ref, sa_ref, sb_ref)
    for k, w in enumerate(POOL_WINDOWS):
        src, dst = refs[k], refs[k + 1]
        lo, cols = SUBLANE * (k + 1), slice(k * POOL_GROUP, D_MODEL)
        dst[lo:n, cols] = src[lo:n, cols] + src[lo - w // 2:n - w // 2, cols]
    pos = start + t * tt + lax.broadcasted_iota(jnp.int32, (tt, 1), 0)
    for gi, w in enumerate(POOL_WINDOWS):
        sl = slice(gi * POOL_GROUP, (gi + 1) * POOL_GROUP)
        win = refs[gi + 1][halo:n, sl]
        count = jnp.minimum(w, pos + 1).astype(F32)
        pooled = win / count - h[:, sl]
        mixed = _dot(pooled.astype(BF16), pw_ref[gi])
        y_ref[0, :, sl] = x[:, sl] + mixed * sc_ref[:, sl]
    st_ref[0] = hb_ref[n - POOL_BUF:n, :]
    if nt > 1:
        hb_ref[0:halo, :] = hb_ref[tt:n, :]


def _pool(x, buf, g, pool_w, scale, start):
    b, t_len, _ = x.shape
    tt = min(t_len, ROW_TILE)
    nt = t_len // tt
    halo = POOL_HALO
    assert POOL_WINDOWS == tuple(2 ** (k + 1) for k in range(len(POOL_WINDOWS))) and halo > POOL_BUF
    buf16 = jnp.pad(buf, ((0, 0), (halo - POOL_BUF, 0), (0, 0)))
    body = functools.partial(_pool_body, tt=tt, start=start, nt=nt)
    return pl.pallas_call(
        body,
        grid=(b, nt),
        in_specs=[
            pl.BlockSpec((1, tt, D_MODEL), lambda i, t: (i, t, 0)),
            pl.BlockSpec((1, halo, D_MODEL), lambda i, t: (i, 0, 0)),
            pl.BlockSpec((1, D_MODEL), lambda i, t: (0, 0)),
            pl.BlockSpec((len(POOL_WINDOWS), POOL_GROUP, POOL_GROUP), lambda i, t: (0, 0, 0)),
            pl.BlockSpec((1, D_MODEL), lambda i, t: (0, 0)),
        ],
        out_specs=[
            pl.BlockSpec((1, tt, D_MODEL), lambda i, t: (i, t, 0)),
            pl.BlockSpec((1, POOL_BUF, D_MODEL), lambda i, t: (i, 0, 0)),
        ],
        out_shape=[jax.ShapeDtypeStruct(x.shape, F32),
                   jax.ShapeDtypeStruct((b, POOL_BUF, D_MODEL), F32)],
        scratch_shapes=[pltpu.VMEM((halo + tt, D_MODEL), F32)] * 3,
        compiler_params=_params(("arbitrary", "arbitrary"), 40),
        name="pool",
    )(x, buf16, g.reshape(1, D_MODEL), pool_w.astype(BF16), scale.reshape(1, D_MODEL))


def _chunk_body(x_ref, g_ref, win_ref, vg_ref, ws_ref, bst_ref, wout_ref, *rest, tm, emit_v):
    if emit_v:
        y_ref, vn_ref, mix_ref = rest
    else:
        y_ref, mix_ref = rest
    x = x_ref[...]
    h = _rms(x, g_ref[...]).astype(BF16)
    uv = _dot(h, win_ref[...])
    uv = 0.5 * uv * (1.0 + lax.erf(uv * np.float32(np.sqrt(0.5))))
    u = uv[:, :D_MODEL]
    vn = _rms(uv[:, D_MODEL:], vg_ref[...])
    if emit_v:
        vn_ref[...] = vn
    q_idx = lax.broadcasted_iota(jnp.int32, (CHUNK, CHUNK), 0)
    c_idx = lax.broadcasted_iota(jnp.int32, (CHUNK, CHUNK), 1)
    causal = c_idx <= q_idx
    for gi in range(CHUNK_GROUPS):
        cols = slice(gi * CHUNK_GROUP_W, (gi + 1) * CHUNK_GROUP_W)
        ws = jnp.where(causal, ws_ref[gi], 0.0).astype(BF16)
        bias = bst_ref[:, gi:gi + 1]
        for c in range(tm // CHUNK):
            rows = slice(c * CHUNK, (c + 1) * CHUNK)
            mixed = _dot(ws, vn[rows, cols].astype(BF16)) + bias
            mix_ref[rows, cols] = (u[rows, cols] * mixed).astype(BF16)
    y_ref[...] = x + _dot(mix_ref[...], wout_ref[...])


def _chunk(x, g, w_in, v_gain, w_s, b_s, w_out, emit_v):
    m = x.shape[0]
    tm = CHUNK_ROW_TILE
    body = functools.partial(_chunk_body, tm=tm, emit_v=emit_v)
    row_spec = pl.BlockSpec((tm, D_MODEL), lambda i: (i, 0))
    once = pl.Buffered(1)
    out_specs = [row_spec]
    out_shape = [jax.ShapeDtypeStruct((m, D_MODEL), F32)]
    if emit_v:
        out_specs.append(row_spec)
        out_shape.append(jax.ShapeDtypeStruct((m, D_MODEL), F32))
    return pl.pallas_call(
        body,
        grid=(m // tm,),
        in_specs=[
            row_spec,
            pl.BlockSpec((1, D_MODEL), lambda i: (0, 0)),
            pl.BlockSpec((D_MODEL, 2 * D_MODEL), lambda i: (0, 0), pipeline_mode=once),
            pl.BlockSpec((1, D_MODEL), lambda i: (0, 0)),
            pl.BlockSpec((CHUNK_GROUPS, CHUNK, CHUNK), lambda i: (0, 0, 0)),
            pl.BlockSpec((CHUNK, CHUNK_GROUPS), lambda i: (0, 0)),
            pl.BlockSpec((D_MODEL, D_MODEL), lambda i: (0, 0), pipeline_mode=once),
        ],
        out_specs=out_specs,
        out_shape=out_shape,
        scratch_shapes=[pltpu.VMEM((tm, D_MODEL), BF16)],
        compiler_params=_params(("arbitrary",), 56),
        name="chunk",
    )(x, g.reshape(1, D_MODEL), w_in.astype(BF16), v_gain.reshape(1, D_MODEL), w_s, b_s.T,
      w_out.astype(BF16))


def _tile_perm(dil):
    u = np.arange(ROW_TILE)
    n_k = ROW_TILE // dil
    return (u % n_k) * dil + u // n_k


def _rope_tables(pos):
    half = HEAD_DIM // 2
    freqs = ROPE_THETA ** (-2.0 * jnp.arange(half, dtype=F32) / HEAD_DIM)
    ang = pos.astype(F32)[:, None] * freqs[None, :]
    cos, sin = jnp.cos(ang), jnp.sin(ang)
    return jnp.concatenate([cos, cos], axis=-1), jnp.concatenate([-sin, sin], axis=-1)


def _rope(x, cos, sin_signed):
    return x * cos + pltpu.roll(x, HEAD_DIM // 2, 1) * sin_signed


def _norm_perm_body(x_ref, g_ref, *h_refs):
    h = _rms(x_ref[...], g_ref[...]).astype(BF16)
    u = lax.broadcasted_iota(jnp.int32, (ROW_TILE, ROW_TILE), 0)
    t = lax.broadcasted_iota(jnp.int32, (ROW_TILE, ROW_TILE), 1)
    for h_ref, (_, dil) in zip(h_refs, ATTN_GROUPS):
        if dil == 1:
            h_ref[...] = h
        else:
            n_k = ROW_TILE // dil
            src = (u & (n_k - 1)) * dil + lax.shift_right_logical(u, n_k.bit_length() - 1)
            select = jnp.where(t == src, 1.0, 0.0).astype(BF16)
            h_ref[...] = _dot(select, h).astype(BF16)


def _norm_perm(x, g):
    m = x.shape[0]
    row_spec = pl.BlockSpec((ROW_TILE, D_MODEL), lambda i: (i, 0))
    return pl.pallas_call(
        _norm_perm_body,
        grid=(m // ROW_TILE,),
        in_specs=[row_spec, pl.BlockSpec((1, D_MODEL), lambda i: (0, 0))],
        out_specs=[row_spec] * N_GROUPS,
        out_shape=[jax.ShapeDtypeStruct((m, D_MODEL), BF16)] * N_GROUPS,
        compiler_params=_params(("arbitrary",), 40),
        name="norm_perm",
    )(x, g.reshape(1, D_MODEL))


def _qkv_body(h_ref, w_ref, qn_ref, kn_ref, cos_ref, sin_ref, att_ref, kvf_ref,
              wbf_ref, ya_ref, yb_ref, slab_ref, *, dil, final_rows, tiles_per_seq, table_tiles, n_m,
              n_steps):
    p = pl.program_id(0)
    done = jnp.maximum(p - 1, 0)
    role = _div(_div(done, n_m), COL_TILES_PER_ROLE)
    row_tile = _mod(done, n_m)

    @pl.when(p == 0)
    def _():
        yb_ref[...] = jnp.zeros_like(yb_ref)

    @pl.when((_mod(jnp.minimum(p, n_steps - 1), n_m) == 0) & (p < n_steps))
    def _():
        wbf_ref[...] = w_ref[...].astype(BF16)

    def to_slab(hl, val):
        slab_ref[hl * SLAB_PITCH:hl * SLAB_PITCH + ROW_TILE, :] = val

    def gather_final():
        n_k = ROW_TILE // dil
        for t in range(final_rows):
            t_nat = (ROW_TILE - final_rows) + t
            u = (t_nat % dil) * n_k + t_nat // dil
            kvf_ref[t, 0] = slab_ref[pl.ds(u, HEADS_PER_STEP, stride=SLAB_PITCH), :]

    def emit_final():
        if tiles_per_seq == 1:
            gather_final()
        else:
            pl.when(_mod(row_tile, tiles_per_seq) == tiles_per_seq - 1)(gather_final)

    def finish(which, y_ref):
        rows = pl.ds(pl.multiple_of(_mod(row_tile, table_tiles) * ROW_TILE, ROW_TILE), ROW_TILE)
        for hl in range(HEADS_PER_STEP):
            yh = y_ref[:, hl * HEAD_DIM:(hl + 1) * HEAD_DIM]
            if which == 0:
                qn = _rope(_rms(yh, qn_ref[...]), cos_ref[rows, :], sin_ref[rows, :])
                att_ref[0, hl] = (qn * ATTN_SCALE).astype(BF16)
            elif which == 1:
                kn = _rope(_rms(yh, kn_ref[...]), cos_ref[rows, :], sin_ref[rows, :])
                att_ref[0, hl] = kn.astype(BF16)
                to_slab(hl, kn)
            else:
                att_ref[0, hl] = yh.astype(BF16)
                to_slab(hl, yh)
        if which > 0:
            emit_final()

    for parity, (cur_ref, prev_ref) in enumerate(((ya_ref, yb_ref), (yb_ref, ya_ref))):
        for which in range(3):
            @pl.when((_mod(p, 2) == parity) & (role == which))
            def _(which=which, cur_ref=cur_ref, prev_ref=prev_ref):
                cur_ref[...] = _dot(h_ref[...], wbf_ref[...])
                finish(which, prev_ref)


def _qkv(h, w_qkv, q_gain, k_gain, cos, sin, group, seq):
    window, dil = ATTN_GROUPS[group]
    m = h.shape[0]
    n_m = m // ROW_TILE
    tiles_per_seq = seq // ROW_TILE
    keep = min(window, seq)
    final_rows = min(keep, ROW_TILE)
    every_tile = keep == seq
    halves = COL_TILES_PER_ROLE
    n_steps = 3 * halves * n_m

    def project_tile(p):
        q = jnp.minimum(p, n_steps - 1)
        return _div(q, n_m), _mod(q, n_m)

    def finish_tile(p):
        e = jnp.maximum(p - 1, 0)
        return _div(e, n_m), _mod(e, n_m)

    def att_idx(p):
        c, i = finish_tile(p)
        return (_div(c, halves), _mod(c, halves), i, 0)

    def final_idx(p):
        c, i = finish_tile(p)
        s, hh = _div(c, halves), _mod(c, halves)
        rb = i if every_tile else _div(i, tiles_per_seq)
        live = s > 0
        return (jnp.where(live, rb, 0), jnp.where(live, s - 1, 0), jnp.where(live, hh, 0), 0)

    def w_idx(p):
        c, _ = project_tile(p)
        return (0, (_div(c, halves) * N_GROUPS + group) * halves + _mod(c, halves))

    body = functools.partial(_qkv_body, dil=dil, final_rows=final_rows, n_m=n_m, n_steps=n_steps,
                             tiles_per_seq=1 if every_tile else tiles_per_seq,
                             table_tiles=tiles_per_seq)
    att_shape = jax.ShapeDtypeStruct((3, HEADS, m, HEAD_DIM), BF16)
    att_block = (1, HEADS_PER_STEP, ROW_TILE, HEAD_DIM)
    table = pl.BlockSpec((seq, HEAD_DIM), lambda p: (0, 0))
    batch = m // seq
    return pl.pallas_call(
        body,
        grid=(n_steps + 1,),
        in_specs=[
            pl.BlockSpec((ROW_TILE, D_MODEL), lambda p: (project_tile(p)[1], 0)),
            pl.BlockSpec((D_MODEL, QKV_COL_TILE), w_idx),
            pl.BlockSpec((1, HEAD_DIM), lambda p: (0, 0)),
            pl.BlockSpec((1, HEAD_DIM), lambda p: (0, 0)),
            table,
            table,
        ],
        out_specs=[
            pl.BlockSpec(att_block, att_idx),
            pl.BlockSpec((final_rows, 1, HEADS_PER_STEP, HEAD_DIM), final_idx),
        ],
        out_shape=[att_shape, jax.ShapeDtypeStruct((batch * keep, 2, HEADS, HEAD_DIM), F32)],
        scratch_shapes=[pltpu.VMEM((D_MODEL, QKV_COL_TILE), BF16),
                        pltpu.VMEM((ROW_TILE, QKV_COL_TILE), F32),
                        pltpu.VMEM((ROW_TILE, QKV_COL_TILE), F32),
                        pltpu.VMEM((HEADS_PER_STEP * SLAB_PITCH, HEAD_DIM), F32)],
        compiler_params=_params(("arbitrary",), 48),
        name="qkv",
    )(h, w_qkv, q_gain.reshape(1, HEAD_DIM), k_gain.reshape(1, HEAD_DIM), cos, sin)


def _band_mask(n_keys):
    qi = lax.broadcasted_iota(jnp.int32, (ATTN_BLOCK, n_keys), 0)
    kj = lax.broadcasted_iota(jnp.int32, (ATTN_BLOCK, n_keys), 1)
    dist = (n_keys - ATTN_BLOCK) + qi - kj
    return (dist >= 0) & (dist <= ATTN_BLOCK)


def _attn_body(*refs, seq):
    qkv_refs, o_ref, og_ref, lg_ref = refs[:N_GROUPS], refs[N_GROUPS], refs[-2], refs[-1]
    masks = {n: _band_mask(n) for n in (ATTN_BLOCK, 2 * ATTN_BLOCK)}
    for gi, (_, dil) in enumerate(ATTN_GROUPS):
        qkv_ref = qkv_refs[gi]
        Q, K, V = 0, 1, 2
        n_k = ROW_TILE // dil
        n_blocks = seq // dil // ATTN_BLOCK

        def rows_of(role, r, blk):
            if n_k >= ATTN_BLOCK:
                per_tile = n_k // ATTN_BLOCK
                start = (blk // per_tile) * ROW_TILE + r * n_k + (blk % per_tile) * ATTN_BLOCK
                return qkv_ref[role, 0, start:start + ATTN_BLOCK, :]
            parts = [qkv_ref[role, 0, tl * ROW_TILE + r * n_k:tl * ROW_TILE + (r + 1) * n_k, :]
                     for tl in range(blk * (ATTN_BLOCK // n_k), (blk + 1) * (ATTN_BLOCK // n_k))]
            return jnp.concatenate(parts, axis=0)

        def keys_of(role, r, blk):
            if blk == 0:
                return rows_of(role, r, 0)
            return jnp.concatenate([rows_of(role, r, blk - 1), rows_of(role, r, blk)], axis=0)

        todo = [(r, blk) for r in range(dil) for blk in range(n_blocks)]
        for i in range(0, len(todo), ATTN_INTERLEAVE):
            batch = todo[i:i + ATTN_INTERLEAVE]
            ks = [keys_of(K, r, blk) for r, blk in batch]
            ss = [jnp.where(masks[k.shape[0]], _dot_nt(rows_of(Q, r, blk), k), NEG)
                  for (r, blk), k in zip(batch, ks)]
            mxs = [jnp.max(s, axis=-1, keepdims=True) for s in ss]
            ps = [jnp.exp(s - mx) for s, mx in zip(ss, mxs)]
            dens = [jnp.sum(p, axis=-1, keepdims=True) for p in ps]
            accs = [_dot(p.astype(BF16), keys_of(V, r, blk)) for p, (r, blk) in zip(ps, batch)]
            for (r, blk), acc, mx, den in zip(batch, accs, mxs, dens):
                if dil == 1:
                    dst = pl.ds(blk * ATTN_BLOCK, ATTN_BLOCK)
                else:
                    dst = pl.ds(blk * ATTN_BLOCK * dil + r, ATTN_BLOCK, stride=dil)
                og_ref[gi, dst, :] = acc / den
                lg_ref[gi, dst, :] = jnp.broadcast_to(mx + jnp.log(den), (ATTN_BLOCK, HEAD_DIM))
    for blk in range(seq // ATTN_BLOCK):
        rows = slice(blk * ATTN_BLOCK, (blk + 1) * ATTN_BLOCK)
        ls = [lg_ref[gi, rows, :] for gi in range(N_GROUPS)]
        mx = functools.reduce(jnp.maximum, ls)
        es = [jnp.exp(l - mx) for l in ls]
        num = sum(e * og_ref[gi, rows, :] for gi, e in enumerate(es))
        o_ref[rows, :] = (num / sum(es)).astype(BF16)


def _attn(qkvs, batch, seq):
    flat = list(qkvs)
    blk = pl.BlockSpec((3, 1, seq, HEAD_DIM), lambda b, hd: (0, hd, b, 0))
    return pl.pallas_call(
        functools.partial(_attn_body, seq=seq),
        grid=(batch, HEADS),
        in_specs=[blk] * len(flat),
        out_specs=pl.BlockSpec((seq, HEAD_DIM), lambda b, hd: (b, hd)),
        out_shape=jax.ShapeDtypeStruct((batch * seq, ATTN_W), BF16),
        scratch_shapes=[pltpu.VMEM((N_GROUPS, seq, HEAD_DIM), F32)] * 2,
        compiler_params=_params(("arbitrary", "arbitrary"), 40),
        name="attn",
    )(*flat)


def _out_proj_body(x_ref, o_ref, w_ref, y_ref, wbf_ref):
    @pl.when(pl.program_id(0) == 0)
    def _():
        wbf_ref[...] = w_ref[...].astype(BF16)

    y_ref[...] = x_ref[...] + _dot(o_ref[...], wbf_ref[...])


def _out_proj(x, o, w_out):
    m = x.shape[0]
    tm = min(m, ROW_TILE)
    return pl.pallas_call(
        _out_proj_body,
        grid=(m // tm,),
        in_specs=[pl.BlockSpec((tm, D_MODEL), lambda i: (i, 0)),
                  pl.BlockSpec((tm, ATTN_W), lambda i: (i, 0)),
                  pl.BlockSpec((ATTN_W, D_MODEL), lambda i: (0, 0), pipeline_mode=pl.Buffered(1))],
        out_specs=pl.BlockSpec((tm, D_MODEL), lambda i: (i, 0)),
        out_shape=jax.ShapeDtypeStruct((m, D_MODEL), F32),
        scratch_shapes=[pltpu.VMEM((ATTN_W, D_MODEL), BF16)],
        compiler_params=_params(("arbitrary",), 48),
        name="attn_out_proj",
    )(x, o, w_out)


def _qkv_rows_body(x_ref, g_ref, w_ref, y_ref):
    h = _rms(x_ref[...], g_ref[...]).astype(BF16)
    y_ref[...] = _dot(h, w_ref[...].astype(BF16))


def _qkv_rows(x, g, w_qkv):
    m = x.shape[0]
    n = w_qkv.shape[1]
    return pl.pallas_call(
        _qkv_rows_body,
        grid=(n // QKV_COL_TILE,),
        in_specs=[pl.BlockSpec((m, D_MODEL), lambda c: (0, 0)),
                  pl.BlockSpec((1, D_MODEL), lambda c: (0, 0)),
                  pl.BlockSpec((D_MODEL, QKV_COL_TILE), lambda c: (0, c))],
        out_specs=pl.BlockSpec((m, QKV_COL_TILE), lambda c: (0, c)),
        out_shape=jax.ShapeDtypeStruct((m, n), F32),
        compiler_params=_params(("arbitrary",), 32),
        name="qkv_rows",
    )(x, g.reshape(1, D_MODEL), w_qkv)


def _cache_attn_body(y_ref, c0_ref, c1_ref, c2_ref, qn_ref, kn_ref, cos_ref, sin_ref,
                     o_ref, n0_ref, n1_ref, n2_ref, *, t_len):
    cache_refs = (c0_ref, c1_ref, c2_ref)
    new_refs = (n0_ref, n1_ref, n2_ref)

    def norm_rope(val, gain_ref, t):
        return _rope(_rms(val, gain_ref[...]), cos_ref[t:t + 1, :], sin_ref[t:t + 1, :])

    k_new = [[norm_rope(y_ref[0, t, 1, gi], kn_ref, t) for t in range(t_len)]
             for gi in range(N_GROUPS)]
    v_new = [[y_ref[0, t, 2, gi] for t in range(t_len)] for gi in range(N_GROUPS)]
    for gi in range(N_GROUPS):
        for t in range(t_len):
            new_refs[gi][0, t, 0] = k_new[gi][t]
            new_refs[gi][0, t, 1] = v_new[gi][t]

    cached_pos = lax.broadcasted_iota(jnp.int32, (ATTN_BLOCK, 1, 1), 0)
    for t in range(t_len):
        outs, lses = [], []
        for gi, (_, dil) in enumerate(ATTN_GROUPS):
            c_ref = cache_refs[gi]
            q = norm_rope(y_ref[0, t, 0, gi], qn_ref, t) * ATTN_SCALE
            if dil == 1:
                kc, vc = c_ref[0, :, 0], c_ref[0, :, 1]
                new_rows = range(t + 1)
            else:
                kc, vc = c_ref[0, :, t, 0], c_ref[0, :, t, 1]
                new_rows = [t]
            s_c = jnp.sum(kc * q[None], axis=-1, keepdims=True)
            if dil == 1:
                s_c = jnp.where(cached_pos >= t, s_c, NEG)
            s_n = [jnp.sum(k_new[gi][tt] * q, axis=-1, keepdims=True) for tt in new_rows]
            mx = functools.reduce(jnp.maximum, s_n, jnp.max(s_c, axis=0))
            p_c = jnp.exp(s_c - mx[None])
            p_n = [jnp.exp(sn - mx) for sn in s_n]
            den = jnp.sum(p_c, axis=0) + sum(p_n)
            acc = jnp.sum(p_c * vc, axis=0) + sum(p * v_new[gi][tt] for p, tt in zip(p_n, new_rows))
            outs.append(acc / den)
            lses.append(mx + jnp.log(den))
        lmax = functools.reduce(jnp.maximum, lses)
        es = [jnp.exp(l - lmax) for l in lses]
        o_ref[0, t] = sum(e * o for e, o in zip(es, outs)) / sum(es)


def _cache_attn(y, caches, q_gain, k_gain, cos, sin, batch, t_len):
    y6 = y.reshape(batch, t_len, 3, N_GROUPS, HEADS, HEAD_DIM)
    views, specs = [], []
    for cache, (window, dil) in zip(caches, ATTN_GROUPS):
        assert cache.shape[1] == window and window // dil == ATTN_BLOCK and (dil == 1 or t_len <= dil)
        if dil == 1:
            views.append(cache)
            specs.append(pl.BlockSpec((1, ATTN_BLOCK, 2, HEADS, HEAD_DIM), lambda b: (b, 0, 0, 0, 0)))
        else:
            views.append(cache.reshape(batch, ATTN_BLOCK, dil, 2, HEADS, HEAD_DIM))
            specs.append(pl.BlockSpec((1, ATTN_BLOCK, t_len, 2, HEADS, HEAD_DIM),
                                      lambda b: (b, 0, 0, 0, 0, 0)))
    new_shape = jax.ShapeDtypeStruct((batch, t_len, 2, HEADS, HEAD_DIM), F32)
    new_spec = pl.BlockSpec((1, t_len, 2, HEADS, HEAD_DIM), lambda b: (b, 0, 0, 0, 0))
    vec = pl.BlockSpec((1, HEAD_DIM), lambda b: (0, 0))
    tab = pl.BlockSpec((t_len, HEAD_DIM), lambda b: (0, 0))
    return pl.pallas_call(
        functools.partial(_cache_attn_body, t_len=t_len),
        grid=(batch,),
        in_specs=[pl.BlockSpec((1, t_len, 3, N_GROUPS, HEADS, HEAD_DIM), lambda b: (b, 0, 0, 0, 0, 0))]
        + specs + [vec, vec, tab, tab],
        out_specs=[pl.BlockSpec((1, t_len, HEADS, HEAD_DIM), lambda b: (b, 0, 0, 0))] + [new_spec] * N_GROUPS,
        out_shape=[jax.ShapeDtypeStruct((batch, t_len, HEADS, HEAD_DIM), F32)] + [new_shape] * N_GROUPS,
        compiler_params=_params(("arbitrary",), 56),
        name="cache_attn",
    )(y6, *views, q_gain.reshape(1, HEAD_DIM), k_gain.reshape(1, HEAD_DIM), cos, sin)


def kernel(x_prompt, x_sample, state_pool, cache_kv_g0, cache_kv_g1, cache_kv_g2, norm_ffn1, ffn1_w_in, ffn1_w_out, norm_mix, norm_ffn2, ffn2_w_in, ffn2_w_out, pool_w, pool_scale, chunk_w_in, chunk_v_norm, chunk_w_s, chunk_b_s, chunk_w_out, attn_w_qkv, attn_q_norm, attn_k_norm, attn_w_out):
    caches = (cache_kv_g0, cache_kv_g1, cache_kv_g2)
    batch, seq, _ = x_prompt.shape
    dec_batch, dec_seq, _ = x_sample.shape
    depth = norm_ffn1.shape[0]
    xp = x_prompt.reshape(batch * seq, D_MODEL)
    xs = x_sample.reshape(dec_batch * dec_seq, D_MODEL)
    pool_p, pool_s, chunk_s = [], [], []
    kv_p = [[] for _ in ATTN_GROUPS]
    kv_s = [[] for _ in ATTN_GROUPS]
    half_steps = [(w_in, w_out, i) for i in range(depth)
                  for w_in, w_out in ((ffn1_w_in, ffn1_w_out), (ffn2_w_in, ffn2_w_out))]
    w_bf = (ffn1_w_in[0].astype(BF16), ffn1_w_out[0].astype(BF16))

    def ffn(xp, xs, g, w_bf, step):
        nxt = half_steps[step + 1] if step + 1 < len(half_steps) else None
        out = _ffn(xp, xs, g, *w_bf, nxt=nxt)
        return out[0], out[1], tuple(out[2:])

    for i in range(depth):
        kind, j = i % 3, i // 3
        xp, xs, w_bf = ffn(xp, xs, norm_ffn1[i], w_bf, 2 * i)
        if kind == 0:
            zero_buf = jnp.zeros((batch, POOL_BUF, D_MODEL), F32)
            yp, st_p = _pool(xp.reshape(batch, seq, D_MODEL), zero_buf, norm_mix[i], pool_w[j],
                             pool_scale[j], 0)
            ys, st_s = _pool(xs.reshape(dec_batch, dec_seq, D_MODEL), state_pool[j], norm_mix[i],
                             pool_w[j], pool_scale[j], PAST_LEN)
            xp = yp.reshape(batch * seq, D_MODEL)
            xs = ys.reshape(dec_batch * dec_seq, D_MODEL)
            pool_p.append(st_p)
            pool_s.append(st_s)
        elif kind == 1:
            args = (norm_mix[i], chunk_w_in[j], chunk_v_norm[j], chunk_w_s[j], chunk_b_s[j],
                    chunk_w_out[j])
            (xp,) = _chunk(xp, *args, emit_v=False)
            xs_pad = jnp.pad(xs.reshape(dec_batch, dec_seq, D_MODEL),
                             ((0, 0), (0, CHUNK - dec_seq), (0, 0)))
            ys_pad, v_pad = _chunk(xs_pad.reshape(dec_batch * CHUNK, D_MODEL), *args, emit_v=True)
            xs = ys_pad.reshape(dec_batch, CHUNK, D_MODEL)[:, :dec_seq].reshape(-1, D_MODEL)
            chunk_s.append(v_pad.reshape(dec_batch, CHUNK, D_MODEL)[:, :dec_seq])
        else:
            hs = _norm_perm(xp, norm_mix[i])
            qkvs = []
            for gi, (window, dil) in enumerate(ATTN_GROUPS):
                pos = (np.arange(seq // ROW_TILE)[:, None] * ROW_TILE + _tile_perm(dil)[None, :]).reshape(-1)
                cos_p, sin_p = _rope_tables(jnp.asarray(pos))
                qkv, kv_final = _qkv(hs[gi], attn_w_qkv[j], attn_q_norm[j], attn_k_norm[j],
                                     cos_p, sin_p, gi, seq)
                qkvs.append(qkv)
                kv_p[gi].append(kv_final.reshape(batch, min(window, seq), 2, HEADS, HEAD_DIM))
            xp = _out_proj(xp, _attn(qkvs, batch, seq), attn_w_out[j])

            cos_s, sin_s = _rope_tables(PAST_LEN + jnp.arange(dec_seq))
            y_s = _qkv_rows(xs, norm_mix[i], attn_w_qkv[j])
            o_s, *new_rows = _cache_attn(y_s, tuple(c[j] for c in caches), attn_q_norm[j],
                                         attn_k_norm[j], cos_s, sin_s, dec_batch, dec_seq)
            for gi in range(N_GROUPS):
                kv_s[gi].append(new_rows[gi])
            xs = _out_proj(xs, o_s.reshape(dec_batch * dec_seq, ATTN_W).astype(BF16), attn_w_out[j])
        xp, xs, w_bf = ffn(xp, xs, norm_ffn2[i], w_bf, 2 * i + 1)
    return (xp.reshape(batch, seq, D_MODEL), xs.reshape(dec_batch, dec_seq, D_MODEL),
            jnp.stack(pool_p), jnp.stack(pool_s), jnp.stack(chunk_s),
            jnp.stack(kv_p[0]), jnp.stack(kv_s[0]), jnp.stack(kv_p[1]), jnp.stack(kv_s[1]),
            jnp.stack(kv_p[2]), jnp.stack(kv_s[2]))
```

```python
import functools

import numpy as np
import jax
import jax.numpy as jnp
from jax import lax
from jax.experimental import pallas as pl
from jax.experimental.pallas import tpu as pltpu

F32 = jnp.float32
BF16 = jnp.bfloat16

D_MODEL = 2048
D_FF = 5504
RMS_EPS = 1e-6
POOL_WINDOWS = (2, 4, 8, 16)
POOL_GROUP = D_MODEL // len(POOL_WINDOWS)
POOL_BUF = max(POOL_WINDOWS) - 1
CHUNK = 128
CHUNK_GROUPS = 8
CHUNK_GROUP_W = D_MODEL // CHUNK_GROUPS
ATTN_GROUPS = ((128, 1), (512, 4), (2048, 16))
N_GROUPS = len(ATTN_GROUPS)
HEADS = 16
HEAD_DIM = 128
ATTN_W = HEADS * HEAD_DIM
ROPE_THETA = 10000.0
ATTN_SCALE = HEAD_DIM ** -0.5
NEG = float(np.finfo(np.float32).min)
PAST_LEN = 16384

LANE = 128
SUBLANE = 8
FF_TILE = 512
N_FF_TILES = -(-D_FF // FF_TILE)
FFN_ROW_TILE = 1024
CONVERT_IN_ROWS = 32
CONVERT_OUT_ROWS = 64
ROW_TILE = 512
CHUNK_ROW_TILE = 256
POOL_HALO = SUBLANE * len(POOL_WINDOWS)
ATTN_BLOCK = 128
ATTN_INTERLEAVE = 8
QKV_COL_TILE = 1024
HEADS_PER_STEP = QKV_COL_TILE // HEAD_DIM
COL_TILES_PER_ROLE = ATTN_W // QKV_COL_TILE
SLAB_PITCH = ROW_TILE + SUBLANE
MIB = 1024 * 1024


def _params(semantics, vmem_mib):
    return pltpu.CompilerParams(dimension_semantics=semantics, vmem_limit_bytes=vmem_mib * MIB)


def _rms(x, g):
    ms = jnp.mean(x * x, axis=-1, keepdims=True)
    return x * lax.rsqrt(ms + RMS_EPS) * g


def _dot(a, b):
    return jnp.dot(a, b, preferred_element_type=F32)


def _div(x, d):
    return lax.shift_right_logical(x, d.bit_length() - 1) if d & (d - 1) == 0 else x // d


def _mod(x, d):
    return x & (d - 1) if d & (d - 1) == 0 else x % d


def _dot_nt(a, b):
    return lax.dot_general(a, b, (((1,), (1,)), ((), ())), preferred_element_type=F32)


def _ff_start(f, base=0):
    return (base // LANE + jnp.minimum(f * (FF_TILE // LANE), (D_FF - FF_TILE) // LANE)) * LANE


def _ffn_body(xp_ref, xs_ref, g_ref, wg_ref, wu_ref, wo_ref, *rest, convert_next):
    if convert_next:
        nwi_ref, nwo_ref, yp_ref, ys_ref, nwi_bf_ref, nwo_bf_ref, h_ref = rest
    else:
        yp_ref, ys_ref, h_ref = rest
    m = pl.program_id(0)
    f = pl.program_id(1)
    tm, ms = xp_ref.shape[0], xs_ref.shape[0]

    def init(x_ref, rows, y_ref):
        x = x_ref[...]
        h_ref[rows, :] = _rms(x, g_ref[...]).astype(BF16)
        y_ref[...] = x

    col = lax.broadcasted_iota(jnp.int32, (1, FF_TILE), 1)
    fresh = col >= f * FF_TILE - _ff_start(f)

    def accumulate(n_rows):
        h = h_ref[:n_rows]
        gate = _dot(h, wg_ref[...])
        up = _dot(h, wu_ref[...])
        hid = jnp.where(fresh, gate * jax.nn.sigmoid(gate) * up * 0.5, 0.0).astype(BF16)
        out = _dot(hid, wo_ref[...])
        yp_ref[...] += out[:tm]
        if n_rows > tm:
            ys_ref[...] += out[tm:]
        if convert_next:
            nwi_bf_ref[...] = nwi_ref[...].astype(BF16)
            nwo_bf_ref[...] = nwo_ref[...].astype(BF16)

    @pl.when(f == 0)
    def _():
        init(xp_ref, slice(0, tm), yp_ref)

    @pl.when((f == 0) & (m == 0))
    def _():
        init(xs_ref, slice(tm, tm + ms), ys_ref)

    @pl.when(m == 0)
    def _():
        accumulate(tm + ms)

    @pl.when(m != 0)
    def _():
        accumulate(tm)


def _ffn(xp, xs, g, w_in, w_out, nxt=None):
    mp, ms = xp.shape[0], xs.shape[0]
    n_m = mp // FFN_ROW_TILE
    elem = pl.Element
    in_specs = [
        pl.BlockSpec((FFN_ROW_TILE, D_MODEL), lambda m, f: (m, 0)),
        pl.BlockSpec((ms, D_MODEL), lambda m, f: (0, 0)),
        pl.BlockSpec((1, D_MODEL), lambda m, f: (0, 0)),
        pl.BlockSpec((elem(D_MODEL), elem(FF_TILE)), lambda m, f: (0, _ff_start(f))),
        pl.BlockSpec((elem(D_MODEL), elem(FF_TILE)), lambda m, f: (0, _ff_start(f, D_FF))),
        pl.BlockSpec((elem(FF_TILE), elem(D_MODEL)), lambda m, f: (_ff_start(f), 0)),
    ]
    out_specs = [
        pl.BlockSpec((FFN_ROW_TILE, D_MODEL), lambda m, f: (m, 0)),
        pl.BlockSpec((ms, D_MODEL), lambda m, f: (0, 0)),
    ]
    out_shape = [jax.ShapeDtypeStruct(xp.shape, F32), jax.ShapeDtypeStruct(xs.shape, F32)]
    args = [xp, xs, g.reshape(1, D_MODEL), w_in, w_in, w_out]
    if nxt is not None:
        nwi, nwo, layer = nxt
        n_in, n_out = D_MODEL // CONVERT_IN_ROWS, D_FF // CONVERT_OUT_ROWS
        assert max(n_in, n_out) <= n_m * N_FF_TILES and D_FF % CONVERT_OUT_ROWS == 0
        slab = lambda m, f, n: jnp.minimum(m * N_FF_TILES + f, n - 1)
        in_specs += [
            pl.BlockSpec((None, CONVERT_IN_ROWS, 2 * D_FF), lambda m, f: (layer, slab(m, f, n_in), 0)),
            pl.BlockSpec((None, CONVERT_OUT_ROWS, D_MODEL), lambda m, f: (layer, slab(m, f, n_out), 0)),
        ]
        out_specs += [
            pl.BlockSpec((CONVERT_IN_ROWS, 2 * D_FF), lambda m, f: (slab(m, f, n_in), 0)),
            pl.BlockSpec((CONVERT_OUT_ROWS, D_MODEL), lambda m, f: (slab(m, f, n_out), 0)),
        ]
        out_shape += [jax.ShapeDtypeStruct((D_MODEL, 2 * D_FF), BF16),
                      jax.ShapeDtypeStruct((D_FF, D_MODEL), BF16)]
        args += [nwi, nwo]
    return pl.pallas_call(
        functools.partial(_ffn_body, convert_next=nxt is not None),
        grid=(n_m, N_FF_TILES),
        in_specs=in_specs,
        out_specs=out_specs,
        out_shape=out_shape,
        scratch_shapes=[pltpu.VMEM((FFN_ROW_TILE + ms, D_MODEL), BF16)],
        compiler_params=_params(("arbitrary", "arbitrary"), 62),
        name="ffn",
    )(*args)


def _pool_body(x_ref, buf_ref, g_ref, pw_ref, sc_ref, y_ref, st_ref, hb_ref, sa_ref, sb_ref, *,
               tt, start, nt):
    t = pl.program_id(1)
    halo = POOL_HALO
    n = halo + tt

    @pl.when(t == 0)
    def _():
        hb_ref[0:halo, :] = buf_ref[0]

    x = x_ref[0]
    h = _rms(x, g_ref[...])
    hb_ref[halo:n, :] = h
    refs = (hb_ref, sa_ref, sb_ref, sa_ref, sb_ref)
    for k, w in enumerate(POOL_WINDOWS):
        src, dst = refs[k], refs[k + 1]
        lo, cols = SUBLANE * (k + 1), slice(k * POOL_GROUP, D_MODEL)
        dst[lo:n, cols] = src[lo:n, cols] + src[lo - w // 2:n - w // 2, cols]
    pos = start + t * tt + lax.broadcasted_iota(jnp.int32, (tt, 1), 0)
    for gi, w in enumerate(POOL_WINDOWS):
        sl = slice(gi * POOL_GROUP, (gi + 1) * POOL_GROUP)
        win = refs[gi + 1][halo:n, sl]
        count = jnp.minimum(w, pos + 1).astype(F32)
        pooled = win / count - h[:, sl]
        mixed = _dot(pooled.astype(BF16), pw_ref[gi])
        y_ref[0, :, sl] = x[:, sl] + mixed * sc_ref[:, sl]
    st_ref[0] = hb_ref[n - POOL_BUF:n, :]
    if nt > 1:
        hb_ref[0:halo, :] = hb_ref[tt:n, :]


def _pool(x, buf, g, pool_w, scale, start):
    b, t_len, _ = x.shape
    tt = min(t_len, ROW_TILE)
    nt = t_len // tt
    halo = POOL_HALO
    assert POOL_WINDOWS == tuple(2 ** (k + 1) for k in range(len(POOL_WINDOWS))) and halo > POOL_BUF
    buf16 = jnp.pad(buf, ((0, 0), (halo - POOL_BUF, 0), (0, 0)))
    body = functools.partial(_pool_body, tt=tt, start=start, nt=nt)
    return pl.pallas_call(
        body,
        grid=(b, nt),
        in_specs=[
            pl.BlockSpec((1, tt, D_MODEL), lambda i, t: (i, t, 0)),
            pl.BlockSpec((1, halo, D_MODEL), lambda i, t: (i, 0, 0)),
            pl.BlockSpec((1, D_MODEL), lambda i, t: (0, 0)),
            pl.BlockSpec((len(POOL_WINDOWS), POOL_GROUP, POOL_GROUP), lambda i, t: (0, 0, 0)),
            pl.BlockSpec((1, D_MODEL), lambda i, t: (0, 0)),
        ],
        out_specs=[
            pl.BlockSpec((1, tt, D_MODEL), lambda i, t: (i, t, 0)),
            pl.BlockSpec((1, POOL_BUF, D_MODEL), lambda i, t: (i, 0, 0)),
        ],
        out_shape=[jax.ShapeDtypeStruct(x.shape, F32),
                   jax.ShapeDtypeStruct((b, POOL_BUF, D_MODEL), F32)],
        scratch_shapes=[pltpu.VMEM((halo + tt, D_MODEL), F32)] * 3,
        compiler_params=_params(("arbitrary", "arbitrary"), 40),
        name="pool",
    )(x, buf16, g.reshape(1, D_MODEL), pool_w.astype(BF16), scale.reshape(1, D_MODEL))


def _chunk_body(x_ref, g_ref, win_ref, vg_ref, ws_ref, bst_ref, wout_ref, *rest, tm, emit_v):
    if emit_v:
        y_ref, vn_ref, mix_ref = rest
    else:
        y_ref, mix_ref = rest
    x = x_ref[...]
    h = _rms(x, g_ref[...]).astype(BF16)
    uv = _dot(h, win_ref[...])
    uv = 0.5 * uv * (1.0 + lax.erf(uv * np.float32(np.sqrt(0.5))))
    u = uv[:, :D_MODEL]
    vn = _rms(uv[:, D_MODEL:], vg_ref[...])
    if emit_v:
        vn_ref[...] = vn
    q_idx = lax.broadcasted_iota(jnp.int32, (CHUNK, CHUNK), 0)
    c_idx = lax.broadcasted_iota(jnp.int32, (CHUNK, CHUNK), 1)
    causal = c_idx <= q_idx
    for gi in range(CHUNK_GROUPS):
        cols = slice(gi * CHUNK_GROUP_W, (gi + 1) * CHUNK_GROUP_W)
        ws = jnp.where(causal, ws_ref[gi], 0.0).astype(BF16)
        bias = bst_ref[:, gi:gi + 1]
        for c in range(tm // CHUNK):
            rows = slice(c * CHUNK, (c + 1) * CHUNK)
            mixed = _dot(ws, vn[rows, cols].astype(BF16)) + bias
            mix_ref[rows, cols] = (u[rows, cols] * mixed).astype(BF16)
    y_ref[...] = x + _dot(mix_ref[...], wout_ref[...])


def _chunk(x, g, w_in, v_gain, w_s, b_s, w_out, emit_v):
    m = x.shape[0]
    tm = CHUNK_ROW_TILE
    body = functools.partial(_chunk_body, tm=tm, emit_v=emit_v)
    row_spec = pl.BlockSpec((tm, D_MODEL), lambda i: (i, 0))
    once = pl.Buffered(1)
    out_specs = [row_spec]
    out_shape = [jax.ShapeDtypeStruct((m, D_MODEL), F32)]
    if emit_v:
        out_specs.append(row_spec)
        out_shape.append(jax.ShapeDtypeStruct((m, D_MODEL), F32))
    return pl.pallas_call(
        body,
        grid=(m // tm,),
        in_specs=[
            row_spec,
            pl.BlockSpec((1, D_MODEL), lambda i: (0, 0)),
            pl.BlockSpec((D_MODEL, 2 * D_MODEL), lambda i: (0, 0), pipeline_mode=once),
            pl.BlockSpec((1, D_MODEL), lambda i: (0, 0)),
            pl.BlockSpec((CHUNK_GROUPS, CHUNK, CHUNK), lambda i: (0, 0, 0)),
            pl.BlockSpec((CHUNK, CHUNK_GROUPS), lambda i: (0, 0)),
            pl.BlockSpec((D_MODEL, D_MODEL), lambda i: (0, 0), pipeline_mode=once),
        ],
        out_specs=out_specs,
        out_shape=out_shape,
        scratch_shapes=[pltpu.VMEM((tm, D_MODEL), BF16)],
        compiler_params=_params(("arbitrary",), 56),
        name="chunk",
    )(x, g.reshape(1, D_MODEL), w_in.astype(BF16), v_gain.reshape(1, D_MODEL), w_s, b_s.T,
      w_out.astype(BF16))


def _tile_perm(dil):
    u = np.arange(ROW_TILE)
    n_k = ROW_TILE // dil
    return (u % n_k) * dil + u // n_k


def _rope_tables(pos):
    half = HEAD_DIM // 2
    freqs = ROPE_THETA ** (-2.0 * jnp.arange(half, dtype=F32) / HEAD_DIM)
    ang = pos.astype(F32)[:, None] * freqs[None, :]
    cos, sin = jnp.cos(ang), jnp.sin(ang)
    return jnp.concatenate([cos, cos], axis=-1), jnp.concatenate([-sin, sin], axis=-1)


def _rope(x, cos, sin_signed):
    return x * cos + pltpu.roll(x, HEAD_DIM // 2, 1) * sin_signed


def _norm_perm_body(x_ref, g_ref, *h_refs):
    h = _rms(x_ref[...], g_ref[...]).astype(BF16)
    u = lax.broadcasted_iota(jnp.int32, (ROW_TILE, ROW_TILE), 0)
    t = lax.broadcasted_iota(jnp.int32, (ROW_TILE, ROW_TILE), 1)
    for h_ref, (_, dil) in zip(h_refs, ATTN_GROUPS):
        if dil == 1:
            h_ref[...] = h
        else:
            n_k = ROW_TILE // dil
            src = (u & (n_k - 1)) * dil + lax.shift_right_logical(u, n_k.bit_length() - 1)
            select = jnp.where(t == src, 1.0, 0.0).astype(BF16)
            h_ref[...] = _dot(select, h).astype(BF16)


def _norm_perm(x, g):
    m = x.shape[0]
    row_spec = pl.BlockSpec((ROW_TILE, D_MODEL), lambda i: (i, 0))
    return pl.pallas_call(
        _norm_perm_body,
        grid=(m // ROW_TILE,),
        in_specs=[row_spec, pl.BlockSpec((1, D_MODEL), lambda i: (0, 0))],
        out_specs=[row_spec] * N_GROUPS,
        out_shape=[jax.ShapeDtypeStruct((m, D_MODEL), BF16)] * N_GROUPS,
        compiler_params=_params(("arbitrary",), 40),
        name="norm_perm",
    )(x, g.reshape(1, D_MODEL))


def _qkv_body(h_ref, w_ref, qn_ref, kn_ref, cos_ref, sin_ref, att_ref, kvf_ref,
              wbf_ref, ya_ref, yb_ref, slab_ref, *, dil, final_rows, tiles_per_seq, table_tiles, n_m,
              n_steps):
    p = pl.program_id(0)
    done = jnp.maximum(p - 1, 0)
    role = _div(_div(done, n_m), COL_TILES_PER_ROLE)
    row_tile = _mod(done, n_m)

    @pl.when(p == 0)
    def _():
        yb_ref[...] = jnp.zeros_like(yb_ref)

    @pl.when((_mod(jnp.minimum(p, n_steps - 1), n_m) == 0) & (p < n_steps))
    def _():
        wbf_ref[...] = w_ref[...].astype(BF16)

    def to_slab(hl, val):
        slab_ref[hl * SLAB_PITCH:hl * SLAB_PITCH + ROW_TILE, :] = val

    def gather_final():
        n_k = ROW_TILE // dil
        for t in range(final_rows):
            t_nat = (ROW_TILE - final_rows) + t
            u = (t_nat % dil) * n_k + t_nat // dil
            kvf_ref[t, 0] = slab_ref[pl.ds(u, HEADS_PER_STEP, stride=SLAB_PITCH), :]

    def emit_final():
        if tiles_per_seq == 1:
            gather_final()
        else:
            pl.when(_mod(row_tile, tiles_per_seq) == tiles_per_seq - 1)(gather_final)

    def finish(which, y_ref):
        rows = pl.ds(pl.multiple_of(_mod(row_tile, table_tiles) * ROW_TILE, ROW_TILE), ROW_TILE)
        for hl in range(HEADS_PER_STEP):
            yh = y_ref[:, hl * HEAD_DIM:(hl + 1) * HEAD_DIM]
            if which == 0:
                qn = _rope(_rms(yh, qn_ref[...]), cos_ref[rows, :], sin_ref[rows, :])
                att_ref[0, hl] = (qn * ATTN_SCALE).astype(BF16)
            elif which == 1:
                kn = _rope(_rms(yh, kn_ref[...]), cos_ref[rows, :], sin_ref[rows, :])
                att_ref[0, hl] = kn.astype(BF16)
                to_slab(hl, kn)
            else:
                att_ref[0, hl] = yh.astype(BF16)
                to_slab(hl, yh)
        if which > 0:
            emit_final()

    for parity, (cur_ref, prev_ref) in enumerate(((ya_ref, yb_ref), (yb_ref, ya_ref))):
        for which in range(3):
            @pl.when((_mod(p, 2) == parity) & (role == which))
            def _(which=which, cur_ref=cur_ref, prev_ref=prev_ref):
                cur_ref[...] = _dot(h_ref[...], wbf_ref[...])
                finish(which, prev_ref)


def _qkv(h, w_qkv, q_gain, k_gain, cos, sin, group, seq):
    window, dil = ATTN_GROUPS[group]
    m = h.shape[0]
    n_m = m // ROW_TILE
    tiles_per_seq = seq // ROW_TILE
    keep = min(window, seq)
    final_rows = min(keep, ROW_TILE)
    every_tile = keep == seq
    halves = COL_TILES_PER_ROLE
    n_steps = 3 * halves * n_m

    def project_tile(p):
        q = jnp.minimum(p, n_steps - 1)
        return _div(q, n_m), _mod(q, n_m)

    def finish_tile(p):
        e = jnp.maximum(p - 1, 0)
        return _div(e, n_m), _mod(e, n_m)

    def att_idx(p):
        c, i = finish_tile(p)
        return (_div(c, halves), _mod(c, halves), i, 0)

    def final_idx(p):
        c, i = finish_tile(p)
        s, hh = _div(c, halves), _mod(c, halves)
        rb = i if every_tile else _div(i, tiles_per_seq)
        live = s > 0
        return (jnp.where(live, rb, 0), jnp.where(live, s - 1, 0), jnp.where(live, hh, 0), 0)

    def w_idx(p):
        c, _ = project_tile(p)
        return (0, (_div(c, halves) * N_GROUPS + group) * halves + _mod(c, halves))

    body = functools.partial(_qkv_body, dil=dil, final_rows=final_rows, n_m=n_m, n_steps=n_steps,
                             tiles_per_seq=1 if every_tile else tiles_per_seq,
                             table_tiles=tiles_per_seq)
    att_shape = jax.ShapeDtypeStruct((3, HEADS, m, HEAD_DIM), BF16)
    att_block = (1, HEADS_PER_STEP, ROW_TILE, HEAD_DIM)
    table = pl.BlockSpec((seq, HEAD_DIM), lambda p: (0, 0))
    batch = m // seq
    return pl.pallas_call(
        body,
        grid=(n_steps + 1,),
        in_specs=[
            pl.BlockSpec((ROW_TILE, D_MODEL), lambda p: (project_tile(p)[1], 0)),
            pl.BlockSpec((D_MODEL, QKV_COL_TILE), w_idx),
            pl.BlockSpec((1, HEAD_DIM), lambda p: (0, 0)),
            pl.BlockSpec((1, HEAD_DIM), lambda p: (0, 0)),
            table,
            table,
        ],
        out_specs=[
            pl.BlockSpec(att_block, att_idx),
            pl.BlockSpec((final_rows, 1, HEADS_PER_STEP, HEAD_DIM), final_idx),
        ],
        out_shape=[att_shape, jax.ShapeDtypeStruct((batch * keep, 2, HEADS, HEAD_DIM), F32)],
        scratch_shapes=[pltpu.VMEM((D_MODEL, QKV_COL_TILE), BF16),
                        pltpu.VMEM((ROW_TILE, QKV_COL_TILE), F32),
                        pltpu.VMEM((ROW_TILE, QKV_COL_TILE), F32),
                        pltpu.VMEM((HEADS_PER_STEP * SLAB_PITCH, HEAD_DIM), F32)],
        compiler_params=_params(("arbitrary",), 48),
        name="qkv",
    )(h, w_qkv, q_gain.reshape(1, HEAD_DIM), k_gain.reshape(1, HEAD_DIM), cos, sin)


def _band_mask(n_keys):
    qi = lax.broadcasted_iota(jnp.int32, (ATTN_BLOCK, n_keys), 0)
    kj = lax.broadcasted_iota(jnp.int32, (ATTN_BLOCK, n_keys), 1)
    dist = (n_keys - ATTN_BLOCK) + qi - kj
    return (dist >= 0) & (dist <= ATTN_BLOCK)


def _attn_body(*refs, seq):
    qkv_refs, o_ref, og_ref, lg_ref = refs[:N_GROUPS], refs[N_GROUPS], refs[-2], refs[-1]
    masks = {n: _band_mask(n) for n in (ATTN_BLOCK, 2 * ATTN_BLOCK)}
    for gi, (_, dil) in enumerate(ATTN_GROUPS):
        qkv_ref = qkv_refs[gi]
        Q, K, V = 0, 1, 2
        n_k = ROW_TILE // dil
        n_blocks = seq // dil // ATTN_BLOCK

        def rows_of(role, r, blk):
            if n_k >= ATTN_BLOCK:
                per_tile = n_k // ATTN_BLOCK
                start = (blk // per_tile) * ROW_TILE + r * n_k + (blk % per_tile) * ATTN_BLOCK
                return qkv_ref[role, 0, start:start + ATTN_BLOCK, :]
            parts = [qkv_ref[role, 0, tl * ROW_TILE + r * n_k:tl * ROW_TILE + (r + 1) * n_k, :]
                     for tl in range(blk * (ATTN_BLOCK // n_k), (blk + 1) * (ATTN_BLOCK // n_k))]
            return jnp.concatenate(parts, axis=0)

        def keys_of(role, r, blk):
            if blk == 0:
                return rows_of(role, r, 0)
            return jnp.concatenate([rows_of(role, r, blk - 1), rows_of(role, r, blk)], axis=0)

        todo = [(r, blk) for r in range(dil) for blk in range(n_blocks)]
        for i in range(0, len(todo), ATTN_INTERLEAVE):
            batch = todo[i:i + ATTN_INTERLEAVE]
            ks = [keys_of(K, r, blk) for r, blk in batch]
            ss = [jnp.where(masks[k.shape[0]], _dot_nt(rows_of(Q, r, blk), k), NEG)
                  for (r, blk), k in zip(batch, ks)]
            mxs = [jnp.max(s, axis=-1, keepdims=True) for s in ss]
            ps = [jnp.exp(s - mx) for s, mx in zip(ss, mxs)]
            dens = [jnp.sum(p, axis=-1, keepdims=True) for p in ps]
            accs = [_dot(p.astype(BF16), keys_of(V, r, blk)) for p, (r, blk) in zip(ps, batch)]
            for (r, blk), acc, mx, den in zip(batch, accs, mxs, dens):
                if dil == 1:
                    dst = pl.ds(blk * ATTN_BLOCK, ATTN_BLOCK)
                else:
                    dst = pl.ds(blk * ATTN_BLOCK * dil + r, ATTN_BLOCK, stride=dil)
                og_ref[gi, dst, :] = acc / den
                lg_ref[gi, dst, :] = jnp.broadcast_to(mx + jnp.log(den), (ATTN_BLOCK, HEAD_DIM))
    for blk in range(seq // ATTN_BLOCK):
        rows = slice(blk * ATTN_BLOCK, (blk + 1) * ATTN_BLOCK)
        ls = [lg_ref[gi, rows, :] for gi in range(N_GROUPS)]
        mx = functools.reduce(jnp.maximum, ls)
        es = [jnp.exp(l - mx) for l in ls]
        num = sum(e * og_ref[gi, rows, :] for gi, e in enumerate(es))
        o_ref[rows, :] = (num / sum(es)).astype(BF16)


def _attn(qkvs, batch, seq):
    flat = list(qkvs)
    blk = pl.BlockSpec((3, 1, seq, HEAD_DIM), lambda b, hd: (0, hd, b, 0))
    return pl.pallas_call(
        functools.partial(_attn_body, seq=seq),
        grid=(batch, HEADS),
        in_specs=[blk] * len(flat),
        out_specs=pl.BlockSpec((seq, HEAD_DIM), lambda b, hd: (b, hd)),
        out_shape=jax.ShapeDtypeStruct((batch * seq, ATTN_W), BF16),
        scratch_shapes=[pltpu.VMEM((N_GROUPS, seq, HEAD_DIM), F32)] * 2,
        compiler_params=_params(("arbitrary", "arbitrary"), 40),
        name="attn",
    )(*flat)


def _out_proj_body(x_ref, o_ref, w_ref, y_ref, wbf_ref):
    @pl.when(pl.program_id(0) == 0)
    def _():
        wbf_ref[...] = w_ref[...].astype(BF16)

    y_ref[...] = x_ref[...] + _dot(o_ref[...], wbf_ref[...])


def _out_proj(x, o, w_out):
    m = x.shape[0]
    tm = min(m, ROW_TILE)
    return pl.pallas_call(
        _out_proj_body,
        grid=(m // tm,),
        in_specs=[pl.BlockSpec((tm, D_MODEL), lambda i: (i, 0)),
                  pl.BlockSpec((tm, ATTN_W), lambda i: (i, 0)),
                  pl.BlockSpec((ATTN_W, D_MODEL), lambda i: (0, 0), pipeline_mode=pl.Buffered(1))],
        out_specs=pl.BlockSpec((tm, D_MODEL), lambda i: (i, 0)),
        out_shape=jax.ShapeDtypeStruct((m, D_MODEL), F32),
        scratch_shapes=[pltpu.VMEM((ATTN_W, D_MODEL), BF16)],
        compiler_params=_params(("arbitrary",), 48),
        name="attn_out_proj",
    )(x, o, w_out)


def _qkv_rows_body(x_ref, g_ref, w_ref, y_ref):
    h = _rms(x_ref[...], g_ref[...]).astype(BF16)
    y_ref[...] = _dot(h, w_ref[...].astype(BF16))


def _qkv_rows(x, g, w_qkv):
    m = x.shape[0]
    n = w_qkv.shape[1]
    return pl.pallas_call(
        _qkv_rows_body,
        grid=(n // QKV_COL_TILE,),
        in_specs=[pl.BlockSpec((m, D_MODEL), lambda c: (0, 0)),
                  pl.BlockSpec((1, D_MODEL), lambda c: (0, 0)),
                  pl.BlockSpec((D_MODEL, QKV_COL_TILE), lambda c: (0, c))],
        out_specs=pl.BlockSpec((m, QKV_COL_TILE), lambda c: (0, c)),
        out_shape=jax.ShapeDtypeStruct((m, n), F32),
        compiler_params=_params(("arbitrary",), 32),
        name="qkv_rows",
    )(x, g.reshape(1, D_MODEL), w_qkv)


def _cache_attn_body(y_ref, c0_ref, c1_ref, c2_ref, qn_ref, kn_ref, cos_ref, sin_ref,
                     o_ref, n0_ref, n1_ref, n2_ref, *, t_len):
    cache_refs = (c0_ref, c1_ref, c2_ref)
    new_refs = (n0_ref, n1_ref, n2_ref)

    def norm_rope(val, gain_ref, t):
        return _rope(_rms(val, gain_ref[...]), cos_ref[t:t + 1, :], sin_ref[t:t + 1, :])

    k_new = [[norm_rope(y_ref[0, t, 1, gi], kn_ref, t) for t in range(t_len)]
             for gi in range(N_GROUPS)]
    v_new = [[y_ref[0, t, 2, gi] for t in range(t_len)] for gi in range(N_GROUPS)]
    for gi in range(N_GROUPS):
        for t in range(t_len):
            new_refs[gi][0, t, 0] = k_new[gi][t]
            new_refs[gi][0, t, 1] = v_new[gi][t]

    cached_pos = lax.broadcasted_iota(jnp.int32, (ATTN_BLOCK, 1, 1), 0)
    for t in range(t_len):
        outs, lses = [], []
        for gi, (_, dil) in enumerate(ATTN_GROUPS):
            c_ref = cache_refs[gi]
            q = norm_rope(y_ref[0, t, 0, gi], qn_ref, t) * ATTN_SCALE
            if dil == 1:
                kc, vc = c_ref[0, :, 0], c_ref[0, :, 1]
                new_rows = range(t + 1)
            else:
                kc, vc = c_ref[0, :, t, 0], c_ref[0, :, t, 1]
                new_rows = [t]
            s_c = jnp.sum(kc * q[None], axis=-1, keepdims=True)
            if dil == 1:
                s_c = jnp.where(cached_pos >= t, s_c, NEG)
            s_n = [jnp.sum(k_new[gi][tt] * q, axis=-1, keepdims=True) for tt in new_rows]
            mx = functools.reduce(jnp.maximum, s_n, jnp.max(s_c, axis=0))
            p_c = jnp.exp(s_c - mx[None])
            p_n = [jnp.exp(sn - mx) for sn in s_n]
            den = jnp.sum(p_c, axis=0) + sum(p_n)
            acc = jnp.sum(p_c * vc, axis=0) + sum(p * v_new[gi][tt] for p, tt in zip(p_n, new_rows))
            outs.append(acc / den)
            lses.append(mx + jnp.log(den))
        lmax = functools.reduce(jnp.maximum, lses)
        es = [jnp.exp(l - lmax) for l in lses]
        o_ref[0, t] = sum(e * o for e, o in zip(es, outs)) / sum(es)


def _cache_attn(y, caches, q_gain, k_gain, cos, sin, batch, t_len):
    y6 = y.reshape(batch, t_len, 3, N_GROUPS, HEADS, HEAD_DIM)
    views, specs = [], []
    for cache, (window, dil) in zip(caches, ATTN_GROUPS):
        assert cache.shape[1] == window and window // dil == ATTN_BLOCK and (dil == 1 or t_len <= dil)
        if dil == 1:
            views.append(cache)
            specs.append(pl.BlockSpec((1, ATTN_BLOCK, 2, HEADS, HEAD_DIM), lambda b: (b, 0, 0, 0, 0)))
        else:
            views.append(cache.reshape(batch, ATTN_BLOCK, dil, 2, HEADS, HEAD_DIM))
            specs.append(pl.BlockSpec((1, ATTN_BLOCK, t_len, 2, HEADS, HEAD_DIM),
                                      lambda b: (b, 0, 0, 0, 0, 0)))
    new_shape = jax.ShapeDtypeStruct((batch, t_len, 2, HEADS, HEAD_DIM), F32)
    new_spec = pl.BlockSpec((1, t_len, 2, HEADS, HEAD_DIM), lambda b: (b, 0, 0, 0, 0))
    vec = pl.BlockSpec((1, HEAD_DIM), lambda b: (0, 0))
    tab = pl.BlockSpec((t_len, HEAD_DIM), lambda b: (0, 0))
    return pl.pallas_call(
        functools.partial(_cache_attn_body, t_len=t_len),
        grid=(batch,),
        in_specs=[pl.BlockSpec((1, t_len, 3, N_GROUPS, HEADS, HEAD_DIM), lambda b: (b, 0, 0, 0, 0, 0))]
        + specs + [vec, vec, tab, tab],
        out_specs=[pl.BlockSpec((1, t_len, HEADS, HEAD_DIM), lambda b: (b, 0, 0, 0))] + [new_spec] * N_GROUPS,
        out_shape=[jax.ShapeDtypeStruct((batch, t_len, HEADS, HEAD_DIM), F32)] + [new_shape] * N_GROUPS,
        compiler_params=_params(("arbitrary",), 56),
        name="cache_attn",
    )(y6, *views, q_gain.reshape(1, HEAD_DIM), k_gain.reshape(1, HEAD_DIM), cos, sin)


def kernel(x_prompt, x_sample, state_pool, cache_kv_g0, cache_kv_g1, cache_kv_g2, norm_ffn1, ffn1_w_in, ffn1_w_out, norm_mix, norm_ffn2, ffn2_w_in, ffn2_w_out, pool_w, pool_scale, chunk_w_in, chunk_v_norm, chunk_w_s, chunk_b_s, chunk_w_out, attn_w_qkv, attn_q_norm, attn_k_norm, attn_w_out):
    caches = (cache_kv_g0, cache_kv_g1, cache_kv_g2)
    batch, seq, _ = x_prompt.shape
    dec_batch, dec_seq, _ = x_sample.shape
    depth = norm_ffn1.shape[0]
    xp = x_prompt.reshape(batch * seq, D_MODEL)
    xs = x_sample.reshape(dec_batch * dec_seq, D_MODEL)
    pool_p, pool_s, chunk_s = [], [], []
    kv_p = [[] for _ in ATTN_GROUPS]
    kv_s = [[] for _ in ATTN_GROUPS]
    half_steps = [(w_in, w_out, i) for i in range(depth)
                  for w_in, w_out in ((ffn1_w_in, ffn1_w_out), (ffn2_w_in, ffn2_w_out))]
    w_bf = (ffn1_w_in[0].astype(BF16), ffn1_w_out[0].astype(BF16))

    def ffn(xp, xs, g, w_bf, step):
        nxt = half_steps[step + 1] if step + 1 < len(half_steps) else None
        out = _ffn(xp, xs, g, *w_bf, nxt=nxt)
        return out[0], out[1], tuple(out[2:])

    for i in range(depth):
        kind, j = i % 3, i // 3
        xp, xs, w_bf = ffn(xp, xs, norm_ffn1[i], w_bf, 2 * i)
        if kind == 0:
            zero_buf = jnp.zeros((batch, POOL_BUF, D_MODEL), F32)
            yp, st_p = _pool(xp.reshape(batch, seq, D_MODEL), zero_buf, norm_mix[i], pool_w[j],
                             pool_scale[j], 0)
            ys, st_s = _pool(xs.reshape(dec_batch, dec_seq, D_MODEL), state_pool[j], norm_mix[i],
                             pool_w[j], pool_scale[j], PAST_LEN)
            xp = yp.reshape(batch * seq, D_MODEL)
            xs = ys.reshape(dec_batch * dec_seq, D_MODEL)
            pool_p.append(st_p)
            pool_s.append(st_s)
        elif kind == 1:
            args = (norm_mix[i], chunk_w_in[j], chunk_v_norm[j], chunk_w_s[j], chunk_b_s[j],
                    chunk_w_out[j])
            (xp,) = _chunk(xp, *args, emit_v=False)
            xs_pad = jnp.pad(xs.reshape(dec_batch, dec_seq, D_MODEL),
                             ((0, 0), (0, CHUNK - dec_seq), (0, 0)))
            ys_pad, v_pad = _chunk(xs_pad.reshape(dec_batch * CHUNK, D_MODEL), *args, emit_v=True)
            xs = ys_pad.reshape(dec_batch, CHUNK, D_MODEL)[:, :dec_seq].reshape(-1, D_MODEL)
            chunk_s.append(v_pad.reshape(dec_batch, CHUNK, D_MODEL)[:, :dec_seq])
        else:
            hs = _norm_perm(xp, norm_mix[i])
            qkvs = []
            for gi, (window, dil) in enumerate(ATTN_GROUPS):
                pos = (np.arange(seq // ROW_TILE)[:, None] * ROW_TILE + _tile_perm(dil)[None, :]).reshape(-1)
                cos_p, sin_p = _rope_tables(jnp.asarray(pos))
                qkv, kv_final = _qkv(hs[gi], attn_w_qkv[j], attn_q_norm[j], attn_k_norm[j],
                                     cos_p, sin_p, gi, seq)
                qkvs.append(qkv)
                kv_p[gi].append(kv_final.reshape(batch, min(window, seq), 2, HEADS, HEAD_DIM))
            xp = _out_proj(xp, _attn(qkvs, batch, seq), attn_w_out[j])

            cos_s, sin_s = _rope_tables(PAST_LEN + jnp.arange(dec_seq))
            y_s = _qkv_rows(xs, norm_mix[i], attn_w_qkv[j])
            o_s, *new_rows = _cache_attn(y_s, tuple(c[j] for c in caches), attn_q_norm[j],
                                         attn_k_norm[j], cos_s, sin_s, dec_batch, dec_seq)
            for gi in range(N_GROUPS):
                kv_s[gi].append(new_rows[gi])
            xs = _out_proj(xs, o_s.reshape(dec_batch * dec_seq, ATTN_W).astype(BF16), attn_w_out[j])
        xp, xs, w_bf = ffn(xp, xs, norm_ffn2[i], w_bf, 2 * i + 1)
    return (xp.reshape(batch, seq, D_MODEL), xs.reshape(dec_batch, dec_seq, D_MODEL),
            jnp.stack(pool_p), jnp.stack(pool_s), jnp.stack(chunk_s),
            jnp.stack(kv_p[0]), jnp.stack(kv_s[0]), jnp.stack(kv_p[1]), jnp.stack(kv_s[1]),
            jnp.stack(kv_p[2]), jnp.stack(kv_s[2]))
```

```python
import functools

import numpy as np
import jax
import jax.numpy as jnp
from jax import lax
from jax.experimental import pallas as pl
from jax.experimental.pallas import tpu as pltpu

F32 = jnp.float32
BF16 = jnp.bfloat16

D_MODEL = 2048
D_FF = 5504
RMS_EPS = 1e-6
POOL_WINDOWS = (2, 4, 8, 16)
POOL_GROUP = D_MODEL // len(POOL_WINDOWS)
POOL_BUF = max(POOL_WINDOWS) - 1
CHUNK = 128
CHUNK_GROUPS = 8
CHUNK_GROUP_W = D_MODEL // CHUNK_GROUPS
ATTN_GROUPS = ((128, 1), (512, 4), (2048, 16))
N_GROUPS = len(ATTN_GROUPS)
HEADS = 16
HEAD_DIM = 128
ATTN_W = HEADS * HEAD_DIM
ROPE_THETA = 10000.0
ATTN_SCALE = HEAD_DIM ** -0.5
NEG = float(np.finfo(np.float32).min)
PAST_LEN = 16384

LANE = 128
SUBLANE = 8
FF_TILE = 512
N_FF_TILES = -(-D_FF // FF_TILE)
FFN_ROW_TILE = 1024
CONVERT_IN_ROWS = 32
CONVERT_OUT_ROWS = 64
ROW_TILE = 512
CHUNK_ROW_TILE = 256
POOL_HALO = SUBLANE * len(POOL_WINDOWS)
ATTN_BLOCK = 128
ATTN_HEADS_PER_STEP = 2
ATTN_INTERLEAVE = 8
QKV_COL_TILE = 1024
HEADS_PER_STEP = QKV_COL_TILE // HEAD_DIM
COL_TILES_PER_ROLE = ATTN_W // QKV_COL_TILE
SLAB_PITCH = ROW_TILE + SUBLANE
MIB = 1024 * 1024


def _params(semantics, vmem_mib):
    return pltpu.CompilerParams(dimension_semantics=semantics, vmem_limit_bytes=vmem_mib * MIB)


def _rms(x, g):
    ms = jnp.mean(x * x, axis=-1, keepdims=True)
    return x * lax.rsqrt(ms + RMS_EPS) * g


def _dot(a, b):
    return jnp.dot(a, b, preferred_element_type=F32)


def _div(x, d):
    return lax.shift_right_logical(x, d.bit_length() - 1) if d & (d - 1) == 0 else x // d


def _mod(x, d):
    return x & (d - 1) if d & (d - 1) == 0 else x % d


def _dot_nt(a, b):
    return lax.dot_general(a, b, (((1,), (1,)), ((), ())), preferred_element_type=F32)


def _ff_start(f, base=0):
    return (base // LANE + jnp.minimum(f * (FF_TILE // LANE), (D_FF - FF_TILE) // LANE)) * LANE


def _ffn_body(xp_ref, xs_ref, g_ref, wg_ref, wu_ref, wo_ref, *rest, convert_next):
    if convert_next:
        nwi_ref, nwo_ref, yp_ref, ys_ref, nwi_bf_ref, nwo_bf_ref, h_ref = rest
    else:
        yp_ref, ys_ref, h_ref = rest
    m = pl.program_id(0)
    f = pl.program_id(1)
    tm, ms = xp_ref.shape[0], xs_ref.shape[0]

    def init(x_ref, rows, y_ref):
        x = x_ref[...]
        h_ref[rows, :] = _rms(x, g_ref[...]).astype(BF16)
        y_ref[...] = x

    col = lax.broadcasted_iota(jnp.int32, (1, FF_TILE), 1)
    fresh = col >= f * FF_TILE - _ff_start(f)

    def accumulate(n_rows):
        h = h_ref[:n_rows]
        gate = _dot(h, wg_ref[...])
        up = _dot(h, wu_ref[...])
        hid = jnp.where(fresh, gate * jax.nn.sigmoid(gate) * up * 0.5, 0.0).astype(BF16)
        out = _dot(hid, wo_ref[...])
        yp_ref[...] += out[:tm]
        if n_rows > tm:
            ys_ref[...] += out[tm:]
        if convert_next:
            nwi_bf_ref[...] = nwi_ref[...].astype(BF16)
            nwo_bf_ref[...] = nwo_ref[...].astype(BF16)

    @pl.when(f == 0)
    def _():
        init(xp_ref, slice(0, tm), yp_ref)

    @pl.when((f == 0) & (m == 0))
    def _():
        init(xs_ref, slice(tm, tm + ms), ys_ref)

    @pl.when(m == 0)
    def _():
        accumulate(tm + ms)

    @pl.when(m != 0)
    def _():
        accumulate(tm)


def _ffn(xp, xs, g, w_in, w_out, nxt=None):
    mp, ms = xp.shape[0], xs.shape[0]
    n_m = mp // FFN_ROW_TILE
    elem = pl.Element
    in_specs = [
        pl.BlockSpec((FFN_ROW_TILE, D_MODEL), lambda m, f: (m, 0)),
        pl.BlockSpec((ms, D_MODEL), lambda m, f: (0, 0)),
        pl.BlockSpec((1, D_MODEL), lambda m, f: (0, 0)),
        pl.BlockSpec((elem(D_MODEL), elem(FF_TILE)), lambda m, f: (0, _ff_start(f))),
        pl.BlockSpec((elem(D_MODEL), elem(FF_TILE)), lambda m, f: (0, _ff_start(f, D_FF))),
        pl.BlockSpec((elem(FF_TILE), elem(D_MODEL)), lambda m, f: (_ff_start(f), 0)),
    ]
    out_specs = [
        pl.BlockSpec((FFN_ROW_TILE, D_MODEL), lambda m, f: (m, 0)),
        pl.BlockSpec((ms, D_MODEL), lambda m, f: (0, 0)),
    ]
    out_shape = [jax.ShapeDtypeStruct(xp.shape, F32), jax.ShapeDtypeStruct(xs.shape, F32)]
    args = [xp, xs, g.reshape(1, D_MODEL), w_in, w_in, w_out]
    if nxt is not None:
        nwi, nwo, layer = nxt
        n_in, n_out = D_MODEL // CONVERT_IN_ROWS, D_FF // CONVERT_OUT_ROWS
        assert max(n_in, n_out) <= n_m * N_FF_TILES and D_FF % CONVERT_OUT_ROWS == 0
        slab = lambda m, f, n: jnp.minimum(m * N_FF_TILES + f, n - 1)
        in_specs += [
            pl.BlockSpec((None, CONVERT_IN_ROWS, 2 * D_FF), lambda m, f: (layer, slab(m, f, n_in), 0)),
            pl.BlockSpec((None, CONVERT_OUT_ROWS, D_MODEL), lambda m, f: (layer, slab(m, f, n_out), 0)),
        ]
        out_specs += [
            pl.BlockSpec((CONVERT_IN_ROWS, 2 * D_FF), lambda m, f: (slab(m, f, n_in), 0)),
            pl.BlockSpec((CONVERT_OUT_ROWS, D_MODEL), lambda m, f: (slab(m, f, n_out), 0)),
        ]
        out_shape += [jax.ShapeDtypeStruct((D_MODEL, 2 * D_FF), BF16),
                      jax.ShapeDtypeStruct((D_FF, D_MODEL), BF16)]
        args += [nwi, nwo]
    return pl.pallas_call(
        functools.partial(_ffn_body, convert_next=nxt is not None),
        grid=(n_m, N_FF_TILES),
        in_specs=in_specs,
        out_specs=out_specs,
        out_shape=out_shape,
        scratch_shapes=[pltpu.VMEM((FFN_ROW_TILE + ms, D_MODEL), BF16)],
        compiler_params=_params(("arbitrary", "arbitrary"), 62),
        name="ffn",
    )(*args)


def _pool_body(x_ref, buf_ref, g_ref, pw_ref, sc_ref, y_ref, st_ref, hb_ref, sa_ref, sb_ref, *,
               tt, start, nt):
    t = pl.program_id(1)
    halo = POOL_HALO
    n = halo + tt

    @pl.when(t == 0)
    def _():
        hb_ref[0:halo, :] = buf_ref[0]

    x = x_ref[0]
    h = _rms(x, g_ref[...])
    hb_ref[halo:n, :] = h
    refs = (hb_ref, sa_ref, sb_ref, sa_ref, sb_ref)
    for k, w in enumerate(POOL_WINDOWS):
        src, dst = refs[k], refs[k + 1]
        lo, cols = SUBLANE * (k + 1), slice(k * POOL_GROUP, D_MODEL)
        dst[lo:n, cols] = src[lo:n, cols] + src[lo - w // 2:n - w // 2, cols]
    pos = start + t * tt + lax.broadcasted_iota(jnp.int32, (tt, 1), 0)
    for gi, w in enumerate(POOL_WINDOWS):
        sl = slice(gi * POOL_GROUP, (gi + 1) * POOL_GROUP)
        win = refs[gi + 1][halo:n, sl]
        count = jnp.minimum(w, pos + 1).astype(F32)
        pooled = win / count - h[:, sl]
        mixed = _dot(pooled.astype(BF16), pw_ref[gi])
        y_ref[0, :, sl] = x[:, sl] + mixed * sc_ref[:, sl]
    st_ref[0] = hb_ref[n - POOL_BUF:n, :]
    if nt > 1:
        hb_ref[0:halo, :] = hb_ref[tt:n, :]


def _pool(x, buf, g, pool_w, scale, start):
    b, t_len, _ = x.shape
    tt = min(t_len, ROW_TILE)
    nt = t_len // tt
    halo = POOL_HALO
    assert POOL_WINDOWS == tuple(2 ** (k + 1) for k in range(len(POOL_WINDOWS))) and halo > POOL_BUF
    buf16 = jnp.pad(buf, ((0, 0), (halo - POOL_BUF, 0), (0, 0)))
    body = functools.partial(_pool_body, tt=tt, start=start, nt=nt)
    return pl.pallas_call(
        body,
        grid=(b, nt),
        in_specs=[
            pl.BlockSpec((1, tt, D_MODEL), lambda i, t: (i, t, 0)),
            pl.BlockSpec((1, halo, D_MODEL), lambda i, t: (i, 0, 0)),
            pl.BlockSpec((1, D_MODEL), lambda i, t: (0, 0)),
            pl.BlockSpec((len(POOL_WINDOWS), POOL_GROUP, POOL_GROUP), lambda i, t: (0, 0, 0)),
            pl.BlockSpec((1, D_MODEL), lambda i, t: (0, 0)),
        ],
        out_specs=[
            pl.BlockSpec((1, tt, D_MODEL), lambda i, t: (i, t, 0)),
            pl.BlockSpec((1, POOL_BUF, D_MODEL), lambda i, t: (i, 0, 0)),
        ],
        out_shape=[jax.ShapeDtypeStruct(x.shape, F32),
                   jax.ShapeDtypeStruct((b, POOL_BUF, D_MODEL), F32)],
        scratch_shapes=[pltpu.VMEM((halo + tt, D_MODEL), F32)] * 3,
        compiler_params=_params(("arbitrary", "arbitrary"), 40),
        name="pool",
    )(x, buf16, g.reshape(1, D_MODEL), pool_w.astype(BF16), scale.reshape(1, D_MODEL))


def _chunk_body(x_ref, g_ref, win_ref, vg_ref, ws_ref, bst_ref, wout_ref, *rest, tm, emit_v):
    if emit_v:
        y_ref, vn_ref, mix_ref = rest
    else:
        y_ref, mix_ref = rest
    x = x_ref[...]
    h = _rms(x, g_ref[...]).astype(BF16)
    uv = _dot(h, win_ref[...])
    uv = 0.5 * uv * (1.0 + lax.erf(uv * np.float32(np.sqrt(0.5))))
    u = uv[:, :D_MODEL]
    vn = _rms(uv[:, D_MODEL:], vg_ref[...])
    if emit_v:
        vn_ref[...] = vn
    q_idx = lax.broadcasted_iota(jnp.int32, (CHUNK, CHUNK), 0)
    c_idx = lax.broadcasted_iota(jnp.int32, (CHUNK, CHUNK), 1)
    causal = c_idx <= q_idx
    for gi in range(CHUNK_GROUPS):
        cols = slice(gi * CHUNK_GROUP_W, (gi + 1) * CHUNK_GROUP_W)
        ws = jnp.where(causal, ws_ref[gi], 0.0).astype(BF16)
        bias = bst_ref[:, gi:gi + 1]
        for c in range(tm // CHUNK):
            rows = slice(c * CHUNK, (c + 1) * CHUNK)
            mixed = _dot(ws, vn[rows, cols].astype(BF16)) + bias
            mix_ref[rows, cols] = (u[rows, cols] * mixed).astype(BF16)
    y_ref[...] = x + _dot(mix_ref[...], wout_ref[...])


def _chunk(x, g, w_in, v_gain, w_s, b_s, w_out, emit_v):
    m = x.shape[0]
    tm = CHUNK_ROW_TILE
    body = functools.partial(_chunk_body, tm=tm, emit_v=emit_v)
    row_spec = pl.BlockSpec((tm, D_MODEL), lambda i: (i, 0))
    once = pl.Buffered(1)
    out_specs = [row_spec]
    out_shape = [jax.ShapeDtypeStruct((m, D_MODEL), F32)]
    if emit_v:
        out_specs.append(row_spec)
        out_shape.append(jax.ShapeDtypeStruct((m, D_MODEL), F32))
    return pl.pallas_call(
        body,
        grid=(m // tm,),
        in_specs=[
            row_spec,
            pl.BlockSpec((1, D_MODEL), lambda i: (0, 0)),
            pl.BlockSpec((D_MODEL, 2 * D_MODEL), lambda i: (0, 0), pipeline_mode=once),
            pl.BlockSpec((1, D_MODEL), lambda i: (0, 0)),
            pl.BlockSpec((CHUNK_GROUPS, CHUNK, CHUNK), lambda i: (0, 0, 0)),
            pl.BlockSpec((CHUNK, CHUNK_GROUPS), lambda i: (0, 0)),
            pl.BlockSpec((D_MODEL, D_MODEL), lambda i: (0, 0), pipeline_mode=once),
        ],
        out_specs=out_specs,
        out_shape=out_shape,
        scratch_shapes=[pltpu.VMEM((tm, D_MODEL), BF16)],
        compiler_params=_params(("arbitrary",), 56),
        name="chunk",
    )(x, g.reshape(1, D_MODEL), w_in.astype(BF16), v_gain.reshape(1, D_MODEL), w_s, b_s.T,
      w_out.astype(BF16))


def _tile_perm(dil):
    u = np.arange(ROW_TILE)
    n_k = ROW_TILE // dil
    return (u % n_k) * dil + u // n_k


def _rope_tables(pos):
    half = HEAD_DIM // 2
    freqs = ROPE_THETA ** (-2.0 * jnp.arange(half, dtype=F32) / HEAD_DIM)
    ang = pos.astype(F32)[:, None] * freqs[None, :]
    cos, sin = jnp.cos(ang), jnp.sin(ang)
    return jnp.concatenate([cos, cos], axis=-1), jnp.concatenate([-sin, sin], axis=-1)


def _rope(x, cos, sin_signed):
    return x * cos + pltpu.roll(x, HEAD_DIM // 2, 1) * sin_signed


def _norm_perm_body(x_ref, g_ref, *h_refs):
    h = _rms(x_ref[...], g_ref[...]).astype(BF16)
    u = lax.broadcasted_iota(jnp.int32, (ROW_TILE, ROW_TILE), 0)
    t = lax.broadcasted_iota(jnp.int32, (ROW_TILE, ROW_TILE), 1)
    for h_ref, (_, dil) in zip(h_refs, ATTN_GROUPS):
        if dil == 1:
            h_ref[...] = h
        else:
            n_k = ROW_TILE // dil
            src = (u & (n_k - 1)) * dil + lax.shift_right_logical(u, n_k.bit_length() - 1)
            select = jnp.where(t == src, 1.0, 0.0).astype(BF16)
            h_ref[...] = _dot(select, h).astype(BF16)


def _norm_perm(x, g):
    m = x.shape[0]
    row_spec = pl.BlockSpec((ROW_TILE, D_MODEL), lambda i: (i, 0))
    return pl.pallas_call(
        _norm_perm_body,
        grid=(m // ROW_TILE,),
        in_specs=[row_spec, pl.BlockSpec((1, D_MODEL), lambda i: (0, 0))],
        out_specs=[row_spec] * N_GROUPS,
        out_shape=[jax.ShapeDtypeStruct((m, D_MODEL), BF16)] * N_GROUPS,
        compiler_params=_params(("arbitrary",), 40),
        name="norm_perm",
    )(x, g.reshape(1, D_MODEL))


def _qkv_body(h_ref, w_ref, qn_ref, kn_ref, cos_ref, sin_ref, att_ref, kvf_ref,
              wbf_ref, ya_ref, yb_ref, slab_ref, *, dil, final_rows, tiles_per_seq, table_tiles, n_m,
              n_steps):
    p = pl.program_id(0)
    done = jnp.maximum(p - 1, 0)
    role = _div(_div(done, n_m), COL_TILES_PER_ROLE)
    row_tile = _mod(done, n_m)

    @pl.when(p == 0)
    def _():
        yb_ref[...] = jnp.zeros_like(yb_ref)

    @pl.when((_mod(jnp.minimum(p, n_steps - 1), n_m) == 0) & (p < n_steps))
    def _():
        wbf_ref[...] = w_ref[...].astype(BF16)

    def to_slab(hl, val):
        slab_ref[hl * SLAB_PITCH:hl * SLAB_PITCH + ROW_TILE, :] = val

    def gather_final():
        n_k = ROW_TILE // dil
        for t in range(final_rows):
            t_nat = (ROW_TILE - final_rows) + t
            u = (t_nat % dil) * n_k + t_nat // dil
            kvf_ref[t, 0] = slab_ref[pl.ds(u, HEADS_PER_STEP, stride=SLAB_PITCH), :]

    def emit_final():
        if tiles_per_seq == 1:
            gather_final()
        else:
            pl.when(_mod(row_tile, tiles_per_seq) == tiles_per_seq - 1)(gather_final)

    def finish(which, y_ref):
        rows = pl.ds(pl.multiple_of(_mod(row_tile, table_tiles) * ROW_TILE, ROW_TILE), ROW_TILE)
        for hl in range(HEADS_PER_STEP):
            yh = y_ref[:, hl * HEAD_DIM:(hl + 1) * HEAD_DIM]
            if which == 0:
                qn = _rope(_rms(yh, qn_ref[...]), cos_ref[rows, :], sin_ref[rows, :])
                att_ref[0, hl] = (qn * ATTN_SCALE).astype(BF16)
            elif which == 1:
                kn = _rope(_rms(yh, kn_ref[...]), cos_ref[rows, :], sin_ref[rows, :])
                att_ref[0, hl] = kn.astype(BF16)
                to_slab(hl, kn)
            else:
                att_ref[0, hl] = yh.astype(BF16)
                to_slab(hl, yh)
        if which > 0:
            emit_final()

    for parity, (cur_ref, prev_ref) in enumerate(((ya_ref, yb_ref), (yb_ref, ya_ref))):
        for which in range(3):
            @pl.when((_mod(p, 2) == parity) & (role == which))
            def _(which=which, cur_ref=cur_ref, prev_ref=prev_ref):
                cur_ref[...] = _dot(h_ref[...], wbf_ref[...])
                finish(which, prev_ref)


def _qkv(h, w_qkv, q_gain, k_gain, cos, sin, group, seq):
    window, dil = ATTN_GROUPS[group]
    m = h.shape[0]
    n_m = m // ROW_TILE
    tiles_per_seq = seq // ROW_TILE
    keep = min(window, seq)
    final_rows = min(keep, ROW_TILE)
    every_tile = keep == seq
    halves = COL_TILES_PER_ROLE
    n_steps = 3 * halves * n_m

    def project_tile(p):
        q = jnp.minimum(p, n_steps - 1)
        return _div(q, n_m), _mod(q, n_m)

    def finish_tile(p):
        e = jnp.maximum(p - 1, 0)
        return _div(e, n_m), _mod(e, n_m)

    def att_idx(p):
        c, i = finish_tile(p)
        return (_div(c, halves), _mod(c, halves), i, 0)

    def final_idx(p):
        c, i = finish_tile(p)
        s, hh = _div(c, halves), _mod(c, halves)
        rb = i if every_tile else _div(i, tiles_per_seq)
        live = s > 0
        return (jnp.where(live, rb, 0), jnp.where(live, s - 1, 0), jnp.where(live, hh, 0), 0)

    def w_idx(p):
        c, _ = project_tile(p)
        return (0, (_div(c, halves) * N_GROUPS + group) * halves + _mod(c, halves))

    body = functools.partial(_qkv_body, dil=dil, final_rows=final_rows, n_m=n_m, n_steps=n_steps,
                             tiles_per_seq=1 if every_tile else tiles_per_seq,
                             table_tiles=tiles_per_seq)
    att_shape = jax.ShapeDtypeStruct((3, HEADS, m, HEAD_DIM), BF16)
    att_block = (1, HEADS_PER_STEP, ROW_TILE, HEAD_DIM)
    table = pl.BlockSpec((seq, HEAD_DIM), lambda p: (0, 0))
    batch = m // seq
    return pl.pallas_call(
        body,
        grid=(n_steps + 1,),
        in_specs=[
            pl.BlockSpec((ROW_TILE, D_MODEL), lambda p: (project_tile(p)[1], 0)),
            pl.BlockSpec((D_MODEL, QKV_COL_TILE), w_idx),
            pl.BlockSpec((1, HEAD_DIM), lambda p: (0, 0)),
            pl.BlockSpec((1, HEAD_DIM), lambda p: (0, 0)),
            table,
            table,
        ],
        out_specs=[
            pl.BlockSpec(att_block, att_idx),
            pl.BlockSpec((final_rows, 1, HEADS_PER_STEP, HEAD_DIM), final_idx),
        ],
        out_shape=[att_shape, jax.ShapeDtypeStruct((batch * keep, 2, HEADS, HEAD_DIM), F32)],
        scratch_shapes=[pltpu.VMEM((D_MODEL, QKV_COL_TILE), BF16),
                        pltpu.VMEM((ROW_TILE, QKV_COL_TILE), F32),
                        pltpu.VMEM((ROW_TILE, QKV_COL_TILE), F32),
                        pltpu.VMEM((HEADS_PER_STEP * SLAB_PITCH, HEAD_DIM), F32)],
        compiler_params=_params(("arbitrary",), 48),
        name="qkv",
    )(h, w_qkv, q_gain.reshape(1, HEAD_DIM), k_gain.reshape(1, HEAD_DIM), cos, sin)


def _band_mask(n_keys):
    qi = lax.broadcasted_iota(jnp.int32, (ATTN_BLOCK, n_keys), 0)
    kj = lax.broadcasted_iota(jnp.int32, (ATTN_BLOCK, n_keys), 1)
    dist = (n_keys - ATTN_BLOCK) + qi - kj
    return (dist >= 0) & (dist <= ATTN_BLOCK)


def _attn_body(*refs, seq):
    qkv_refs, o_ref, og_ref, lg_ref = refs[:N_GROUPS], refs[N_GROUPS], refs[-2], refs[-1]
    for hh in range(ATTN_HEADS_PER_STEP):
        _attn_head(qkv_refs, o_ref, og_ref, lg_ref, hh, seq)


def _attn_head(qkv_refs, o_ref, og_ref, lg_ref, hh, seq):
    masks = {n: _band_mask(n) for n in (ATTN_BLOCK, 2 * ATTN_BLOCK)}
    for gi, (_, dil) in enumerate(ATTN_GROUPS):
        qkv_ref = qkv_refs[gi]
        Q, K, V = 0, 1, 2
        n_k = ROW_TILE // dil
        n_blocks = seq // dil // ATTN_BLOCK

        def rows_of(role, r, blk):
            if n_k >= ATTN_BLOCK:
                per_tile = n_k // ATTN_BLOCK
                start = (blk // per_tile) * ROW_TILE + r * n_k + (blk % per_tile) * ATTN_BLOCK
                return qkv_ref[role, hh, start:start + ATTN_BLOCK, :]
            parts = [qkv_ref[role, hh, tl * ROW_TILE + r * n_k:tl * ROW_TILE + (r + 1) * n_k, :]
                     for tl in range(blk * (ATTN_BLOCK // n_k), (blk + 1) * (ATTN_BLOCK // n_k))]
            return jnp.concatenate(parts, axis=0)

        def keys_of(role, r, blk):
            if blk == 0:
                return rows_of(role, r, 0)
            return jnp.concatenate([rows_of(role, r, blk - 1), rows_of(role, r, blk)], axis=0)

        todo = [(r, blk) for r in range(dil) for blk in range(n_blocks)]
        for i in range(0, len(todo), ATTN_INTERLEAVE):
            batch = todo[i:i + ATTN_INTERLEAVE]
            ks = [keys_of(K, r, blk) for r, blk in batch]
            ss = [jnp.where(masks[k.shape[0]], _dot_nt(rows_of(Q, r, blk), k), NEG)
                  for (r, blk), k in zip(batch, ks)]
            mxs = [jnp.max(s, axis=-1, keepdims=True) for s in ss]
            ps = [jnp.exp(s - mx) for s, mx in zip(ss, mxs)]
            dens = [jnp.sum(p, axis=-1, keepdims=True) for p in ps]
            accs = [_dot(p.astype(BF16), keys_of(V, r, blk)) for p, (r, blk) in zip(ps, batch)]
            for (r, blk), acc, mx, den in zip(batch, accs, mxs, dens):
                if dil == 1:
                    dst = pl.ds(blk * ATTN_BLOCK, ATTN_BLOCK)
                else:
                    dst = pl.ds(blk * ATTN_BLOCK * dil + r, ATTN_BLOCK, stride=dil)
                og_ref[gi, dst, :] = acc / den
                lg_ref[gi, dst, :] = jnp.broadcast_to(mx + jnp.log(den), (ATTN_BLOCK, HEAD_DIM))
    for blk in range(seq // ATTN_BLOCK):
        rows = slice(blk * ATTN_BLOCK, (blk + 1) * ATTN_BLOCK)
        ls = [lg_ref[gi, rows, :] for gi in range(N_GROUPS)]
        mx = functools.reduce(jnp.maximum, ls)
        es = [jnp.exp(l - mx) for l in ls]
        num = sum(e * og_ref[gi, rows, :] for gi, e in enumerate(es))
        o_ref[rows, hh * HEAD_DIM:(hh + 1) * HEAD_DIM] = (num / sum(es)).astype(BF16)


def _attn(qkvs, batch, seq):
    flat = list(qkvs)
    n_h = ATTN_HEADS_PER_STEP
    blk = pl.BlockSpec((3, n_h, seq, HEAD_DIM), lambda b, hd: (0, hd, b, 0))
    return pl.pallas_call(
        functools.partial(_attn_body, seq=seq),
        grid=(batch, HEADS // n_h),
        in_specs=[blk] * len(flat),
        out_specs=pl.BlockSpec((seq, n_h * HEAD_DIM), lambda b, hd: (b, hd)),
        out_shape=jax.ShapeDtypeStruct((batch * seq, ATTN_W), BF16),
        scratch_shapes=[pltpu.VMEM((N_GROUPS, seq, HEAD_DIM), F32)] * 2,
        compiler_params=_params(("arbitrary", "arbitrary"), 40),
        name="attn",
    )(*flat)


def _out_proj_body(x_ref, o_ref, w_ref, y_ref, wbf_ref):
    @pl.when(pl.program_id(0) == 0)
    def _():
        wbf_ref[...] = w_ref[...].astype(BF16)

    y_ref[...] = x_ref[...] + _dot(o_ref[...], wbf_ref[...])


def _out_proj(x, o, w_out):
    m = x.shape[0]
    tm = min(m, ROW_TILE)
    return pl.pallas_call(
        _out_proj_body,
        grid=(m // tm,),
        in_specs=[pl.BlockSpec((tm, D_MODEL), lambda i: (i, 0)),
                  pl.BlockSpec((tm, ATTN_W), lambda i: (i, 0)),
                  pl.BlockSpec((ATTN_W, D_MODEL), lambda i: (0, 0), pipeline_mode=pl.Buffered(1))],
        out_specs=pl.BlockSpec((tm, D_MODEL), lambda i: (i, 0)),
        out_shape=jax.ShapeDtypeStruct((m, D_MODEL), F32),
        scratch_shapes=[pltpu.VMEM((ATTN_W, D_MODEL), BF16)],
        compiler_params=_params(("arbitrary",), 48),
        name="attn_out_proj",
    )(x, o, w_out)


def _qkv_rows_body(x_ref, g_ref, w_ref, y_ref):
    h = _rms(x_ref[...], g_ref[...]).astype(BF16)
    y_ref[...] = _dot(h, w_ref[...].astype(BF16))


def _qkv_rows(x, g, w_qkv):
    m = x.shape[0]
    n = w_qkv.shape[1]
    return pl.pallas_call(
        _qkv_rows_body,
        grid=(n // QKV_COL_TILE,),
        in_specs=[pl.BlockSpec((m, D_MODEL), lambda c: (0, 0)),
                  pl.BlockSpec((1, D_MODEL), lambda c: (0, 0)),
                  pl.BlockSpec((D_MODEL, QKV_COL_TILE), lambda c: (0, c))],
        out_specs=pl.BlockSpec((m, QKV_COL_TILE), lambda c: (0, c)),
        out_shape=jax.ShapeDtypeStruct((m, n), F32),
        compiler_params=_params(("arbitrary",), 32),
        name="qkv_rows",
    )(x, g.reshape(1, D_MODEL), w_qkv)


def _cache_attn_body(y_ref, c0_ref, c1_ref, c2_ref, qn_ref, kn_ref, cos_ref, sin_ref,
                     o_ref, n0_ref, n1_ref, n2_ref, *, t_len):
    cache_refs = (c0_ref, c1_ref, c2_ref)
    new_refs = (n0_ref, n1_ref, n2_ref)

    def norm_rope(val, gain_ref, t):
        return _rope(_rms(val, gain_ref[...]), cos_ref[t:t + 1, :], sin_ref[t:t + 1, :])

    k_new = [[norm_rope(y_ref[0, t, 1, gi], kn_ref, t) for t in range(t_len)]
             for gi in range(N_GROUPS)]
    v_new = [[y_ref[0, t, 2, gi] for t in range(t_len)] for gi in range(N_GROUPS)]
    for gi in range(N_GROUPS):
        for t in range(t_len):
            new_refs[gi][0, t, 0] = k_new[gi][t]
            new_refs[gi][0, t, 1] = v_new[gi][t]

    cached_pos = lax.broadcasted_iota(jnp.int32, (ATTN_BLOCK, 1, 1), 0)
    for t in range(t_len):
        outs, lses = [], []
        for gi, (_, dil) in enumerate(ATTN_GROUPS):
            c_ref = cache_refs[gi]
            q = norm_rope(y_ref[0, t, 0, gi], qn_ref, t) * ATTN_SCALE
            if dil == 1:
                kc, vc = c_ref[0, :, 0], c_ref[0, :, 1]
                new_rows = range(t + 1)
            else:
                kc, vc = c_ref[0, :, t, 0], c_ref[0, :, t, 1]
                new_rows = [t]
            s_c = jnp.sum(kc * q[None], axis=-1, keepdims=True)
            if dil == 1:
                s_c = jnp.where(cached_pos >= t, s_c, NEG)
            s_n = [jnp.sum(k_new[gi][tt] * q, axis=-1, keepdims=True) for tt in new_rows]
            mx = functools.reduce(jnp.maximum, s_n, jnp.max(s_c, axis=0))
            p_c = jnp.exp(s_c - mx[None])
            p_n = [jnp.exp(sn - mx) for sn in s_n]
            den = jnp.sum(p_c, axis=0) + sum(p_n)
            acc = jnp.sum(p_c * vc, axis=0) + sum(p * v_new[gi][tt] for p, tt in zip(p_n, new_rows))
            outs.append(acc / den)
            lses.append(mx + jnp.log(den))
        lmax = functools.reduce(jnp.maximum, lses)
        es = [jnp.exp(l - lmax) for l in lses]
        o_ref[0, t] = sum(e * o for e, o in zip(es, outs)) / sum(es)


def _cache_attn(y, caches, q_gain, k_gain, cos, sin, batch, t_len):
    y6 = y.reshape(batch, t_len, 3, N_GROUPS, HEADS, HEAD_DIM)
    views, specs = [], []
    for cache, (window, dil) in zip(caches, ATTN_GROUPS):
        assert cache.shape[1] == window and window // dil == ATTN_BLOCK and (dil == 1 or t_len <= dil)
        if dil == 1:
            views.append(cache)
            specs.append(pl.BlockSpec((1, ATTN_BLOCK, 2, HEADS, HEAD_DIM), lambda b: (b, 0, 0, 0, 0)))
        else:
            views.append(cache.reshape(batch, ATTN_BLOCK, dil, 2, HEADS, HEAD_DIM))
            specs.append(pl.BlockSpec((1, ATTN_BLOCK, t_len, 2, HEADS, HEAD_DIM),
                                      lambda b: (b, 0, 0, 0, 0, 0)))
    new_shape = jax.ShapeDtypeStruct((batch, t_len, 2, HEADS, HEAD_DIM), F32)
    new_spec = pl.BlockSpec((1, t_len, 2, HEADS, HEAD_DIM), lambda b: (b, 0, 0, 0, 0))
    vec = pl.BlockSpec((1, HEAD_DIM), lambda b: (0, 0))
    tab = pl.BlockSpec((t_len, HEAD_DIM), lambda b: (0, 0))
    return pl.pallas_call(
        functools.partial(_cache_attn_body, t_len=t_len),
        grid=(batch,),
        in_specs=[pl.BlockSpec((1, t_len, 3, N_GROUPS, HEADS, HEAD_DIM), lambda b: (b, 0, 0, 0, 0, 0))]
        + specs + [vec, vec, tab, tab],
        out_specs=[pl.BlockSpec((1, t_len, HEADS, HEAD_DIM), lambda b: (b, 0, 0, 0))] + [new_spec] * N_GROUPS,
        out_shape=[jax.ShapeDtypeStruct((batch, t_len, HEADS, HEAD_DIM), F32)] + [new_shape] * N_GROUPS,
        compiler_params=_params(("arbitrary",), 56),
        name="cache_attn",
    )(y6, *views, q_gain.reshape(1, HEAD_DIM), k_gain.reshape(1, HEAD_DIM), cos, sin)


def kernel(x_prompt, x_sample, state_pool, cache_kv_g0, cache_kv_g1, cache_kv_g2, norm_ffn1, ffn1_w_in, ffn1_w_out, norm_mix, norm_ffn2, ffn2_w_in, ffn2_w_out, pool_w, pool_scale, chunk_w_in, chunk_v_norm, chunk_w_s, chunk_b_s, chunk_w_out, attn_w_qkv, attn_q_norm, attn_k_norm, attn_w_out):
    caches = (cache_kv_g0, cache_kv_g1, cache_kv_g2)
    batch, seq, _ = x_prompt.shape
    dec_batch, dec_seq, _ = x_sample.shape
    depth = norm_ffn1.shape[0]
    xp = x_prompt.reshape(batch * seq, D_MODEL)
    xs = x_sample.reshape(dec_batch * dec_seq, D_MODEL)
    pool_p, pool_s, chunk_s = [], [], []
    kv_p = [[] for _ in ATTN_GROUPS]
    kv_s = [[] for _ in ATTN_GROUPS]
    half_steps = [(w_in, w_out, i) for i in range(depth)
                  for w_in, w_out in ((ffn1_w_in, ffn1_w_out), (ffn2_w_in, ffn2_w_out))]
    w_bf = (ffn1_w_in[0].astype(BF16), ffn1_w_out[0].astype(BF16))

    def ffn(xp, xs, g, w_bf, step):
        nxt = half_steps[step + 1] if step + 1 < len(half_steps) else None
        out = _ffn(xp, xs, g, *w_bf, nxt=nxt)
        return out[0], out[1], tuple(out[2:])

    for i in range(depth):
        kind, j = i % 3, i // 3
        xp, xs, w_bf = ffn(xp, xs, norm_ffn1[i], w_bf, 2 * i)
        if kind == 0:
            zero_buf = jnp.zeros((batch, POOL_BUF, D_MODEL), F32)
            yp, st_p = _pool(xp.reshape(batch, seq, D_MODEL), zero_buf, norm_mix[i], pool_w[j],
                             pool_scale[j], 0)
            ys, st_s = _pool(xs.reshape(dec_batch, dec_seq, D_MODEL), state_pool[j], norm_mix[i],
                             pool_w[j], pool_scale[j], PAST_LEN)
            xp = yp.reshape(batch * seq, D_MODEL)
            xs = ys.reshape(dec_batch * dec_seq, D_MODEL)
            pool_p.append(st_p)
            pool_s.append(st_s)
        elif kind == 1:
            args = (norm_mix[i], chunk_w_in[j], chunk_v_norm[j], chunk_w_s[j], chunk_b_s[j],
                    chunk_w_out[j])
            (xp,) = _chunk(xp, *args, emit_v=False)
            xs_pad = jnp.pad(xs.reshape(dec_batch, dec_seq, D_MODEL),
                             ((0, 0), (0, CHUNK - dec_seq), (0, 0)))
            ys_pad, v_pad = _chunk(xs_pad.reshape(dec_batch * CHUNK, D_MODEL), *args, emit_v=True)
            xs = ys_pad.reshape(dec_batch, CHUNK, D_MODEL)[:, :dec_seq].reshape(-1, D_MODEL)
            chunk_s.append(v_pad.reshape(dec_batch, CHUNK, D_MODEL)[:, :dec_seq])
        else:
            hs = _norm_perm(xp, norm_mix[i])
            qkvs = []
            for gi, (window, dil) in enumerate(ATTN_GROUPS):
                pos = (np.arange(seq // ROW_TILE)[:, None] * ROW_TILE + _tile_perm(dil)[None, :]).reshape(-1)
                cos_p, sin_p = _rope_tables(jnp.asarray(pos))
                qkv, kv_final = _qkv(hs[gi], attn_w_qkv[j], attn_q_norm[j], attn_k_norm[j],
                                     cos_p, sin_p, gi, seq)
                qkvs.append(qkv)
                kv_p[gi].append(kv_final.reshape(batch, min(window, seq), 2, HEADS, HEAD_DIM))
            xp = _out_proj(xp, _attn(qkvs, batch, seq), attn_w_out[j])

            cos_s, sin_s = _rope_tables(PAST_LEN + jnp.arange(dec_seq))
            y_s = _qkv_rows(xs, norm_mix[i], attn_w_qkv[j])
            o_s, *new_rows = _cache_attn(y_s, tuple(c[j] for c in caches), attn_q_norm[j],
                                         attn_k_norm[j], cos_s, sin_s, dec_batch, dec_seq)
            for gi in range(N_GROUPS):
                kv_s[gi].append(new_rows[gi])
            xs = _out_proj(xs, o_s.reshape(dec_batch * dec_seq, ATTN_W).astype(BF16), attn_w_out[j])
        xp, xs, w_bf = ffn(xp, xs, norm_ffn2[i], w_bf, 2 * i + 1)
    return (xp.reshape(batch, seq, D_MODEL), xs.reshape(dec_batch, dec_seq, D_MODEL),
            jnp.stack(pool_p), jnp.stack(pool_s), jnp.stack(chunk_s),
            jnp.stack(kv_p[0]), jnp.stack(kv_s[0]), jnp.stack(kv_p[1]), jnp.stack(kv_s[1]),
            jnp.stack(kv_p[2]), jnp.stack(kv_s[2]))
```
